```python
import jax
import jax.numpy as jnp
from jax import lax
import numpy as np

D_MODEL = 1024
BATCH = 4
SEQ = 4096
DEPTH = 4

HG_HEADS = 8
HG_DK = 128
HG_DV = 128
HG_WIDTH = HG_HEADS * HG_DK
HG_VWIDTH = HG_HEADS * HG_DV
HG_CHUNK = 64
NORM_EPS = 1e-5

RW_HEAD = 64
RW_HEADS = D_MODEL // RW_HEAD
RW_WIDTH = RW_HEADS * RW_HEAD
DECAY_LORA = 64
AAA_LORA = 64
MV_LORA = 32
GATE_LORA = 128
RW_LN_EPS = 64e-5
RW_SPLITS = (RW_WIDTH, RW_WIDTH, RW_WIDTH, DECAY_LORA, DECAY_LORA, AAA_LORA, GATE_LORA)
RW_COLS = 3 * RW_WIDTH + 2 * DECAY_LORA + AAA_LORA + GATE_LORA

IN_SPLITS = (HG_WIDTH, HG_WIDTH, HG_WIDTH, HG_VWIDTH, HG_VWIDTH, RW_COLS, D_MODEL, D_MODEL)
IN_COLS = 3 * HG_WIDTH + 2 * HG_VWIDTH + RW_COLS + 2 * D_MODEL

N_EXPERTS = 32
TOP_K = 4
D_EXPERT = D_MODEL
SWIGLU_ALPHA = 1.702
SWIGLU_LIMIT = 7.0
MOE_BLOCK = 256

DN_ALPHA = (2 * DEPTH) ** 0.25
DN_BETA = (8 * DEPTH) ** -0.25

kernel_name = 'hgrn2_rwkv7_gated_moe_deepnorm_encoder'


def _split(t, sizes):
    idx = np.cumsum(sizes)[:-1].tolist()
    return jnp.split(t, idx, axis=-1)


def _layernorm(x, g, b):
    xf = x.astype(jnp.float32)
    xc = xf - jnp.mean(xf, axis=-1, keepdims=True)
    var = jnp.mean(xc * xc, axis=-1, keepdims=True)
    return (xc * lax.rsqrt(var + NORM_EPS) * g + b).astype(x.dtype)


def _centred_shift(p, mu):
    pp = jnp.pad(p, ((0, 0), (1, 1), (0, 0)))
    nb = 0.5 * (pp[:, :-2] + pp[:, 2:])
    return p + mu * (nb - p)


def _gla_chunked(q, lf, k, v):
    n, s, dk = q.shape
    dv = v.shape[-1]
    c = HG_CHUNK
    nc = s // c

    def chunks(t):
        return t.reshape(n, nc, c, t.shape[-1]).transpose(1, 0, 2, 3)

    lower = jnp.tril(jnp.ones((c, c), dtype=bool))

    def step(state, inp):
        qc, lfc, kc, vc = inp
        b = jnp.cumsum(lfc, axis=1)
        inter = jnp.einsum('nck,nkv->ncv', qc * jnp.exp(b), state)
        diff = b[:, :, None, :] - b[:, None, :, :]
        dec = jnp.exp(jnp.where(lower[None, :, :, None], diff, -jnp.inf))
        scores = jnp.einsum('ntk,ntsk,nsk->nts', qc, dec, kc)
        intra = jnp.einsum('nts,nsv->ntv', scores, vc)
        b_last = b[:, -1]
        new_state = jnp.exp(b_last)[:, :, None] * state + jnp.einsum(
            'nsk,nsv->nkv', kc * jnp.exp(b_last[:, None, :] - b), vc)
        return new_state, inter + intra

    s0 = jnp.zeros((n, dk, dv), jnp.float32)
    _, o = lax.scan(step, s0, (chunks(q), chunks(lf), chunks(k), chunks(v)))
    return o.transpose(1, 0, 2, 3).reshape(n, s, dv)


def _hgrn2_branch(q, f_fwd, f_bwd, i, og, lb, norm_w):
    bsz, s, _ = q.shape

    def heads(t):
        return t.astype(jnp.float32).reshape(bsz, s, HG_HEADS, -1).transpose(0, 2, 1, 3)

    def forget(z):
        f = lb + (1.0 - lb) * jax.nn.sigmoid(z.astype(jnp.float32))
        return heads(jnp.log(f)), heads(1.0 - f)

    lf_f, k_f = forget(f_fwd)
    lf_b, k_b = forget(f_bwd)
    qh, vh = heads(q), heads(i)

    def flip(t):
        return jnp.flip(t, axis=2)

    def both(fw, bw):
        return jnp.stack([fw, flip(bw)]).reshape(2 * bsz * HG_HEADS, s, -1)

    o = _gla_chunked(both(qh, qh), both(lf_f, lf_b), both(k_f, k_b), both(vh, vh))
    o = o.reshape(2, bsz, HG_HEADS, s, HG_DV)
    o = o[0] + flip(o[1])
    o = o * lax.rsqrt(jnp.mean(o * o, axis=-1, keepdims=True) + NORM_EPS) * norm_w
    o = o.transpose(0, 2, 1, 3).reshape(bsz, s, HG_VWIDTH)
    return (o * jax.nn.silu(og.astype(jnp.float32))).astype(q.dtype)


def _rwkv7_branch(prw, x, v_first, v_mix, mu, w0, w_up, a0, a_up, g_up, k_k, k_a, r_k,
                  lnx_w, lnx_b):
    bsz, s, _ = x.shape
    p = _centred_shift(prw.astype(jnp.float32), mu)
    r, k, v, wl_f, wl_b, a_lo, g_lo = _split(p, RW_SPLITS)
    decays = []
    for d, wl in enumerate((wl_f, wl_b)):
        w = -jax.nn.softplus(-(w0[d] + jnp.tanh(wl) @ w_up[d])) - 0.5
        decays.append(jnp.exp(-jnp.exp(w)))
    a = jax.nn.sigmoid(a0 + a_lo @ a_up)
    g = jax.nn.sigmoid(g_lo) @ g_up
    v_raw = v
    if v_mix is not None:
        v_down, v_up, v0 = v_mix
        v = v + (v_first - v) * jax.nn.sigmoid(v0 + (x.astype(jnp.float32) @ v_down) @ v_up)

    def heads(t):
        return t.reshape(bsz, s, RW_HEADS, RW_HEAD)

    kk = heads(k * k_k)
    kk = kk / jnp.maximum(jnp.sqrt(jnp.sum(kk * kk, axis=-1, keepdims=True)), 1e-12)
    k = heads(k * (1.0 + (a - 1.0) * k_a))
    r, v, a = heads(r), heads(v), heads(a)

    def tm(fw, bw):
        return jnp.stack([fw, jnp.flip(bw, axis=1)], axis=0).transpose(2, 0, 1, 3, 4)

    xs = (tm(r, r), tm(heads(decays[0]), heads(decays[1])), tm(k, k), tm(v, v),
          tm(-kk, -kk), tm(kk * a, kk * a))

    def step(st, inp):
        r_t, w_t, k_t, v_t, a_t, b_t = inp
        sa = jnp.einsum('...vk,...k->...v', st, a_t)
        st = (st * w_t[..., None, :] + sa[..., :, None] * b_t[..., None, :]
              + v_t[..., :, None] * k_t[..., None, :])
        return st, jnp.einsum('...vk,...k->...v', st, r_t)

    s0 = jnp.zeros((2, bsz, RW_HEADS, RW_HEAD, RW_HEAD), jnp.float32)
    _, y = lax.scan(step, s0, xs)
    y = (y[:, 0] + jnp.flip(y[:, 1], axis=0)).transpose(1, 0, 2, 3)
    yc = y - jnp.mean(y, axis=-1, keepdims=True)
    yn = yc * lax.rsqrt(jnp.mean(yc * yc, axis=-1, keepdims=True) + RW_LN_EPS)
    yn = yn * lnx_w.reshape(RW_HEADS, RW_HEAD) + lnx_b.reshape(RW_HEADS, RW_HEAD)
    bonus = jnp.sum(r * k * r_k, axis=-1, keepdims=True) * v
    out = (yn + bonus).reshape(bsz, s, RW_WIDTH) * g
    return out.astype(x.dtype), v_raw


def _moe(x2, router_w, router_b, w1, b1, w2, b2):
    t, d = x2.shape
    n_assign = t * TOP_K
    n_blocks = -(-n_assign // MOE_BLOCK) + N_EXPERTS
    n_slots = n_blocks * MOE_BLOCK
    logits = (x2 @ router_w + router_b).astype(jnp.float32)
    top_val, top_idx = lax.top_k(logits, TOP_K)
    gates = jax.nn.softmax(top_val, axis=-1)
    e_flat = top_idx.reshape(-1).astype(jnp.int32)
    tok_flat = jnp.repeat(jnp.arange(t, dtype=jnp.int32), TOP_K)
    g_flat = gates.reshape(-1)
    order = jnp.argsort(e_flat)
    e_s, tok_s, g_s = e_flat[order], tok_flat[order], g_flat[order]
    counts = jnp.bincount(e_flat, length=N_EXPERTS)
    starts = jnp.cumsum(counts) - counts
    padded = (counts + MOE_BLOCK - 1) // MOE_BLOCK * MOE_BLOCK
    pad_ends = jnp.cumsum(padded)
    pad_starts = pad_ends - padded
    dest = pad_starts[e_s] + jnp.arange(n_assign, dtype=jnp.int32) - starts[e_s]
    slot_tok = jnp.zeros((n_slots,), jnp.int32).at[dest].set(tok_s)
    slot_gate = jnp.zeros((n_slots,), jnp.float32).at[dest].set(g_s)
    block_start = jnp.arange(n_blocks, dtype=jnp.int32) * MOE_BLOCK
    block_exp = jnp.minimum(jnp.searchsorted(pad_ends, block_start, side='right'), N_EXPERTS - 1)

    def expert_block(args):
        tok, e = args
        h = x2[tok] @ w1[e] + b1[e]
        glu = jnp.minimum(h[:, ::2], SWIGLU_LIMIT)
        lin = jnp.clip(h[:, 1::2], -SWIGLU_LIMIT, SWIGLU_LIMIT)
        act = glu * jax.nn.sigmoid(SWIGLU_ALPHA * glu) * (lin + 1.0)
        return act @ w2[e] + b2[e]

    y = lax.map(expert_block, (slot_tok.reshape(n_blocks, MOE_BLOCK), block_exp))
    y = y.reshape(n_slots, d).astype(jnp.float32) * slot_gate[:, None]
    return jnp.zeros((t, d), jnp.float32).at[slot_tok].add(y).astype(x2.dtype)


def setup_inputs(seed: int = 0) -> dict:
    key = jax.random.key(seed)
    ks = list(jax.random.split(key, 40))
    L, D = DEPTH, D_MODEL

    def nrm(i, shape, scale):
        return scale * jax.random.normal(ks[i], shape, jnp.float32)

    ramp = -6.5 + 5.0 * jnp.linspace(0.0, 1.0, RW_WIDTH, dtype=jnp.float32) ** 0.85
    return {
        'x': nrm(0, (BATCH, SEQ, D), 1.0),
        'w_in': nrm(1, (L, D, IN_COLS), D ** -0.5),
        'hg_lb_logits': nrm(2, (L, HG_WIDTH), 0.1),
        'hg_norm_w': 1.0 + nrm(3, (L, HG_DV), 0.05),
        'rw_mu': jax.random.uniform(ks[4], (L, RW_COLS), jnp.float32, 0.2, 0.8),
        'rw_w0': ramp + nrm(5, (L, 2, RW_WIDTH), 0.1),
        'rw_w_up': nrm(6, (L, 2, DECAY_LORA, RW_WIDTH), DECAY_LORA ** -0.5),
        'rw_a0': nrm(7, (L, RW_WIDTH), 0.1),
        'rw_a_up': nrm(8, (L, AAA_LORA, RW_WIDTH), AAA_LORA ** -0.5),
        'rw_g_up': nrm(9, (L, GATE_LORA, RW_WIDTH), GATE_LORA ** -0.5),
        'rw_k_k': 0.85 + nrm(10, (L, RW_WIDTH), 0.05),
        'rw_k_a': 1.0 + nrm(11, (L, RW_WIDTH), 0.05),
        'rw_r_k': nrm(12, (L, RW_HEADS, RW_HEAD), 0.1),
        'rw_lnx_w': 1.0 + nrm(13, (L, RW_WIDTH), 0.05),
        'rw_lnx_b': nrm(14, (L, RW_WIDTH), 0.01),
        'rw_v_down': nrm(15, (L - 1, D, MV_LORA), D ** -0.5),
        'rw_v_up': nrm(16, (L - 1, MV_LORA, RW_WIDTH), MV_LORA ** -0.5),
        'rw_v0': nrm(17, (L - 1, RW_WIDTH), 0.1),
        'proj_a': nrm(18, (L, HG_VWIDTH, D), HG_VWIDTH ** -0.5),
        'proj_b': nrm(19, (L, RW_WIDTH, D), RW_WIDTH ** -0.5),
        'w_out': nrm(20, (L, D, D), DN_BETA * D ** -0.5),
        'ln1_g': 1.0 + nrm(21, (L, D), 0.05),
        'ln1_b': nrm(22, (L, D), 0.01),
        'router_w': nrm(23, (L, D, N_EXPERTS), D ** -0.5),
        'router_b': nrm(24, (L, N_EXPERTS), 0.01),
        'moe_w1': nrm(25, (L, N_EXPERTS, D, 2 * D_EXPERT), D ** -0.5),
        'moe_b1': nrm(26, (L, N_EXPERTS, 2 * D_EXPERT), 0.01),
        'moe_w2': nrm(27, (L, N_EXPERTS, D_EXPERT, D), DN_BETA * D_EXPERT ** -0.5),
        'moe_b2': nrm(28, (L, N_EXPERTS, D), 0.01),
        'ln2_g': 1.0 + nrm(29, (L, D), 0.05),
        'ln2_b': nrm(30, (L, D), 0.01),
    }


def reference(x, w_in, hg_lb_logits, hg_norm_w, rw_mu, rw_w0, rw_w_up, rw_a0, rw_a_up,
              rw_g_up, rw_k_k, rw_k_a, rw_r_k, rw_lnx_w, rw_lnx_b, rw_v_down, rw_v_up, rw_v0,
              proj_a, proj_b, w_out, ln1_g, ln1_b, router_w, router_b, moe_w1, moe_b1,
              moe_w2, moe_b2, ln2_g, ln2_b):
    bsz, s, d = x.shape
    lb_all = jnp.cumsum(jax.nn.softmax(hg_lb_logits.astype(jnp.float32), axis=0), axis=0)
    lb_all = lb_all - lb_all[0:1]
    v_first = None
    for l in range(DEPTH):
        p = x @ w_in[l]
        hq, hf_f, hf_b, hi, hog, prw, gate_a, gate_b = _split(p, IN_SPLITS)
        o_a = _hgrn2_branch(hq, hf_f, hf_b, hi, hog, lb_all[l], hg_norm_w[l])
        v_mix = None if l == 0 else (rw_v_down[l - 1], rw_v_up[l - 1], rw_v0[l - 1])
        o_b, v_raw = _rwkv7_branch(prw, x, v_first, v_mix, rw_mu[l], rw_w0[l], rw_w_up[l],
                                   rw_a0[l], rw_a_up[l], rw_g_up[l], rw_k_k[l], rw_k_a[l],
                                   rw_r_k[l], rw_lnx_w[l], rw_lnx_b[l])
        if l == 0:
            v_first = v_raw
        merged = (jax.nn.sigmoid(gate_a) * (o_a @ proj_a[l])
                  + jax.nn.sigmoid(gate_b) * (o_b @ proj_b[l]))
        x = _layernorm(DN_ALPHA * x + merged @ w_out[l], ln1_g[l], ln1_b[l])
        moe_out = _moe(x.reshape(bsz * s, d), router_w[l], router_b[l], moe_w1[l], moe_b1[l],
                       moe_w2[l], moe_b2[l]).reshape(bsz, s, d)
        x = _layernorm(DN_ALPHA * x + moe_out, ln2_g[l], ln2_b[l])
    return x
```

```python
import functools

import jax
import jax.numpy as jnp
from jax import lax
from jax.experimental import pallas as pl
from jax.experimental.pallas import tpu as pltpu

F32 = jnp.float32
BF16 = jnp.bfloat16

D_MODEL = 1024
DEPTH = 4
HG_HEADS = 8
HG_D = 128
HG_WIDTH = HG_HEADS * HG_D
RW_HEAD = 64
RW_HEADS = D_MODEL // RW_HEAD
RW_WIDTH = D_MODEL
DECAY_LORA = 64
AAA_LORA = 64
GATE_LORA = 128
N_EXPERTS = 32
TOP_K = 4
MOE_BLOCK = 256
SWIGLU_ALPHA = 1.702
SWIGLU_LIMIT = 7.0
NORM_EPS = 1e-5
RW_LN_EPS = 64e-5
DN_ALPHA = (2 * DEPTH) ** 0.25

CHUNK = 64
SUB = 16
RW_LANES = 256
VMEM_LIMIT = 56 * 1024 * 1024

_NT = (((1,), (1,)), ((), ()))
_TN = (((0,), (0,)), ((), ()))


def _dot(a, b):
    return jnp.dot(a.astype(BF16), b.astype(BF16), preferred_element_type=F32)


def _dot_nt(a, b):
    return lax.dot_general(a.astype(BF16), b.astype(BF16), _NT, preferred_element_type=F32)


def _dot_tn(a, b):
    return lax.dot_general(a.astype(BF16), b.astype(BF16), _TN, preferred_element_type=F32)


def _dot_f32(a, b):
    return jnp.dot(a, b, preferred_element_type=F32, precision=lax.Precision.HIGHEST)


def _mm_kernel(x_ref, w_ref, o_ref):
    o_ref[...] = _dot(x_ref[...], w_ref[...]).astype(o_ref.dtype)


def _matmul(x, w, *, tm=512, tn=512, out_dtype=F32):
    m, k = x.shape
    n = w.shape[1]
    tm = min(tm, m)
    tn = min(tn, n)
    assert m % tm == 0 and n % tn == 0
    return pl.pallas_call(
        _mm_kernel,
        grid=(n // tn, m // tm),
        in_specs=[pl.BlockSpec((tm, k), lambda j, i: (i, 0)),
                  pl.BlockSpec((k, tn), lambda j, i: (0, j))],
        out_specs=pl.BlockSpec((tm, tn), lambda j, i: (i, j)),
        out_shape=jax.ShapeDtypeStruct((m, n), out_dtype),
        compiler_params=pltpu.CompilerParams(
            dimension_semantics=("arbitrary", "arbitrary"), vmem_limit_bytes=VMEM_LIMIT),
        name="matmul",
    )(x, w)


def _mm_f32_kernel(x_ref, w_ref, b_ref, o_ref):
    o_ref[...] = _dot_f32(x_ref[...], w_ref[...]) + b_ref[...]


def _matmul_f32(x, w, b, *, tm=512):
    m, k = x.shape
    n = w.shape[1]
    return pl.pallas_call(
        _mm_f32_kernel,
        grid=(m // tm,),
        in_specs=[pl.BlockSpec((tm, k), lambda i: (i, 0)),
                  pl.BlockSpec((k, n), lambda i: (0, 0)),
                  pl.BlockSpec((1, n), lambda i: (0, 0))],
        out_specs=pl.BlockSpec((tm, n), lambda i: (i, 0)),
        out_shape=jax.ShapeDtypeStruct((m, n), F32),
        compiler_params=pltpu.CompilerParams(
            dimension_semantics=("arbitrary",), vmem_limit_bytes=VMEM_LIMIT),
        name="matmul_f32",
    )(x, w, b.reshape(1, n))


def _hgrn_kernel(q_ref, z_ref, v_ref, lb_ref, o_ref, st_ref, *, reverse, n_chunks):
    @pl.when(pl.program_id(2) == 0)
    def _():
        st_ref[...] = jnp.zeros_like(st_ref)

    c = CHUNK
    lb = lb_ref[...]
    row = lax.broadcasted_iota(jnp.int32, (c, c), 0)
    col = lax.broadcasted_iota(jnp.int32, (c, c), 1)
    tri = jnp.where((col >= row) if reverse else (col <= row), 1.0, 0.0).astype(F32)
    srow = lax.broadcasted_iota(jnp.int32, (SUB, 1), 0)

    splits = []
    size = c
    while size > SUB:
        for lo in range(0, c, size):
            splits.append((lo, lo + size // 2, lo + size))
        size //= 2

    order = range(n_chunks - 1, -1, -1) if reverse else range(n_chunks)
    for ci in order:
        r0 = ci * c
        z = z_ref[r0:r0 + c, :]
        f = lb + (1.0 - lb) * jax.nn.sigmoid(z)
        lf = jnp.log(f)
        kk = 1.0 - f
        q = q_ref[r0:r0 + c, :]
        v = v_ref[r0:r0 + c, :]
        b = _dot_f32(tri, lf)
        b_end = b[0:1, :] if reverse else b[c - 1:c, :]
        st = st_ref[...]
        o_ref[r0:r0 + c, :] = _dot_nt(q * jnp.exp(b), st)
        st_ref[...] = st * jnp.exp(b_end) + _dot_tn(v, kk * jnp.exp(b_end - b))

        for lo, mid, hi in splits:
            if reverse:
                anc = b[mid:mid + 1, :]
                qs, ks = slice(lo, mid), slice(mid, hi)
            else:
                anc = b[mid - 1:mid, :]
                qs, ks = slice(mid, hi), slice(lo, mid)
            s = _dot_nt(q[qs] * jnp.exp(b[qs] - anc), kk[ks] * jnp.exp(anc - b[ks]))
            o_ref[r0 + qs.start:r0 + qs.stop, :] += _dot(s, v[ks])

        for d0 in range(0, c, SUB):
            qb, bb, kb, vb = (t[d0:d0 + SUB] for t in (q, b, kk, v))
            acc = jnp.zeros((SUB, HG_D), F32)
            for s in range(SUB):
                mask = (srow <= s) if reverse else (srow >= s)
                e = jnp.where(mask, jnp.exp(jnp.minimum(bb - bb[s:s + 1, :], 0.0)), 0.0)
                w = jnp.sum(qb * kb[s:s + 1, :] * e, axis=-1, keepdims=True)
                acc = acc + w * vb[s:s + 1, :]
            o_ref[r0 + d0:r0 + d0 + SUB, :] += acc


def _hgrn_scan(p_hg, lb, *, bsz, seq, reverse, tb=512):
    t = bsz * seq
    tb = min(tb, seq)
    nblk = seq // tb
    h = HG_HEADS
    zoff = 2 * h if reverse else h

    def tix(b, i):
        return b * nblk + (nblk - 1 - i if reverse else i)

    return pl.pallas_call(
        functools.partial(_hgrn_kernel, reverse=reverse, n_chunks=tb // CHUNK),
        grid=(bsz, h, nblk),
        in_specs=[pl.BlockSpec((tb, HG_D), lambda b, hh, i: (tix(b, i), hh)),
                  pl.BlockSpec((tb, HG_D), lambda b, hh, i: (tix(b, i), zoff + hh)),
                  pl.BlockSpec((tb, HG_D), lambda b, hh, i: (tix(b, i), 3 * h + hh)),
                  pl.BlockSpec((1, HG_D), lambda b, hh, i: (0, hh))],
        out_specs=pl.BlockSpec((tb, HG_D), lambda b, hh, i: (tix(b, i), hh)),
        out_shape=jax.ShapeDtypeStruct((t, h * HG_D), F32),
        scratch_shapes=[pltpu.VMEM((HG_D, HG_D), F32)],
        compiler_params=pltpu.CompilerParams(
            dimension_semantics=("arbitrary", "arbitrary", "arbitrary"),
            vmem_limit_bytes=VMEM_LIMIT),
        name="hgrn_bwd" if reverse else "hgrn_fwd",
    )(p_hg, p_hg, p_hg, lb.reshape(1, -1))


def _rwkv_kernel(r_ref, lw_ref, k_ref, v_ref, kk_ref, a_ref, y_ref, g_ref, *, reverse, n_chunks):
    @pl.when(pl.program_id(2) == 0)
    def _():
        g_ref[...] = jnp.zeros_like(g_ref)

    c = CHUNK
    n = RW_LANES
    hg = n // RW_HEAD
    m = hg * c
    row = lax.broadcasted_iota(jnp.int32, (c, c), 0)
    col = lax.broadcasted_iota(jnp.int32, (c, c), 1)
    tri = jnp.where((col >= row) if reverse else (col <= row), 1.0, 0.0).astype(F32)
    stack_mask = (lax.broadcasted_iota(jnp.int32, (m, n), 0) // c
                  == lax.broadcasted_iota(jnp.int32, (m, n), 1) // RW_HEAD)
    rr = lax.broadcasted_iota(jnp.int32, (m, m), 0)
    cc = lax.broadcasted_iota(jnp.int32, (m, m), 1)
    same = (rr // c) == (cc // c)
    tr, tc = rr % c, cc % c
    strict = same & ((tc > tr) if reverse else (tc < tr))
    incl = same & ((tc >= tr) if reverse else (tc <= tr))
    eye = jnp.where(rr == cc, 1.0, 0.0).astype(F32)

    def stack(t):
        return jnp.where(stack_mask, jnp.concatenate([t] * hg, axis=0), 0.0)

    order = range(n_chunks - 1, -1, -1) if reverse else range(n_chunks)
    for ci in order:
        rows = slice(ci * c, (ci + 1) * c)
        r, lw, k, v, kk, a = (ref[rows, :] for ref in (r_ref, lw_ref, k_ref, v_ref, kk_ref, a_ref))
        lp = _dot_f32(tri, lw)
        lp_end = lp[0:1, :] if reverse else lp[c - 1:c, :]
        pinv = jnp.exp(-lp)
        ar = jnp.concatenate([stack(-kk * jnp.exp(lp - lw)), stack(r * jnp.exp(lp))], axis=0)
        bk = jnp.concatenate([stack(kk * a * pinv), stack(k * pinv)], axis=0)
        v_s = stack(v)
        sc = _dot_nt(ar, bk)
        ab = jnp.where(strict, sc[:m, :m], 0.0)
        ak = jnp.where(strict, sc[:m, m:], 0.0)
        rb = jnp.where(incl, sc[m:, :m], 0.0)
        rk = jnp.where(incl, sc[m:, m:], 0.0)
        pw = ab
        tinv = eye + ab
        for _ in range(c.bit_length() - 2):
            pw = _dot(pw, pw)
            tinv = tinv + _dot(tinv, pw)
        g = g_ref[...]
        arg = _dot_nt(ar, g)
        u_s = _dot(tinv, arg[:m] + _dot(ak, v_s))
        uv = jnp.concatenate([u_s, v_s], axis=0)
        y_s = arg[m:] + _dot(jnp.concatenate([rb, rk], axis=1), uv)
        y = y_s[0:c]
        for h in range(1, hg):
            y = y + y_s[h * c:(h + 1) * c]
        y_ref[rows, :] = y
        dec = jnp.exp(lp_end - lp)
        bkp = jnp.concatenate([stack(kk * a * dec), stack(k * dec)], axis=0)
        g_ref[...] = g * jnp.exp(lp_end) + _dot_tn(uv, bkp)


def _rwkv_scan(r, lw, k, v, kk, a, *, bsz, seq, reverse, tb=256):
    t = bsz * seq
    tb = min(tb, seq)
    nblk = seq // tb
    n = RW_LANES

    def imap(b, hh, i):
        return (b * nblk + (nblk - 1 - i if reverse else i), hh)

    spec = pl.BlockSpec((tb, n), imap)
    return pl.pallas_call(
        functools.partial(_rwkv_kernel, reverse=reverse, n_chunks=tb // CHUNK),
        grid=(bsz, RW_WIDTH // n, nblk),
        in_specs=[spec] * 6,
        out_specs=spec,
        out_shape=jax.ShapeDtypeStruct((t, RW_WIDTH), F32),
        scratch_shapes=[pltpu.VMEM((n, n), F32)],
        compiler_params=pltpu.CompilerParams(
            dimension_semantics=("arbitrary", "arbitrary", "arbitrary"),
            vmem_limit_bytes=VMEM_LIMIT),
        name="rwkv_bwd" if reverse else "rwkv_fwd",
    )(r, lw, k, v, kk, a)


def _moe_kernel(be_ref, xs_ref, gate_ref, w1g_ref, w1l_ref, b1g_ref, b1l_ref, w2_ref, b2_ref, o_ref):
    del be_ref
    x = xs_ref[...]
    hg = _dot(x, w1g_ref[0]) + b1g_ref[0]
    hl = _dot(x, w1l_ref[0]) + b1l_ref[0]
    glu = jnp.minimum(hg, SWIGLU_LIMIT)
    lin = jnp.clip(hl, -SWIGLU_LIMIT, SWIGLU_LIMIT)
    act = glu * jax.nn.sigmoid(SWIGLU_ALPHA * glu) * (lin + 1.0)
    y = _dot(act, w2_ref[0]) + b2_ref[0]
    o_ref[...] = y * gate_ref[...]


def _moe_experts(block_exp, xs, slot_gate, w1g, w1l, b1g, b1l, w2, b2):
    n_slots, d = xs.shape
    f = w1g.shape[-1]
    blk = MOE_BLOCK
    n_blocks = n_slots // blk
    wspec = lambda shape: pl.BlockSpec((1,) + shape, lambda i, be: (be[i], 0, 0))
    return pl.pallas_call(
        _moe_kernel,
        grid_spec=pltpu.PrefetchScalarGridSpec(
            num_scalar_prefetch=1,
            grid=(n_blocks,),
            in_specs=[pl.BlockSpec((blk, d), lambda i, be: (i, 0)),
                      pl.BlockSpec((blk, 1), lambda i, be: (i, 0)),
                      wspec((d, f)), wspec((d, f)), wspec((1, f)), wspec((1, f)),
                      wspec((f, d)), wspec((1, d))],
            out_specs=pl.BlockSpec((blk, d), lambda i, be: (i, 0))),
        out_shape=jax.ShapeDtypeStruct((n_slots, d), F32),
        compiler_params=pltpu.CompilerParams(
            dimension_semantics=("arbitrary",), vmem_limit_bytes=VMEM_LIMIT),
        name="moe_experts",
    )(block_exp, xs, slot_gate.reshape(n_slots, 1), w1g, w1l,
      b1g.reshape(N_EXPERTS, 1, f), b1l.reshape(N_EXPERTS, 1, f), w2, b2.reshape(N_EXPERTS, 1, d))


def _layernorm(x, g, b):
    xc = x - jnp.mean(x, axis=-1, keepdims=True)
    var = jnp.mean(xc * xc, axis=-1, keepdims=True)
    return xc * lax.rsqrt(var + NORM_EPS) * g + b


def _centred_shift(p, mu):
    pp = jnp.pad(p, ((0, 0), (1, 1), (0, 0)))
    nb = 0.5 * (pp[:, :-2] + pp[:, 2:])
    return p + mu * (nb - p)


def _rw_heads(t):
    return t.reshape(t.shape[0], RW_HEADS, RW_HEAD)


def _moe(x2, router_w, router_b, w1g, w1l, b1g, b1l, w2, b2):
    t, d = x2.shape
    n_assign = t * TOP_K
    n_blocks = -(-n_assign // MOE_BLOCK) + N_EXPERTS
    n_slots = n_blocks * MOE_BLOCK
    npad = 128 - N_EXPERTS
    logits = _matmul_f32(x2, jnp.pad(router_w, ((0, 0), (0, npad))),
                         jnp.pad(router_b, (0, npad)))[:, :N_EXPERTS]
    top_val, top_idx = lax.top_k(logits, TOP_K)
    gates = jax.nn.softmax(top_val, axis=-1)
    e_flat = top_idx.reshape(-1).astype(jnp.int32)
    tok_flat = jnp.repeat(jnp.arange(t, dtype=jnp.int32), TOP_K)
    g_flat = gates.reshape(-1)
    order = jnp.argsort(e_flat)
    e_s, tok_s, g_s = e_flat[order], tok_flat[order], g_flat[order]
    counts = jnp.bincount(e_flat, length=N_EXPERTS)
    starts = jnp.cumsum(counts) - counts
    padded = (counts + MOE_BLOCK - 1) // MOE_BLOCK * MOE_BLOCK
    pad_ends = jnp.cumsum(padded)
    pad_starts = pad_ends - padded
    dest = (pad_starts[e_s] + jnp.arange(n_assign, dtype=jnp.int32) - starts[e_s]).astype(jnp.int32)
    slot_tok = jnp.zeros((n_slots,), jnp.int32).at[dest].set(tok_s)
    slot_gate = jnp.zeros((n_slots,), F32).at[dest].set(g_s)
    block_start = jnp.arange(n_blocks, dtype=jnp.int32) * MOE_BLOCK
    block_exp = jnp.minimum(jnp.searchsorted(pad_ends, block_start, side='right'),
                            N_EXPERTS - 1).astype(jnp.int32)
    xs = x2.astype(BF16)[slot_tok]
    y = _moe_experts(block_exp, xs, slot_gate, w1g, w1l, b1g, b1l, w2, b2)
    slot_of = jnp.zeros((n_assign,), jnp.int32).at[order].set(dest)
    return jnp.sum(y[slot_of.reshape(t, TOP_K)], axis=1)


def kernel(x, w_in, hg_lb_logits, hg_norm_w, rw_mu, rw_w0, rw_w_up, rw_a0, rw_a_up, rw_g_up,
           rw_k_k, rw_k_a, rw_r_k, rw_lnx_w, rw_lnx_b, rw_v_down, rw_v_up, rw_v0, proj_a, proj_b,
           w_out, ln1_g, ln1_b, router_w, router_b, moe_w1, moe_b1, moe_w2, moe_b2, ln2_g, ln2_b):
    bsz, s, d = x.shape
    t = bsz * s
    scan = dict(bsz=bsz, seq=s)
    lb_all = jnp.cumsum(jax.nn.softmax(hg_lb_logits.astype(F32), axis=0), axis=0)
    lb_all = lb_all - lb_all[0:1]
    hg_cols = 5 * HG_WIDTH
    rw_cols = 3 * RW_WIDTH + 2 * DECAY_LORA + AAA_LORA + GATE_LORA
    x2 = x.reshape(t, d)
    v_first = None
    for l in range(DEPTH):
        w_l = w_in[l].astype(BF16)
        w_rw = w_l[:, hg_cols:hg_cols + rw_cols]
        if l > 0:
            w_rw = jnp.concatenate([w_rw, rw_v_down[l - 1].astype(BF16)], axis=1)
        w_rw = jnp.pad(w_rw, ((0, 0), (0, 3456 - w_rw.shape[1])))
        p_hg = _matmul(x2, w_l[:, :hg_cols], tm=1024, tn=1024)
        p_rw = _matmul(x2, w_rw, tm=1024, tn=1152)
        p_gate = _matmul(x2, w_l[:, hg_cols + rw_cols:], tm=1024, tn=1024)

        o = (_hgrn_scan(p_hg, lb_all[l], reverse=False, **scan)
             + _hgrn_scan(p_hg, lb_all[l], reverse=True, **scan))
        o = o.reshape(t, HG_HEADS, HG_D)
        o = o * lax.rsqrt(jnp.mean(o * o, axis=-1, keepdims=True) + NORM_EPS) * hg_norm_w[l]
        o_a = o.reshape(t, HG_WIDTH) * jax.nn.silu(p_hg[:, 4 * HG_WIDTH:])

        ps = _centred_shift(p_rw[:, :rw_cols].reshape(bsz, s, rw_cols), rw_mu[l]).reshape(t, rw_cols)
        r, k, v = (ps[:, i * RW_WIDTH:(i + 1) * RW_WIDTH] for i in range(3))
        off = 3 * RW_WIDTH
        wl = (ps[:, off:off + DECAY_LORA], ps[:, off + DECAY_LORA:off + 2 * DECAY_LORA])
        off += 2 * DECAY_LORA
        a_lo = ps[:, off:off + AAA_LORA]
        g_lo = ps[:, off + AAA_LORA:off + AAA_LORA + GATE_LORA]
        lws = []
        for dd in range(2):
            w = -jax.nn.softplus(-(rw_w0[l, dd] + _matmul(jnp.tanh(wl[dd]), rw_w_up[l, dd]))) - 0.5
            lws.append(-jnp.exp(w))
        a = jax.nn.sigmoid(rw_a0[l] + _matmul(a_lo, rw_a_up[l]))
        g = _matmul(jax.nn.sigmoid(g_lo), rw_g_up[l])
        if l == 0:
            v_first = v
        else:
            xv = p_rw[:, rw_cols:rw_cols + rw_v_down.shape[-1]]
            v = v + (v_first - v) * jax.nn.sigmoid(rw_v0[l - 1] + _matmul(xv, rw_v_up[l - 1]))
        kk = _rw_heads(k * rw_k_k[l])
        kk = kk / jnp.maximum(jnp.sqrt(jnp.sum(kk * kk, axis=-1, keepdims=True)), 1e-12)
        kk = kk.reshape(t, RW_WIDTH)
        k = k * (1.0 + (a - 1.0) * rw_k_a[l])
        y = (_rwkv_scan(r, lws[0], k, v, kk, a, reverse=False, **scan)
             + _rwkv_scan(r, lws[1], k, v, kk, a, reverse=True, **scan))
        y = _rw_heads(y)
        yc = y - jnp.mean(y, axis=-1, keepdims=True)
        yn = yc * lax.rsqrt(jnp.mean(yc * yc, axis=-1, keepdims=True) + RW_LN_EPS)
        yn = yn * _rw_heads(rw_lnx_w[l][None])[0] + _rw_heads(rw_lnx_b[l][None])[0]
        bonus = jnp.sum(_rw_heads(r) * _rw_heads(k) * rw_r_k[l], axis=-1, keepdims=True) * _rw_heads(v)
        o_b = (yn + bonus).reshape(t, RW_WIDTH) * g

        merged = (jax.nn.sigmoid(p_gate[:, :d]) * _matmul(o_a, proj_a[l])
                  + jax.nn.sigmoid(p_gate[:, d:]) * _matmul(o_b, proj_b[l]))
        x2 = _layernorm(DN_ALPHA * x2 + _matmul(merged, w_out[l]), ln1_g[l], ln1_b[l])

        w1 = moe_w1[l].astype(BF16)
        moe_out = _moe(x2, router_w[l], router_b[l], w1[:, :, 0::2], w1[:, :, 1::2],
                       moe_b1[l][:, 0::2], moe_b1[l][:, 1::2], moe_w2[l].astype(BF16), moe_b2[l])
        x2 = _layernorm(DN_ALPHA * x2 + moe_out, ln2_g[l], ln2_b[l])
    return x2.reshape(bsz, s, d)
```

```python
import functools

import jax
import jax.numpy as jnp
from jax import lax
from jax.experimental import pallas as pl
from jax.experimental.pallas import tpu as pltpu

F32 = jnp.float32
BF16 = jnp.bfloat16

D_MODEL = 1024
DEPTH = 4
HG_HEADS = 8
HG_D = 128
HG_WIDTH = HG_HEADS * HG_D
RW_HEAD = 64
RW_HEADS = D_MODEL // RW_HEAD
RW_WIDTH = D_MODEL
DECAY_LORA = 64
AAA_LORA = 64
GATE_LORA = 128
N_EXPERTS = 32
TOP_K = 4
MOE_BLOCK = 256
SWIGLU_ALPHA = 1.702
SWIGLU_LIMIT = 7.0
NORM_EPS = 1e-5
RW_LN_EPS = 64e-5
DN_ALPHA = (2 * DEPTH) ** 0.25

CHUNK = 64
SUB = 16
RW_LANES = 256
VMEM_LIMIT = 56 * 1024 * 1024

_NT = (((1,), (1,)), ((), ()))
_TN = (((0,), (0,)), ((), ()))


def _dot(a, b):
    return jnp.dot(a.astype(BF16), b.astype(BF16), preferred_element_type=F32)


def _dot_nt(a, b):
    return lax.dot_general(a.astype(BF16), b.astype(BF16), _NT, preferred_element_type=F32)


def _dot_tn(a, b):
    return lax.dot_general(a.astype(BF16), b.astype(BF16), _TN, preferred_element_type=F32)


def _dot_f32(a, b):
    return jnp.dot(a, b, preferred_element_type=F32, precision=lax.Precision.HIGHEST)


def _mm_kernel(x_ref, w_ref, o_ref):
    o_ref[...] = _dot(x_ref[...], w_ref[...]).astype(o_ref.dtype)


def _matmul(x, w, *, tm=512, tn=512, out_dtype=F32):
    m, k = x.shape
    n = w.shape[1]
    tm = min(tm, m)
    tn = min(tn, n)
    assert m % tm == 0 and n % tn == 0
    return pl.pallas_call(
        _mm_kernel,
        grid=(n // tn, m // tm),
        in_specs=[pl.BlockSpec((tm, k), lambda j, i: (i, 0)),
                  pl.BlockSpec((k, tn), lambda j, i: (0, j))],
        out_specs=pl.BlockSpec((tm, tn), lambda j, i: (i, j)),
        out_shape=jax.ShapeDtypeStruct((m, n), out_dtype),
        compiler_params=pltpu.CompilerParams(
            dimension_semantics=("arbitrary", "arbitrary"), vmem_limit_bytes=VMEM_LIMIT),
        name="matmul",
    )(x, w)


def _mm_f32_kernel(x_ref, w_ref, b_ref, o_ref):
    o_ref[...] = _dot_f32(x_ref[...], w_ref[...]) + b_ref[...]


def _matmul_f32(x, w, b, *, tm=512):
    m, k = x.shape
    n = w.shape[1]
    return pl.pallas_call(
        _mm_f32_kernel,
        grid=(m // tm,),
        in_specs=[pl.BlockSpec((tm, k), lambda i: (i, 0)),
                  pl.BlockSpec((k, n), lambda i: (0, 0)),
                  pl.BlockSpec((1, n), lambda i: (0, 0))],
        out_specs=pl.BlockSpec((tm, n), lambda i: (i, 0)),
        out_shape=jax.ShapeDtypeStruct((m, n), F32),
        compiler_params=pltpu.CompilerParams(
            dimension_semantics=("arbitrary",), vmem_limit_bytes=VMEM_LIMIT),
        name="matmul_f32",
    )(x, w, b.reshape(1, n))


def _hgrn_kernel(q_ref, z_ref, v_ref, lb_ref, o_ref, st_ref, *, reverse, n_chunks):
    @pl.when(pl.program_id(2) == 0)
    def _():
        st_ref[...] = jnp.zeros_like(st_ref)

    c = CHUNK
    lb = lb_ref[...]
    row = lax.broadcasted_iota(jnp.int32, (c, c), 0)
    col = lax.broadcasted_iota(jnp.int32, (c, c), 1)
    tri = jnp.where((col >= row) if reverse else (col <= row), 1.0, 0.0).astype(F32)
    srow = lax.broadcasted_iota(jnp.int32, (SUB, 1), 0)

    splits = []
    size = c
    while size > SUB:
        for lo in range(0, c, size):
            splits.append((lo, lo + size // 2, lo + size))
        size //= 2

    order = range(n_chunks - 1, -1, -1) if reverse else range(n_chunks)
    for ci in order:
        r0 = ci * c
        z = z_ref[r0:r0 + c, :]
        f = lb + (1.0 - lb) * jax.nn.sigmoid(z)
        lf = jnp.log(f)
        kk = 1.0 - f
        q = q_ref[r0:r0 + c, :]
        v = v_ref[r0:r0 + c, :]
        b = _dot_f32(tri, lf)
        b_end = b[0:1, :] if reverse else b[c - 1:c, :]
        st = st_ref[...]
        o_ref[r0:r0 + c, :] = _dot_nt(q * jnp.exp(b), st)
        st_ref[...] = st * jnp.exp(b_end) + _dot_tn(v, kk * jnp.exp(b_end - b))

        for lo, mid, hi in splits:
            if reverse:
                anc = b[mid:mid + 1, :]
                qs, ks = slice(lo, mid), slice(mid, hi)
            else:
                anc = b[mid - 1:mid, :]
                qs, ks = slice(mid, hi), slice(lo, mid)
            s = _dot_nt(q[qs] * jnp.exp(b[qs] - anc), kk[ks] * jnp.exp(anc - b[ks]))
            o_ref[r0 + qs.start:r0 + qs.stop, :] += _dot(s, v[ks])

        for d0 in range(0, c, SUB):
            qb, bb, kb, vb = (t[d0:d0 + SUB] for t in (q, b, kk, v))
            acc = jnp.zeros((SUB, HG_D), F32)
            for s in range(SUB):
                mask = (srow <= s) if reverse else (srow >= s)
                e = jnp.where(mask, jnp.exp(jnp.minimum(bb - bb[s:s + 1, :], 0.0)), 0.0)
                w = jnp.sum(qb * kb[s:s + 1, :] * e, axis=-1, keepdims=True)
                acc = acc + w * vb[s:s + 1, :]
            o_ref[r0 + d0:r0 + d0 + SUB, :] += acc


def _hgrn_scan(p_hg, lb, *, bsz, seq, reverse, tb=512):
    t = bsz * seq
    tb = min(tb, seq)
    nblk = seq // tb
    h = HG_HEADS
    zoff = 2 * h if reverse else h

    def tix(b, i):
        return b * nblk + (nblk - 1 - i if reverse else i)

    return pl.pallas_call(
        functools.partial(_hgrn_kernel, reverse=reverse, n_chunks=tb // CHUNK),
        grid=(bsz, h, nblk),
        in_specs=[pl.BlockSpec((tb, HG_D), lambda b, hh, i: (tix(b, i), hh)),
                  pl.BlockSpec((tb, HG_D), lambda b, hh, i: (tix(b, i), zoff + hh)),
                  pl.BlockSpec((tb, HG_D), lambda b, hh, i: (tix(b, i), 3 * h + hh)),
                  pl.BlockSpec((1, HG_D), lambda b, hh, i: (0, hh))],
        out_specs=pl.BlockSpec((tb, HG_D), lambda b, hh, i: (tix(b, i), hh)),
        out_shape=jax.ShapeDtypeStruct((t, h * HG_D), F32),
        scratch_shapes=[pltpu.VMEM((HG_D, HG_D), F32)],
        compiler_params=pltpu.CompilerParams(
            dimension_semantics=("arbitrary", "arbitrary", "arbitrary"),
            vmem_limit_bytes=VMEM_LIMIT),
        name="hgrn_bwd" if reverse else "hgrn_fwd",
    )(p_hg, p_hg, p_hg, lb.reshape(1, -1))


class _RwkvMasks:
    def __init__(self, reverse):
        c, n = CHUNK, RW_LANES
        m = (n // RW_HEAD) * c
        row = lax.broadcasted_iota(jnp.int32, (c, c), 0)
        col = lax.broadcasted_iota(jnp.int32, (c, c), 1)
        self.tri = jnp.where((col >= row) if reverse else (col <= row), 1.0, 0.0).astype(F32)
        self.stack = (lax.broadcasted_iota(jnp.int32, (m, n), 0) // c
                      == lax.broadcasted_iota(jnp.int32, (m, n), 1) // RW_HEAD)
        rr = lax.broadcasted_iota(jnp.int32, (m, m), 0)
        cc = lax.broadcasted_iota(jnp.int32, (m, m), 1)
        same = (rr // c) == (cc // c)
        tr, tc = rr % c, cc % c
        self.strict = same & ((tc > tr) if reverse else (tc < tr))
        self.incl = same & ((tc >= tr) if reverse else (tc <= tr))
        self.eye = jnp.where(rr == cc, 1.0, 0.0).astype(F32)


def _rwkv_prepare(refs, rows, masks, reverse):
    c = CHUNK
    hg = RW_LANES // RW_HEAD
    m = hg * c

    def stack(t):
        return jnp.where(masks.stack, jnp.concatenate([t] * hg, axis=0), 0.0).astype(BF16)

    r, k, v, kk, a, lw = (ref[rows, :] for ref in refs)
    lp = _dot_f32(masks.tri, lw)
    lp_end = lp[0:1, :] if reverse else lp[c - 1:c, :]
    pinv = jnp.exp(-lp)
    dec = jnp.exp(lp_end - lp)
    kb = kk * a
    ar = jnp.concatenate([stack(-kk * jnp.exp(lp - lw)), stack(r * jnp.exp(lp))], axis=0)
    bk = jnp.concatenate([stack(kb * pinv), stack(k * pinv)], axis=0)
    v_s = stack(v)
    sc = _dot_nt(ar, bk)
    ab = jnp.where(masks.strict, sc[:m, :m], 0.0)
    ak = jnp.where(masks.strict, sc[:m, m:], 0.0)
    rbrk = jnp.concatenate([jnp.where(masks.incl, sc[m:, :m], 0.0),
                            jnp.where(masks.incl, sc[m:, m:], 0.0)], axis=1).astype(BF16)
    return dict(ar=ar, v_s=v_s, akv=_dot(ak, v_s), rbrk=rbrk, pw=ab, tinv=masks.eye + ab,
                bkp=jnp.concatenate([stack(kb * dec), stack(k * dec)], axis=0),
                gdec=jnp.exp(lp_end))


def _rwkv_advance(p, g_ref, y_ref, rows):
    c = CHUNK
    hg = RW_LANES // RW_HEAD
    m = hg * c
    g = g_ref[...]
    arg = _dot_nt(p["ar"], g)
    u_s = _dot(p["tinv"], arg[:m] + p["akv"])
    uv = jnp.concatenate([u_s.astype(BF16), p["v_s"]], axis=0)
    y_s = arg[m:] + _dot(p["rbrk"], uv)
    y = y_s[0:c]
    for h in range(1, hg):
        y = y + y_s[h * c:(h + 1) * c]
    y_ref[rows, :] = y
    g_ref[...] = g * p["gdec"] + _dot_tn(uv, p["bkp"])


def _rwkv_kernel(*refs, n_chunks):
    in_f, in_b = refs[0:6], refs[6:12]
    y_f, y_b, g_f, g_b = refs[12:16]

    @pl.when(pl.program_id(2) == 0)
    def _():
        g_f[...] = jnp.zeros_like(g_f)
        g_b[...] = jnp.zeros_like(g_b)

    c = CHUNK
    rows = [slice(ci * c, (ci + 1) * c) for ci in range(n_chunks)]
    masks = (_RwkvMasks(False), _RwkvMasks(True))
    plan = [(in_f, y_f, g_f, masks[0], False, rows), (in_b, y_b, g_b, masks[1], True, rows[::-1])]
    preps = [[_rwkv_prepare(ins, rw, mk, rev) for rw in rws] for ins, _, _, mk, rev, rws in plan]
    for _ in range(c.bit_length() - 2):
        for plist in preps:
            for p in plist:
                p["pw"] = _dot(p["pw"], p["pw"])
                p["tinv"] = p["tinv"] + _dot(p["tinv"], p["pw"])
    for step in range(n_chunks):
        for (_, y_ref, g_ref, _, _, rws), plist in zip(plan, preps):
            _rwkv_advance(plist[step], g_ref, y_ref, rws[step])


def _rwkv_scan(r, k, v, kk, a, lw_f, lw_b, *, bsz, seq, tb=256):
    t = bsz * seq
    tb = min(tb, seq)
    nblk = seq // tb
    n = RW_LANES
    spec_f = pl.BlockSpec((tb, n), lambda b, hh, i: (b * nblk + i, hh))
    spec_b = pl.BlockSpec((tb, n), lambda b, hh, i: (b * nblk + nblk - 1 - i, hh))
    out = jax.ShapeDtypeStruct((t, RW_WIDTH), F32)
    return pl.pallas_call(
        functools.partial(_rwkv_kernel, n_chunks=tb // CHUNK),
        grid=(bsz, RW_WIDTH // n, nblk),
        in_specs=[spec_f] * 6 + [spec_b] * 6,
        out_specs=[spec_f, spec_b],
        out_shape=[out, out],
        scratch_shapes=[pltpu.VMEM((n, n), F32), pltpu.VMEM((n, n), F32)],
        compiler_params=pltpu.CompilerParams(
            dimension_semantics=("arbitrary", "arbitrary", "arbitrary"),
            vmem_limit_bytes=VMEM_LIMIT),
        name="rwkv",
    )(r, k, v, kk, a, lw_f, r, k, v, kk, a, lw_b)


PAIR = 256


def _moe_kernel(be_ref, xs_ref, gate_ref, w1_ref, b1_ref, w2_ref, b2_ref, o_ref, w1p_ref, w2b_ref):
    i = pl.program_id(0)
    f2 = w1_ref.shape[-1]
    half = PAIR // 2

    @pl.when((i == 0) | (be_ref[i] != be_ref[jnp.maximum(i - 1, 0)]))
    def _():
        src = lax.broadcasted_iota(jnp.int32, (PAIR, PAIR), 0)
        dst = lax.broadcasted_iota(jnp.int32, (PAIR, PAIR), 1)
        perm = jnp.where(src == jnp.where(dst < half, 2 * dst, 2 * (dst - half) + 1), 1.0, 0.0)
        for j in range(0, f2, PAIR):
            w1p_ref[:, j:j + PAIR] = _dot(w1_ref[0, :, j:j + PAIR], perm).astype(BF16)
        w2b_ref[...] = w2_ref[0].astype(BF16)

    h = _dot(xs_ref[...], w1p_ref[...]) + b1_ref[0]
    acts = []
    for j in range(0, f2, PAIR):
        glu = jnp.minimum(h[:, j:j + half], SWIGLU_LIMIT)
        lin = jnp.clip(h[:, j + half:j + PAIR], -SWIGLU_LIMIT, SWIGLU_LIMIT)
        acts.append((glu * jax.nn.sigmoid(SWIGLU_ALPHA * glu) * (lin + 1.0)).astype(BF16))
    y = _dot(jnp.concatenate(acts, axis=1), w2b_ref[...]) + b2_ref[0]
    o_ref[...] = y * gate_ref[...]


def _moe_experts(block_exp, xs, slot_gate, w1, b1, w2, b2):
    n_slots, d = xs.shape
    f2 = w1.shape[-1]
    f = f2 // 2
    blk = MOE_BLOCK
    n_blocks = n_slots // blk
    b1p = b1.reshape(N_EXPERTS, f2 // PAIR, PAIR // 2, 2).transpose(0, 1, 3, 2).reshape(N_EXPERTS, 1, f2)
    wspec = lambda shape: pl.BlockSpec((1,) + shape, lambda i, be: (be[i], 0, 0))
    return pl.pallas_call(
        _moe_kernel,
        grid_spec=pltpu.PrefetchScalarGridSpec(
            num_scalar_prefetch=1,
            grid=(n_blocks,),
            in_specs=[pl.BlockSpec((blk, d), lambda i, be: (i, 0)),
                      pl.BlockSpec((blk, 1), lambda i, be: (i, 0)),
                      wspec((d, f2)), wspec((1, f2)), wspec((f, d)), wspec((1, d))],
            out_specs=pl.BlockSpec((blk, d), lambda i, be: (i, 0)),
            scratch_shapes=[pltpu.VMEM((d, f2), BF16), pltpu.VMEM((f, d), BF16)]),
        out_shape=jax.ShapeDtypeStruct((n_slots, d), F32),
        compiler_params=pltpu.CompilerParams(
            dimension_semantics=("arbitrary",), vmem_limit_bytes=VMEM_LIMIT),
        name="moe_experts",
    )(block_exp, xs, slot_gate.reshape(n_slots, 1), w1, b1p, w2, b2.reshape(N_EXPERTS, 1, d))


def _ln_rows(y, g, b):
    yc = y - jnp.mean(y, axis=-1, keepdims=True)
    var = jnp.mean(yc * yc, axis=-1, keepdims=True)
    return yc * lax.rsqrt(var + NORM_EPS) * g + b


def _mm_ln_kernel(m_ref, w_ref, x_ref, g_ref, b_ref, o_ref):
    o_ref[...] = _ln_rows(DN_ALPHA * x_ref[...] + _dot(m_ref[...], w_ref[...]), g_ref[...], b_ref[...])


def _matmul_ln(m, w, x, g, b, *, tm=512):
    t, k = m.shape
    d = w.shape[1]
    row = lambda i: (i, 0)
    fixed = lambda i: (0, 0)
    return pl.pallas_call(
        _mm_ln_kernel,
        grid=(t // tm,),
        in_specs=[pl.BlockSpec((tm, k), row), pl.BlockSpec((k, d), fixed), pl.BlockSpec((tm, d), row),
                  pl.BlockSpec((1, d), fixed), pl.BlockSpec((1, d), fixed)],
        out_specs=pl.BlockSpec((tm, d), row),
        out_shape=jax.ShapeDtypeStruct((t, d), F32),
        compiler_params=pltpu.CompilerParams(
            dimension_semantics=("arbitrary",), vmem_limit_bytes=VMEM_LIMIT),
        name="matmul_ln",
    )(m, w, x, g.reshape(1, d), b.reshape(1, d))


def _combine_ln_kernel(x_ref, yg_ref, g_ref, b_ref, o_ref):
    d = x_ref.shape[-1]
    moe = yg_ref[:, 0:d]
    for kk in range(1, TOP_K):
        moe = moe + yg_ref[:, kk * d:(kk + 1) * d]
    o_ref[...] = _ln_rows(DN_ALPHA * x_ref[...] + moe, g_ref[...], b_ref[...])


def _combine_ln(x, yg, g, b, *, tm=512):
    t, d = x.shape
    row = lambda i: (i, 0)
    fixed = lambda i: (0, 0)
    return pl.pallas_call(
        _combine_ln_kernel,
        grid=(t // tm,),
        in_specs=[pl.BlockSpec((tm, d), row), pl.BlockSpec((tm, TOP_K * d), row),
                  pl.BlockSpec((1, d), fixed), pl.BlockSpec((1, d), fixed)],
        out_specs=pl.BlockSpec((tm, d), row),
        out_shape=jax.ShapeDtypeStruct((t, d), F32),
        compiler_params=pltpu.CompilerParams(
            dimension_semantics=("arbitrary",), vmem_limit_bytes=VMEM_LIMIT),
        name="combine_ln",
    )(x, yg, g.reshape(1, d), b.reshape(1, d))


def _layernorm(x, g, b):
    xc = x - jnp.mean(x, axis=-1, keepdims=True)
    var = jnp.mean(xc * xc, axis=-1, keepdims=True)
    return xc * lax.rsqrt(var + NORM_EPS) * g + b


def _centred_shift(p, mu):
    pp = jnp.pad(p, ((0, 0), (1, 1), (0, 0)))
    nb = 0.5 * (pp[:, :-2] + pp[:, 2:])
    return p + mu * (nb - p)


def _rw_heads(t):
    return t.reshape(t.shape[0], RW_HEADS, RW_HEAD)


def _moe(x2, router_w, router_b, w1, b1, w2, b2):
    t, d = x2.shape
    n_assign = t * TOP_K
    n_blocks = -(-n_assign // MOE_BLOCK) + N_EXPERTS
    n_slots = n_blocks * MOE_BLOCK
    npad = 128 - N_EXPERTS
    logits = _matmul_f32(x2, jnp.pad(router_w, ((0, 0), (0, npad))),
                         jnp.pad(router_b, (0, npad)))[:, :N_EXPERTS]
    top_val, top_idx = lax.top_k(logits, TOP_K)
    gates = jax.nn.softmax(top_val, axis=-1)
    e_flat = top_idx.reshape(-1).astype(jnp.int32)
    tok_flat = jnp.repeat(jnp.arange(t, dtype=jnp.int32), TOP_K)
    g_flat = gates.reshape(-1)
    order = jnp.argsort(e_flat)
    e_s, tok_s, g_s = e_flat[order], tok_flat[order], g_flat[order]
    counts = jnp.bincount(e_flat, length=N_EXPERTS)
    starts = jnp.cumsum(counts) - counts
    padded = (counts + MOE_BLOCK - 1) // MOE_BLOCK * MOE_BLOCK
    pad_ends = jnp.cumsum(padded)
    pad_starts = pad_ends - padded
    dest = (pad_starts[e_s] + jnp.arange(n_assign, dtype=jnp.int32) - starts[e_s]).astype(jnp.int32)
    slot_tok = jnp.zeros((n_slots,), jnp.int32).at[dest].set(tok_s)
    slot_gate = jnp.zeros((n_slots,), F32).at[dest].set(g_s)
    block_start = jnp.arange(n_blocks, dtype=jnp.int32) * MOE_BLOCK
    block_exp = jnp.minimum(jnp.searchsorted(pad_ends, block_start, side='right'),
                            N_EXPERTS - 1).astype(jnp.int32)
    xs = x2.astype(BF16)[slot_tok]
    y = _moe_experts(block_exp, xs, slot_gate, w1, b1, w2, b2)
    slot_of = jnp.zeros((n_assign,), jnp.int32).at[order].set(dest)
    return y[slot_of].reshape(t, TOP_K * d)


def kernel(x, w_in, hg_lb_logits, hg_norm_w, rw_mu, rw_w0, rw_w_up, rw_a0, rw_a_up, rw_g_up,
           rw_k_k, rw_k_a, rw_r_k, rw_lnx_w, rw_lnx_b, rw_v_down, rw_v_up, rw_v0, proj_a, proj_b,
           w_out, ln1_g, ln1_b, router_w, router_b, moe_w1, moe_b1, moe_w2, moe_b2, ln2_g, ln2_b):
    bsz, s, d = x.shape
    t = bsz * s
    scan = dict(bsz=bsz, seq=s)
    lb_all = jnp.cumsum(jax.nn.softmax(hg_lb_logits.astype(F32), axis=0), axis=0)
    lb_all = lb_all - lb_all[0:1]
    hg_cols = 5 * HG_WIDTH
    rw_cols = 3 * RW_WIDTH + 2 * DECAY_LORA + AAA_LORA + GATE_LORA
    x2 = x.reshape(t, d)
    v_first = None
    for l in range(DEPTH):
        w_l = w_in[l].astype(BF16)
        w_rw = w_l[:, hg_cols:hg_cols + rw_cols]
        if l > 0:
            w_rw = jnp.concatenate([w_rw, rw_v_down[l - 1].astype(BF16)], axis=1)
        w_rw = jnp.pad(w_rw, ((0, 0), (0, 3456 - w_rw.shape[1])))
        p_hg = _matmul(x2, w_l[:, :hg_cols], tm=1024, tn=1024)
        p_rw = _matmul(x2, w_rw, tm=1024, tn=1152)
        p_gate = _matmul(x2, w_l[:, hg_cols + rw_cols:], tm=1024, tn=1024)

        o = (_hgrn_scan(p_hg, lb_all[l], reverse=False, **scan)
             + _hgrn_scan(p_hg, lb_all[l], reverse=True, **scan))
        o = o.reshape(t, HG_HEADS, HG_D)
        o = o * lax.rsqrt(jnp.mean(o * o, axis=-1, keepdims=True) + NORM_EPS) * hg_norm_w[l]
        o_a = o.reshape(t, HG_WIDTH) * jax.nn.silu(p_hg[:, 4 * HG_WIDTH:])

        ps = _centred_shift(p_rw[:, :rw_cols].reshape(bsz, s, rw_cols), rw_mu[l]).reshape(t, rw_cols)
        r, k, v = (ps[:, i * RW_WIDTH:(i + 1) * RW_WIDTH] for i in range(3))
        off = 3 * RW_WIDTH
        wl = (ps[:, off:off + DECAY_LORA], ps[:, off + DECAY_LORA:off + 2 * DECAY_LORA])
        off += 2 * DECAY_LORA
        a_lo = ps[:, off:off + AAA_LORA]
        g_lo = ps[:, off + AAA_LORA:off + AAA_LORA + GATE_LORA]
        lws = []
        for dd in range(2):
            w = -jax.nn.softplus(-(rw_w0[l, dd] + _matmul(jnp.tanh(wl[dd]), rw_w_up[l, dd]))) - 0.5
            lws.append(-jnp.exp(w))
        a = jax.nn.sigmoid(rw_a0[l] + _matmul(a_lo, rw_a_up[l]))
        g = _matmul(jax.nn.sigmoid(g_lo), rw_g_up[l])
        if l == 0:
            v_first = v
        else:
            xv = p_rw[:, rw_cols:rw_cols + rw_v_down.shape[-1]]
            v = v + (v_first - v) * jax.nn.sigmoid(rw_v0[l - 1] + _matmul(xv, rw_v_up[l - 1]))
        kk = _rw_heads(k * rw_k_k[l])
        kk = kk / jnp.maximum(jnp.sqrt(jnp.sum(kk * kk, axis=-1, keepdims=True)), 1e-12)
        kk = kk.reshape(t, RW_WIDTH)
        k = k * (1.0 + (a - 1.0) * rw_k_a[l])
        y_f, y_b = _rwkv_scan(r, k, v, kk, a, lws[0], lws[1], **scan)
        y = _rw_heads(y_f + y_b)
        yc = y - jnp.mean(y, axis=-1, keepdims=True)
        yn = yc * lax.rsqrt(jnp.mean(yc * yc, axis=-1, keepdims=True) + RW_LN_EPS)
        yn = yn * _rw_heads(rw_lnx_w[l][None])[0] + _rw_heads(rw_lnx_b[l][None])[0]
        bonus = jnp.sum(_rw_heads(r) * _rw_heads(k) * rw_r_k[l], axis=-1, keepdims=True) * _rw_heads(v)
        o_b = (yn + bonus).reshape(t, RW_WIDTH) * g

        merged = (jax.nn.sigmoid(p_gate[:, :d]) * _matmul(o_a, proj_a[l])
                  + jax.nn.sigmoid(p_gate[:, d:]) * _matmul(o_b, proj_b[l]))
        x2 = _matmul_ln(merged, w_out[l], x2, ln1_g[l], ln1_b[l])
        yg = _moe(x2, router_w[l], router_b[l], moe_w1[l], moe_b1[l], moe_w2[l], moe_b2[l])
        x2 = _combine_ln(x2, yg, ln2_g[l], ln2_b[l])
    return x2.reshape(bsz, s, d)
```

```python
import functools

import jax
import jax.numpy as jnp
from jax import lax
from jax.experimental import pallas as pl
from jax.experimental.pallas import tpu as pltpu

F32 = jnp.float32
BF16 = jnp.bfloat16

D_MODEL = 1024
DEPTH = 4
HG_HEADS = 8
HG_D = 128
HG_WIDTH = HG_HEADS * HG_D
RW_HEAD = 64
RW_HEADS = D_MODEL // RW_HEAD
RW_WIDTH = D_MODEL
DECAY_LORA = 64
AAA_LORA = 64
GATE_LORA = 128
N_EXPERTS = 32
TOP_K = 4
MOE_BLOCK = 256
SWIGLU_ALPHA = 1.702
SWIGLU_LIMIT = 7.0
NORM_EPS = 1e-5
RW_LN_EPS = 64e-5
DN_ALPHA = (2 * DEPTH) ** 0.25

CHUNK = 64
SUB = 16
RW_LANES = 256
VMEM_LIMIT = 56 * 1024 * 1024

_NT = (((1,), (1,)), ((), ()))
_TN = (((0,), (0,)), ((), ()))


def _dot(a, b):
    return jnp.dot(a.astype(BF16), b.astype(BF16), preferred_element_type=F32)


def _dot_nt(a, b):
    return lax.dot_general(a.astype(BF16), b.astype(BF16), _NT, preferred_element_type=F32)


def _dot_tn(a, b):
    return lax.dot_general(a.astype(BF16), b.astype(BF16), _TN, preferred_element_type=F32)


def _dot_f32(a, b):
    return jnp.dot(a, b, preferred_element_type=F32, precision=lax.Precision.HIGHEST)


def _mm_kernel(x_ref, w_ref, o_ref):
    o_ref[...] = _dot(x_ref[...], w_ref[...]).astype(o_ref.dtype)


def _matmul(x, w, *, tm=512, tn=512, out_dtype=F32):
    m, k = x.shape
    n = w.shape[1]
    tm = min(tm, m)
    tn = min(tn, n)
    assert m % tm == 0 and n % tn == 0
    return pl.pallas_call(
        _mm_kernel,
        grid=(n // tn, m // tm),
        in_specs=[pl.BlockSpec((tm, k), lambda j, i: (i, 0)),
                  pl.BlockSpec((k, tn), lambda j, i: (0, j))],
        out_specs=pl.BlockSpec((tm, tn), lambda j, i: (i, j)),
        out_shape=jax.ShapeDtypeStruct((m, n), out_dtype),
        compiler_params=pltpu.CompilerParams(
            dimension_semantics=("arbitrary", "arbitrary"), vmem_limit_bytes=VMEM_LIMIT),
        name="matmul",
    )(x, w)


def _mm_f32_kernel(x_ref, w_ref, b_ref, o_ref):
    o_ref[...] = _dot_f32(x_ref[...], w_ref[...]) + b_ref[...]


def _matmul_f32(x, w, b, *, tm=512):
    m, k = x.shape
    n = w.shape[1]
    return pl.pallas_call(
        _mm_f32_kernel,
        grid=(m // tm,),
        in_specs=[pl.BlockSpec((tm, k), lambda i: (i, 0)),
                  pl.BlockSpec((k, n), lambda i: (0, 0)),
                  pl.BlockSpec((1, n), lambda i: (0, 0))],
        out_specs=pl.BlockSpec((tm, n), lambda i: (i, 0)),
        out_shape=jax.ShapeDtypeStruct((m, n), F32),
        compiler_params=pltpu.CompilerParams(
            dimension_semantics=("arbitrary",), vmem_limit_bytes=VMEM_LIMIT),
        name="matmul_f32",
    )(x, w, b.reshape(1, n))


def _hgrn_kernel(q_ref, z_ref, v_ref, lb_ref, o_ref, st_ref, *, reverse, n_chunks):
    @pl.when(pl.program_id(2) == 0)
    def _():
        st_ref[...] = jnp.zeros_like(st_ref)

    c = CHUNK
    lb = lb_ref[...]
    row = lax.broadcasted_iota(jnp.int32, (c, c), 0)
    col = lax.broadcasted_iota(jnp.int32, (c, c), 1)
    tri = jnp.where((col >= row) if reverse else (col <= row), 1.0, 0.0).astype(F32)
    srow = lax.broadcasted_iota(jnp.int32, (SUB, 1), 0)

    splits = []
    size = c
    while size > SUB:
        for lo in range(0, c, size):
            splits.append((lo, lo + size // 2, lo + size))
        size //= 2

    order = range(n_chunks - 1, -1, -1) if reverse else range(n_chunks)
    for ci in order:
        r0 = ci * c
        z = z_ref[r0:r0 + c, :]
        f = lb + (1.0 - lb) * jax.nn.sigmoid(z)
        lf = jnp.log(f)
        kk = 1.0 - f
        q = q_ref[r0:r0 + c, :]
        v = v_ref[r0:r0 + c, :]
        b = _dot_f32(tri, lf)
        b_end = b[0:1, :] if reverse else b[c - 1:c, :]
        st = st_ref[...]
        o_ref[r0:r0 + c, :] = _dot_nt(q * jnp.exp(b), st)
        st_ref[...] = st * jnp.exp(b_end) + _dot_tn(v, kk * jnp.exp(b_end - b))

        for lo, mid, hi in splits:
            if reverse:
                anc = b[mid:mid + 1, :]
                qs, ks = slice(lo, mid), slice(mid, hi)
            else:
                anc = b[mid - 1:mid, :]
                qs, ks = slice(mid, hi), slice(lo, mid)
            s = _dot_nt(q[qs] * jnp.exp(b[qs] - anc), kk[ks] * jnp.exp(anc - b[ks]))
            o_ref[r0 + qs.start:r0 + qs.stop, :] += _dot(s, v[ks])

        for d0 in range(0, c, SUB):
            qb, bb, kb, vb = (t[d0:d0 + SUB] for t in (q, b, kk, v))
            acc = jnp.zeros((SUB, HG_D), F32)
            for s in range(SUB):
                mask = (srow <= s) if reverse else (srow >= s)
                e = jnp.where(mask, jnp.exp(jnp.minimum(bb - bb[s:s + 1, :], 0.0)), 0.0)
                w = jnp.sum(qb * kb[s:s + 1, :] * e, axis=-1, keepdims=True)
                acc = acc + w * vb[s:s + 1, :]
            o_ref[r0 + d0:r0 + d0 + SUB, :] += acc


def _hgrn_scan(p_hg, lb, *, bsz, seq, reverse, tb=512):
    t = bsz * seq
    tb = min(tb, seq)
    nblk = seq // tb
    h = HG_HEADS
    zoff = 2 * h if reverse else h

    def tix(b, i):
        return b * nblk + (nblk - 1 - i if reverse else i)

    return pl.pallas_call(
        functools.partial(_hgrn_kernel, reverse=reverse, n_chunks=tb // CHUNK),
        grid=(bsz, h, nblk),
        in_specs=[pl.BlockSpec((tb, HG_D), lambda b, hh, i: (tix(b, i), hh)),
                  pl.BlockSpec((tb, HG_D), lambda b, hh, i: (tix(b, i), zoff + hh)),
                  pl.BlockSpec((tb, HG_D), lambda b, hh, i: (tix(b, i), 3 * h + hh)),
                  pl.BlockSpec((1, HG_D), lambda b, hh, i: (0, hh))],
        out_specs=pl.BlockSpec((tb, HG_D), lambda b, hh, i: (tix(b, i), hh)),
        out_shape=jax.ShapeDtypeStruct((t, h * HG_D), F32),
        scratch_shapes=[pltpu.VMEM((HG_D, HG_D), F32)],
        compiler_params=pltpu.CompilerParams(
            dimension_semantics=("arbitrary", "arbitrary", "arbitrary"),
            vmem_limit_bytes=VMEM_LIMIT),
        name="hgrn_bwd" if reverse else "hgrn_fwd",
    )(p_hg, p_hg, p_hg, lb.reshape(1, -1))


class _RwkvMasks:
    def __init__(self, reverse):
        c, n = CHUNK, RW_LANES
        m = (n // RW_HEAD) * c
        row = lax.broadcasted_iota(jnp.int32, (c, c), 0)
        col = lax.broadcasted_iota(jnp.int32, (c, c), 1)
        self.tri = jnp.where((col >= row) if reverse else (col <= row), 1.0, 0.0).astype(F32)
        self.stack = (lax.broadcasted_iota(jnp.int32, (m, n), 0) // c
                      == lax.broadcasted_iota(jnp.int32, (m, n), 1) // RW_HEAD)
        tr = lax.broadcasted_iota(jnp.int32, (c, m), 0)
        tc = lax.broadcasted_iota(jnp.int32, (c, m), 1) % c
        self.strict = (tc > tr) if reverse else (tc < tr)
        self.incl = (tc >= tr) if reverse else (tc <= tr)
        self.eye = jnp.where(tc == tr, 1.0, 0.0).astype(F32)
        self.diag = (lax.broadcasted_iota(jnp.int32, (n, n), 0) // RW_HEAD
                     == lax.broadcasted_iota(jnp.int32, (n, n), 1) // RW_HEAD)


def _rwkv_stack(t, masks):
    return jnp.where(masks.stack, jnp.concatenate([t] * (RW_LANES // RW_HEAD), axis=0), 0.0).astype(BF16)


def _rwkv_prepare(refs, rows, masks, reverse):
    c = CHUNK
    m = (RW_LANES // RW_HEAD) * c
    stack = functools.partial(_rwkv_stack, masks=masks)
    r, k, v, kk, a, lw = (ref[rows, :] for ref in refs)
    lp = _dot_f32(masks.tri, lw)
    lp_end = lp[0:1, :] if reverse else lp[c - 1:c, :]
    pinv = jnp.exp(-lp)
    dec = jnp.exp(lp_end - lp)
    kb = kk * a
    ar = jnp.concatenate([-kk * jnp.exp(lp - lw), r * jnp.exp(lp)], axis=0).astype(BF16)
    bk = jnp.concatenate([stack(kb * pinv), stack(k * pinv)], axis=0)
    v_s = stack(v)
    sc = _dot_nt(ar, bk)
    ab = jnp.where(masks.strict, sc[:c, :m], 0.0)
    ak = jnp.where(masks.strict, sc[:c, m:], 0.0)
    rbrk = jnp.concatenate([jnp.where(masks.incl, sc[c:, :m], 0.0),
                            jnp.where(masks.incl, sc[c:, m:], 0.0)], axis=1).astype(BF16)
    return dict(ar=ar, v=v.astype(BF16), v_s=v_s, akv=_dot(ak, v_s), rbrk=rbrk, pw=ab,
                tinv=masks.eye + ab,
                bkp=jnp.concatenate([kb * dec, k * dec], axis=0).astype(BF16),
                gdec=jnp.exp(lp_end))


def _rwkv_double(p, masks, last):
    c = CHUNK
    pw_s = _rwkv_stack(p["pw"], masks)
    if last:
        tx = _dot(p["tinv"], pw_s)
    else:
        z = _dot(jnp.concatenate([p["pw"], p["tinv"]], axis=0), pw_s)
        p["pw"], tx = z[:c], z[c:]
    p["tinv"] = p["tinv"] + _dot(tx, pw_s)


def _rwkv_advance(p, g_ref, y_ref, rows, masks):
    c = CHUNK
    g = g_ref[...]
    arg = _dot_nt(p["ar"], g)
    u = _dot(p["tinv"], _rwkv_stack(arg[:c] + p["akv"], masks))
    uv_s = jnp.concatenate([_rwkv_stack(u, masks), p["v_s"]], axis=0)
    y_ref[rows, :] = arg[c:] + _dot(p["rbrk"], uv_s)
    uv = jnp.concatenate([u.astype(BF16), p["v"]], axis=0)
    g_ref[...] = g * p["gdec"] + jnp.where(masks.diag, _dot_tn(uv, p["bkp"]), 0.0)


def _rwkv_kernel(*refs, n_chunks):
    in_f, in_b = refs[0:6], refs[6:12]
    y_f, y_b, g_f, g_b = refs[12:16]

    @pl.when(pl.program_id(2) == 0)
    def _():
        g_f[...] = jnp.zeros_like(g_f)
        g_b[...] = jnp.zeros_like(g_b)

    c = CHUNK
    rows = [slice(ci * c, (ci + 1) * c) for ci in range(n_chunks)]
    masks = (_RwkvMasks(False), _RwkvMasks(True))
    plan = [(in_f, y_f, g_f, masks[0], False, rows), (in_b, y_b, g_b, masks[1], True, rows[::-1])]
    preps = [[_rwkv_prepare(ins, rw, mk, rev) for rw in rws] for ins, _, _, mk, rev, rws in plan]
    n_double = c.bit_length() - 2
    for it in range(n_double):
        for (_, _, _, mk, _, _), plist in zip(plan, preps):
            for p in plist:
                _rwkv_double(p, mk, last=it == n_double - 1)
    for step in range(n_chunks):
        for (_, y_ref, g_ref, mk, _, rws), plist in zip(plan, preps):
            _rwkv_advance(plist[step], g_ref, y_ref, rws[step], mk)


def _rwkv_scan(r, k, v, kk, a, lw_f, lw_b, *, bsz, seq, tb=256):
    t = bsz * seq
    tb = min(tb, seq)
    nblk = seq // tb
    n = RW_LANES
    spec_f = pl.BlockSpec((tb, n), lambda b, hh, i: (b * nblk + i, hh))
    spec_b = pl.BlockSpec((tb, n), lambda b, hh, i: (b * nblk + nblk - 1 - i, hh))
    out = jax.ShapeDtypeStruct((t, RW_WIDTH), F32)
    return pl.pallas_call(
        functools.partial(_rwkv_kernel, n_chunks=tb // CHUNK),
        grid=(bsz, RW_WIDTH // n, nblk),
        in_specs=[spec_f] * 6 + [spec_b] * 6,
        out_specs=[spec_f, spec_b],
        out_shape=[out, out],
        scratch_shapes=[pltpu.VMEM((n, n), F32), pltpu.VMEM((n, n), F32)],
        compiler_params=pltpu.CompilerParams(
            dimension_semantics=("arbitrary", "arbitrary", "arbitrary"),
            vmem_limit_bytes=VMEM_LIMIT),
        name="rwkv",
    )(r, k, v, kk, a, lw_f, r, k, v, kk, a, lw_b)


PAIR = 256


def _moe_kernel(be_ref, xs_ref, w1_ref, b1_ref, w2_ref, b2_ref, o_ref, w1p_ref, w2b_ref):
    i = pl.program_id(0)
    f2 = w1_ref.shape[-1]
    half = PAIR // 2

    @pl.when((i == 0) | (be_ref[i] != be_ref[jnp.maximum(i - 1, 0)]))
    def _():
        src = lax.broadcasted_iota(jnp.int32, (PAIR, PAIR), 0)
        dst = lax.broadcasted_iota(jnp.int32, (PAIR, PAIR), 1)
        perm = jnp.where(src == jnp.where(dst < half, 2 * dst, 2 * (dst - half) + 1), 1.0, 0.0)
        for j in range(0, f2, PAIR):
            w1p_ref[:, j:j + PAIR] = _dot(w1_ref[0, :, j:j + PAIR], perm).astype(BF16)
        w2b_ref[...] = w2_ref[0].astype(BF16)

    h = _dot(xs_ref[...], w1p_ref[...]) + b1_ref[0]
    acts = []
    for j in range(0, f2, PAIR):
        glu = jnp.minimum(h[:, j:j + half], SWIGLU_LIMIT)
        lin = jnp.clip(h[:, j + half:j + PAIR], -SWIGLU_LIMIT, SWIGLU_LIMIT)
        acts.append((glu * jax.nn.sigmoid(SWIGLU_ALPHA * glu) * (lin + 1.0)).astype(BF16))
    o_ref[...] = _dot(jnp.concatenate(acts, axis=1), w2b_ref[...]) + b2_ref[0]


def _moe_experts(block_exp, xs, w1, b1, w2, b2):
    n_slots, d = xs.shape
    f2 = w1.shape[-1]
    f = f2 // 2
    blk = MOE_BLOCK
    n_blocks = n_slots // blk
    b1p = b1.reshape(N_EXPERTS, f2 // PAIR, PAIR // 2, 2).transpose(0, 1, 3, 2).reshape(N_EXPERTS, 1, f2)
    wspec = lambda shape: pl.BlockSpec((1,) + shape, lambda i, be: (be[i], 0, 0))
    return pl.pallas_call(
        _moe_kernel,
        grid_spec=pltpu.PrefetchScalarGridSpec(
            num_scalar_prefetch=1,
            grid=(n_blocks,),
            in_specs=[pl.BlockSpec((blk, d), lambda i, be: (i, 0)),
                      wspec((d, f2)), wspec((1, f2)), wspec((f, d)), wspec((1, d))],
            out_specs=pl.BlockSpec((blk, d), lambda i, be: (i, 0)),
            scratch_shapes=[pltpu.VMEM((d, f2), BF16), pltpu.VMEM((f, d), BF16)]),
        out_shape=jax.ShapeDtypeStruct((n_slots, d), F32),
        compiler_params=pltpu.CompilerParams(
            dimension_semantics=("arbitrary",), vmem_limit_bytes=VMEM_LIMIT),
        name="moe_experts",
    )(block_exp, xs, w1, b1p, w2, b2.reshape(N_EXPERTS, 1, d))


def _ln_rows(y, g, b):
    yc = y - jnp.mean(y, axis=-1, keepdims=True)
    var = jnp.mean(yc * yc, axis=-1, keepdims=True)
    return yc * lax.rsqrt(var + NORM_EPS) * g + b


def _mm_ln_kernel(m_ref, w_ref, x_ref, g_ref, b_ref, o_ref):
    o_ref[...] = _ln_rows(DN_ALPHA * x_ref[...] + _dot(m_ref[...], w_ref[...]), g_ref[...], b_ref[...])


def _matmul_ln(m, w, x, g, b, *, tm=512):
    t, k = m.shape
    d = w.shape[1]
    row = lambda i: (i, 0)
    fixed = lambda i: (0, 0)
    return pl.pallas_call(
        _mm_ln_kernel,
        grid=(t // tm,),
        in_specs=[pl.BlockSpec((tm, k), row), pl.BlockSpec((k, d), fixed), pl.BlockSpec((tm, d), row),
                  pl.BlockSpec((1, d), fixed), pl.BlockSpec((1, d), fixed)],
        out_specs=pl.BlockSpec((tm, d), row),
        out_shape=jax.ShapeDtypeStruct((t, d), F32),
        compiler_params=pltpu.CompilerParams(
            dimension_semantics=("arbitrary",), vmem_limit_bytes=VMEM_LIMIT),
        name="matmul_ln",
    )(m, w, x, g.reshape(1, d), b.reshape(1, d))


def _combine_ln_kernel(x_ref, yg_ref, gate_ref, g_ref, b_ref, o_ref):
    d = x_ref.shape[-1]
    gate = gate_ref[...]
    moe = gate[:, 0:1] * yg_ref[:, 0:d]
    for kk in range(1, TOP_K):
        moe = moe + gate[:, kk:kk + 1] * yg_ref[:, kk * d:(kk + 1) * d]
    o_ref[...] = _ln_rows(DN_ALPHA * x_ref[...] + moe, g_ref[...], b_ref[...])


def _combine_ln(x, yg, gates, g, b, *, tm=512):
    t, d = x.shape
    row = lambda i: (i, 0)
    fixed = lambda i: (0, 0)
    return pl.pallas_call(
        _combine_ln_kernel,
        grid=(t // tm,),
        in_specs=[pl.BlockSpec((tm, d), row), pl.BlockSpec((tm, TOP_K * d), row),
                  pl.BlockSpec((tm, TOP_K), row),
                  pl.BlockSpec((1, d), fixed), pl.BlockSpec((1, d), fixed)],
        out_specs=pl.BlockSpec((tm, d), row),
        out_shape=jax.ShapeDtypeStruct((t, d), F32),
        compiler_params=pltpu.CompilerParams(
            dimension_semantics=("arbitrary",), vmem_limit_bytes=VMEM_LIMIT),
        name="combine_ln",
    )(x, yg, gates, g.reshape(1, d), b.reshape(1, d))


def _centred_shift(p, mu):
    pp = jnp.pad(p, ((0, 0), (1, 1), (0, 0)))
    nb = 0.5 * (pp[:, :-2] + pp[:, 2:])
    return p + mu * (nb - p)


def _rw_heads(t):
    return t.reshape(t.shape[0], RW_HEADS, RW_HEAD)


def _moe(x2, router_w, router_b, w1, b1, w2, b2):
    t, d = x2.shape
    n_assign = t * TOP_K
    n_blocks = -(-n_assign // MOE_BLOCK) + N_EXPERTS
    n_slots = n_blocks * MOE_BLOCK
    npad = 128 - N_EXPERTS
    logits = _matmul_f32(x2, jnp.pad(router_w, ((0, 0), (0, npad))),
                         jnp.pad(router_b, (0, npad)))[:, :N_EXPERTS]
    top_val, top_idx = lax.top_k(logits, TOP_K)
    gates = jax.nn.softmax(top_val, axis=-1)
    e_flat = top_idx.reshape(-1).astype(jnp.int32)
    order = jnp.argsort(e_flat).astype(jnp.int32)
    rank = jnp.argsort(order).astype(jnp.int32)
    experts = jnp.arange(N_EXPERTS, dtype=jnp.int32)
    counts = jnp.bincount(e_flat, length=N_EXPERTS).astype(jnp.int32)
    starts = jnp.cumsum(counts) - counts
    padded = (counts + MOE_BLOCK - 1) // MOE_BLOCK * MOE_BLOCK
    pad_ends = jnp.cumsum(padded)
    pad_starts = pad_ends - padded
    shift = pad_starts - starts
    slot_of = rank + jnp.sum(jnp.where(e_flat[:, None] == experts, shift, 0), axis=1)
    block_start = jnp.arange(n_blocks, dtype=jnp.int32) * MOE_BLOCK
    block_exp = jnp.minimum(jnp.searchsorted(pad_ends, block_start, side='right'),
                            N_EXPERTS - 1).astype(jnp.int32)
    blk_hot = block_exp[:, None] == experts
    per_slot = lambda tab: jnp.repeat(jnp.sum(jnp.where(blk_hot, tab, 0), axis=1), MOE_BLOCK)
    slot = jnp.arange(n_slots, dtype=jnp.int32)
    valid = slot - per_slot(pad_starts) < per_slot(counts)
    src = jnp.clip(slot - per_slot(shift), 0, n_assign - 1)
    slot_tok = jnp.where(valid, order[src] // TOP_K, 0)
    y = _moe_experts(block_exp, x2[slot_tok], w1, b1, w2, b2)
    return y[slot_of].reshape(t, TOP_K * d), gates


def kernel(x, w_in, hg_lb_logits, hg_norm_w, rw_mu, rw_w0, rw_w_up, rw_a0, rw_a_up, rw_g_up,
           rw_k_k, rw_k_a, rw_r_k, rw_lnx_w, rw_lnx_b, rw_v_down, rw_v_up, rw_v0, proj_a, proj_b,
           w_out, ln1_g, ln1_b, router_w, router_b, moe_w1, moe_b1, moe_w2, moe_b2, ln2_g, ln2_b):
    bsz, s, d = x.shape
    t = bsz * s
    scan = dict(bsz=bsz, seq=s)
    lb_all = jnp.cumsum(jax.nn.softmax(hg_lb_logits.astype(F32), axis=0), axis=0)
    lb_all = lb_all - lb_all[0:1]
    hg_cols = 5 * HG_WIDTH
    rw_cols = 3 * RW_WIDTH + 2 * DECAY_LORA + AAA_LORA + GATE_LORA
    x2 = x.reshape(t, d)
    v_first = None
    for l in range(DEPTH):
        w_l = w_in[l].astype(BF16)
        w_rw = w_l[:, hg_cols:hg_cols + rw_cols]
        if l > 0:
            w_rw = jnp.concatenate([w_rw, rw_v_down[l - 1].astype(BF16)], axis=1)
        w_rw = jnp.pad(w_rw, ((0, 0), (0, 3456 - w_rw.shape[1])))
        p_hg = _matmul(x2, w_l[:, :hg_cols], tm=1024, tn=1024)
        p_rw = _matmul(x2, w_rw, tm=1024, tn=1152)
        p_gate = _matmul(x2, w_l[:, hg_cols + rw_cols:], tm=1024, tn=1024)

        o = (_hgrn_scan(p_hg, lb_all[l], reverse=False, **scan)
             + _hgrn_scan(p_hg, lb_all[l], reverse=True, **scan))
        o = o.reshape(t, HG_HEADS, HG_D)
        o = o * lax.rsqrt(jnp.mean(o * o, axis=-1, keepdims=True) + NORM_EPS) * hg_norm_w[l]
        o_a = o.reshape(t, HG_WIDTH) * jax.nn.silu(p_hg[:, 4 * HG_WIDTH:])

        ps = _centred_shift(p_rw[:, :rw_cols].reshape(bsz, s, rw_cols), rw_mu[l]).reshape(t, rw_cols)
        r, k, v = (ps[:, i * RW_WIDTH:(i + 1) * RW_WIDTH] for i in range(3))
        off = 3 * RW_WIDTH
        wl = (ps[:, off:off + DECAY_LORA], ps[:, off + DECAY_LORA:off + 2 * DECAY_LORA])
        off += 2 * DECAY_LORA
        a_lo = ps[:, off:off + AAA_LORA]
        g_lo = ps[:, off + AAA_LORA:off + AAA_LORA + GATE_LORA]
        lws = []
        for dd in range(2):
            w = -jax.nn.softplus(-(rw_w0[l, dd] + _matmul(jnp.tanh(wl[dd]), rw_w_up[l, dd]))) - 0.5
            lws.append(-jnp.exp(w))
        a = jax.nn.sigmoid(rw_a0[l] + _matmul(a_lo, rw_a_up[l]))
        g = _matmul(jax.nn.sigmoid(g_lo), rw_g_up[l])
        if l == 0:
            v_first = v
        else:
            xv = p_rw[:, rw_cols:rw_cols + rw_v_down.shape[-1]]
            v = v + (v_first - v) * jax.nn.sigmoid(rw_v0[l - 1] + _matmul(xv, rw_v_up[l - 1]))
        kk = _rw_heads(k * rw_k_k[l])
        kk = kk / jnp.maximum(jnp.sqrt(jnp.sum(kk * kk, axis=-1, keepdims=True)), 1e-12)
        kk = kk.reshape(t, RW_WIDTH)
        k = k * (1.0 + (a - 1.0) * rw_k_a[l])
        y_f, y_b = _rwkv_scan(r, k, v, kk, a, lws[0], lws[1], **scan)
        y = _rw_heads(y_f + y_b)
        yc = y - jnp.mean(y, axis=-1, keepdims=True)
        yn = yc * lax.rsqrt(jnp.mean(yc * yc, axis=-1, keepdims=True) + RW_LN_EPS)
        yn = yn * _rw_heads(rw_lnx_w[l][None])[0] + _rw_heads(rw_lnx_b[l][None])[0]
        bonus = jnp.sum(_rw_heads(r) * _rw_heads(k) * rw_r_k[l], axis=-1, keepdims=True) * _rw_heads(v)
        o_b = (yn + bonus).reshape(t, RW_WIDTH) * g

        merged = (jax.nn.sigmoid(p_gate[:, :d]) * _matmul(o_a, proj_a[l])
                  + jax.nn.sigmoid(p_gate[:, d:]) * _matmul(o_b, proj_b[l]))
        x2 = _matmul_ln(merged, w_out[l], x2, ln1_g[l], ln1_b[l])
        yg, gates = _moe(x2, router_w[l], router_b[l], moe_w1[l], moe_b1[l], moe_w2[l], moe_b2[l])
        x2 = _combine_ln(x2, yg, gates, ln2_g[l], ln2_b[l])
    return x2.reshape(bsz, s, d)
```

```python
import functools

import jax
import jax.numpy as jnp
from jax import lax
from jax.experimental import pallas as pl
from jax.experimental.pallas import tpu as pltpu

F32 = jnp.float32
BF16 = jnp.bfloat16

D_MODEL = 1024
DEPTH = 4
HG_HEADS = 8
HG_D = 128
HG_WIDTH = HG_HEADS * HG_D
RW_HEAD = 64
RW_HEADS = D_MODEL // RW_HEAD
RW_WIDTH = D_MODEL
DECAY_LORA = 64
AAA_LORA = 64
GATE_LORA = 128
N_EXPERTS = 32
TOP_K = 4
MOE_BLOCK = 256
SWIGLU_ALPHA = 1.702
SWIGLU_LIMIT = 7.0
NORM_EPS = 1e-5
RW_LN_EPS = 64e-5
DN_ALPHA = (2 * DEPTH) ** 0.25

CHUNK = 64
SUB = 16
RW_LANES = 256
RW_GROUPS = 2
VMEM_LIMIT = 56 * 1024 * 1024

_NT = (((1,), (1,)), ((), ()))
_TN = (((0,), (0,)), ((), ()))


def _dot(a, b):
    return jnp.dot(a.astype(BF16), b.astype(BF16), preferred_element_type=F32)


def _dot_nt(a, b):
    return lax.dot_general(a.astype(BF16), b.astype(BF16), _NT, preferred_element_type=F32)


def _dot_tn(a, b):
    return lax.dot_general(a.astype(BF16), b.astype(BF16), _TN, preferred_element_type=F32)


def _dot_f32(a, b):
    return jnp.dot(a, b, preferred_element_type=F32, precision=lax.Precision.HIGHEST)


def _mm_kernel(x_ref, w_ref, o_ref):
    o_ref[...] = _dot(x_ref[...], w_ref[...]).astype(o_ref.dtype)


def _matmul(x, w, *, tm=512, tn=512, out_dtype=F32):
    m, k = x.shape
    n = w.shape[1]
    tm = min(tm, m)
    tn = min(tn, n)
    assert m % tm == 0 and n % tn == 0
    return pl.pallas_call(
        _mm_kernel,
        grid=(n // tn, m // tm),
        in_specs=[pl.BlockSpec((tm, k), lambda j, i: (i, 0)),
                  pl.BlockSpec((k, tn), lambda j, i: (0, j))],
        out_specs=pl.BlockSpec((tm, tn), lambda j, i: (i, j)),
        out_shape=jax.ShapeDtypeStruct((m, n), out_dtype),
        compiler_params=pltpu.CompilerParams(
            dimension_semantics=("arbitrary", "arbitrary"), vmem_limit_bytes=VMEM_LIMIT),
        name="matmul",
    )(x, w)


def _mm2_kernel(x_ref, y_ref, wx_ref, wy_ref, o_ref):
    o_ref[...] = _dot(x_ref[...], wx_ref[...]) + _dot(y_ref[...], wy_ref[...])


def _matmul2(x, y, wx, wy, *, tm=1024, tn=1152):
    m, k = x.shape
    n = wx.shape[1]
    assert m % tm == 0 and n % tn == 0
    xspec = pl.BlockSpec((tm, k), lambda j, i: (i, 0))
    wspec = pl.BlockSpec((k, tn), lambda j, i: (0, j))
    return pl.pallas_call(
        _mm2_kernel,
        grid=(n // tn, m // tm),
        in_specs=[xspec, xspec, wspec, wspec],
        out_specs=pl.BlockSpec((tm, tn), lambda j, i: (i, j)),
        out_shape=jax.ShapeDtypeStruct((m, n), F32),
        compiler_params=pltpu.CompilerParams(
            dimension_semantics=("arbitrary", "arbitrary"), vmem_limit_bytes=VMEM_LIMIT),
        name="matmul2",
    )(x, y, wx, wy)


def _mm_f32_kernel(x_ref, w_ref, b_ref, o_ref):
    o_ref[...] = _dot_f32(x_ref[...], w_ref[...]) + b_ref[...]


def _matmul_f32(x, w, b, *, tm=512):
    m, k = x.shape
    n = w.shape[1]
    return pl.pallas_call(
        _mm_f32_kernel,
        grid=(m // tm,),
        in_specs=[pl.BlockSpec((tm, k), lambda i: (i, 0)),
                  pl.BlockSpec((k, n), lambda i: (0, 0)),
                  pl.BlockSpec((1, n), lambda i: (0, 0))],
        out_specs=pl.BlockSpec((tm, n), lambda i: (i, 0)),
        out_shape=jax.ShapeDtypeStruct((m, n), F32),
        compiler_params=pltpu.CompilerParams(
            dimension_semantics=("arbitrary",), vmem_limit_bytes=VMEM_LIMIT),
        name="matmul_f32",
    )(x, w, b.reshape(1, n))


def _hgrn_kernel(q_ref, z_ref, v_ref, lb_ref, *rest, reverse, n_chunks):
    if reverse:
        of_ref, og_ref, nw_ref, o_ref, st_ref = rest
    else:
        o_ref, st_ref = rest
    @pl.when(pl.program_id(2) == 0)
    def _():
        st_ref[...] = jnp.zeros_like(st_ref)

    c = CHUNK
    lb = lb_ref[...]
    row = lax.broadcasted_iota(jnp.int32, (c, c), 0)
    col = lax.broadcasted_iota(jnp.int32, (c, c), 1)
    tri = jnp.where((col >= row) if reverse else (col <= row), 1.0, 0.0).astype(F32)
    srow = lax.broadcasted_iota(jnp.int32, (SUB, 1), 0)

    splits = []
    size = c
    while size > SUB:
        for lo in range(0, c, size):
            splits.append((lo, lo + size // 2, lo + size))
        size //= 2

    order = range(n_chunks - 1, -1, -1) if reverse else range(n_chunks)
    for ci in order:
        r0 = ci * c
        z = z_ref[r0:r0 + c, :]
        f = lb + (1.0 - lb) * jax.nn.sigmoid(z)
        lf = jnp.log(f)
        kk = 1.0 - f
        q = q_ref[r0:r0 + c, :]
        v = v_ref[r0:r0 + c, :]
        b = _dot_f32(tri, lf)
        b_end = b[0:1, :] if reverse else b[c - 1:c, :]
        st = st_ref[...]
        o_ref[r0:r0 + c, :] = _dot_nt(q * jnp.exp(b), st)
        st_ref[...] = st * jnp.exp(b_end) + _dot_tn(v, kk * jnp.exp(b_end - b))

        for lo, mid, hi in splits:
            if reverse:
                anc = b[mid:mid + 1, :]
                qs, ks = slice(lo, mid), slice(mid, hi)
            else:
                anc = b[mid - 1:mid, :]
                qs, ks = slice(mid, hi), slice(lo, mid)
            s = _dot_nt(q[qs] * jnp.exp(b[qs] - anc), kk[ks] * jnp.exp(anc - b[ks]))
            o_ref[r0 + qs.start:r0 + qs.stop, :] += _dot(s, v[ks])

        for d0 in range(0, c, SUB):
            qb, bb, kb, vb = (t[d0:d0 + SUB] for t in (q, b, kk, v))
            acc = jnp.zeros((SUB, HG_D), F32)
            for s in range(SUB):
                mask = (srow <= s) if reverse else (srow >= s)
                e = jnp.where(mask, jnp.exp(jnp.minimum(bb - bb[s:s + 1, :], 0.0)), 0.0)
                w = jnp.sum(qb * kb[s:s + 1, :] * e, axis=-1, keepdims=True)
                acc = acc + w * vb[s:s + 1, :]
            o_ref[r0 + d0:r0 + d0 + SUB, :] += acc

        if reverse:
            o = o_ref[r0:r0 + c, :] + of_ref[r0:r0 + c, :]
            o = o * lax.rsqrt(jnp.mean(o * o, axis=-1, keepdims=True) + NORM_EPS) * nw_ref[...]
            o_ref[r0:r0 + c, :] = o * jax.nn.silu(og_ref[r0:r0 + c, :])


def _hgrn_scan(p_hg, lb, *, bsz, seq, o_fwd=None, norm_w=None, tb=512):
    reverse = o_fwd is not None
    t = bsz * seq
    tb = min(tb, seq)
    nblk = seq // tb
    h = HG_HEADS
    zoff = 2 * h if reverse else h

    def col(off):
        return pl.BlockSpec((tb, HG_D),
                            lambda b, hh, i: (b * nblk + (nblk - 1 - i if reverse else i), off + hh))

    vec = pl.BlockSpec((1, HG_D), lambda b, hh, i: (0, 0))
    in_specs = [col(0), col(zoff), col(3 * h), pl.BlockSpec((1, HG_D), lambda b, hh, i: (0, hh))]
    args = [p_hg, p_hg, p_hg, lb.reshape(1, -1)]
    if reverse:
        in_specs += [col(0), col(4 * h), vec]
        args += [o_fwd, p_hg, norm_w.reshape(1, -1)]
    return pl.pallas_call(
        functools.partial(_hgrn_kernel, reverse=reverse, n_chunks=tb // CHUNK),
        grid=(bsz, h, nblk),
        in_specs=in_specs,
        out_specs=col(0),
        out_shape=jax.ShapeDtypeStruct((t, h * HG_D), F32),
        scratch_shapes=[pltpu.VMEM((HG_D, HG_D), F32)],
        compiler_params=pltpu.CompilerParams(
            dimension_semantics=("arbitrary", "arbitrary", "arbitrary"),
            vmem_limit_bytes=VMEM_LIMIT),
        name="hgrn_bwd" if reverse else "hgrn_fwd",
    )(*args)


class _RwkvMasks:
    def __init__(self, reverse):
        c, n = CHUNK, RW_LANES
        m = (n // RW_HEAD) * c
        row = lax.broadcasted_iota(jnp.int32, (c, c), 0)
        col = lax.broadcasted_iota(jnp.int32, (c, c), 1)
        self.tri = jnp.where((col >= row) if reverse else (col <= row), 1.0, 0.0).astype(F32)
        self.stack = (lax.broadcasted_iota(jnp.int32, (m, n), 0) // c
                      == lax.broadcasted_iota(jnp.int32, (m, n), 1) // RW_HEAD)
        tr = lax.broadcasted_iota(jnp.int32, (c, m), 0)
        tc = lax.broadcasted_iota(jnp.int32, (c, m), 1) % c
        self.strict = (tc > tr) if reverse else (tc < tr)
        self.incl = (tc >= tr) if reverse else (tc <= tr)
        self.eye = jnp.where(tc == tr, 1.0, 0.0).astype(F32)
        self.diag = (lax.broadcasted_iota(jnp.int32, (n, n), 0) // RW_HEAD
                     == lax.broadcasted_iota(jnp.int32, (n, n), 1) // RW_HEAD)


def _rwkv_stack(t, masks):
    return jnp.where(masks.stack, jnp.concatenate([t] * (RW_LANES // RW_HEAD), axis=0), 0.0).astype(BF16)


def _rwkv_prepare(refs, rows, lanes, masks, reverse):
    c = CHUNK
    m = (RW_LANES // RW_HEAD) * c
    stack = functools.partial(_rwkv_stack, masks=masks)
    r, k, v, kk, a, lw = (ref[rows, lanes] for ref in refs)
    lp = _dot_f32(masks.tri, lw)
    lp_end = lp[0:1, :] if reverse else lp[c - 1:c, :]
    pinv = jnp.exp(-lp)
    dec = jnp.exp(lp_end - lp)
    kb = kk * a
    ar = jnp.concatenate([-kk * jnp.exp(lp - lw), r * jnp.exp(lp)], axis=0).astype(BF16)
    bk = jnp.concatenate([stack(kb * pinv), stack(k * pinv)], axis=0)
    v_s = stack(v)
    sc = _dot_nt(ar, bk)
    ab = jnp.where(masks.strict, sc[:c, :m], 0.0)
    ak = jnp.where(masks.strict, sc[:c, m:], 0.0)
    rbrk = jnp.concatenate([jnp.where(masks.incl, sc[c:, :m], 0.0),
                            jnp.where(masks.incl, sc[c:, m:], 0.0)], axis=1).astype(BF16)
    return dict(ar=ar, v=v.astype(BF16), v_s=v_s, akv=_dot(ak, v_s), rbrk=rbrk, pw=ab,
                tinv=masks.eye + ab,
                bkp=jnp.concatenate([kb * dec, k * dec], axis=0).astype(BF16),
                gdec=jnp.exp(lp_end))


def _rwkv_double(p, masks, last):
    c = CHUNK
    pw_s = _rwkv_stack(p["pw"], masks)
    if last:
        tx = _dot(p["tinv"], pw_s)
    else:
        z = _dot(jnp.concatenate([p["pw"], p["tinv"]], axis=0), pw_s)
        p["pw"], tx = z[:c], z[c:]
    p["tinv"] = p["tinv"] + _dot(tx, pw_s)


def _rwkv_advance(p, g_ref, y_ref, rows, lanes, masks):
    c = CHUNK
    g = g_ref[...]
    arg = _dot_nt(p["ar"], g)
    u = _dot(p["tinv"], _rwkv_stack(arg[:c] + p["akv"], masks))
    uv_s = jnp.concatenate([_rwkv_stack(u, masks), p["v_s"]], axis=0)
    y_ref[rows, lanes] = arg[c:] + _dot(p["rbrk"], uv_s)
    uv = jnp.concatenate([u.astype(BF16), p["v"]], axis=0)
    g_ref[...] = g * p["gdec"] + jnp.where(masks.diag, _dot_tn(uv, p["bkp"]), 0.0)


def _rwkv_kernel(*refs, n_chunks):
    in_f, in_b = refs[0:6], refs[6:12]
    y_f, y_b, g_f, g_b = refs[12:16]

    @pl.when(pl.program_id(2) == 0)
    def _():
        g_f[...] = jnp.zeros_like(g_f)
        g_b[...] = jnp.zeros_like(g_b)

    c = CHUNK
    rows = [slice(ci * c, (ci + 1) * c) for ci in range(n_chunks)]
    masks = (_RwkvMasks(False), _RwkvMasks(True))
    plan = []
    for gi in range(RW_GROUPS):
        lanes = slice(gi * RW_LANES, (gi + 1) * RW_LANES)
        plan.append((in_f, y_f, g_f.at[gi], masks[0], False, rows, lanes))
        plan.append((in_b, y_b, g_b.at[gi], masks[1], True, rows[::-1], lanes))
    preps = [[_rwkv_prepare(ins, rw, lanes, mk, rev) for rw in rws]
             for ins, _, _, mk, rev, rws, lanes in plan]
    n_double = c.bit_length() - 2
    for it in range(n_double):
        for chain, plist in zip(plan, preps):
            for p in plist:
                _rwkv_double(p, chain[3], last=it == n_double - 1)
    for step in range(n_chunks):
        for (_, y_ref, g_ref, mk, _, rws, lanes), plist in zip(plan, preps):
            _rwkv_advance(plist[step], g_ref, y_ref, rws[step], lanes, mk)


def _rwkv_scan(r, k, v, kk, a, lw_f, lw_b, *, bsz, seq, tb=128):
    t = bsz * seq
    tb = min(tb, seq)
    nblk = seq // tb
    n = RW_LANES * RW_GROUPS
    spec_f = pl.BlockSpec((tb, n), lambda b, hh, i: (b * nblk + i, hh))
    spec_b = pl.BlockSpec((tb, n), lambda b, hh, i: (b * nblk + nblk - 1 - i, hh))
    out = jax.ShapeDtypeStruct((t, RW_WIDTH), F32)
    state = pltpu.VMEM((RW_GROUPS, RW_LANES, RW_LANES), F32)
    return pl.pallas_call(
        functools.partial(_rwkv_kernel, n_chunks=tb // CHUNK),
        grid=(bsz, RW_WIDTH // n, nblk),
        in_specs=[spec_f] * 6 + [spec_b] * 6,
        out_specs=[spec_f, spec_b],
        out_shape=[out, out],
        scratch_shapes=[state, state],
        compiler_params=pltpu.CompilerParams(
            dimension_semantics=("arbitrary", "arbitrary", "arbitrary"),
            vmem_limit_bytes=VMEM_LIMIT),
        name="rwkv",
    )(r, k, v, kk, a, lw_f, r, k, v, kk, a, lw_b)


PAIR = 256


def _moe_kernel(be_ref, xs_ref, w1_ref, b1_ref, w2_ref, b2_ref, o_ref, w1p_ref, w2b_ref):
    i = pl.program_id(0)
    f2 = w1_ref.shape[-1]
    half = PAIR // 2

    @pl.when((i == 0) | (be_ref[i] != be_ref[jnp.maximum(i - 1, 0)]))
    def _():
        src = lax.broadcasted_iota(jnp.int32, (PAIR, PAIR), 0)
        dst = lax.broadcasted_iota(jnp.int32, (PAIR, PAIR), 1)
        perm = jnp.where(src == jnp.where(dst < half, 2 * dst, 2 * (dst - half) + 1), 1.0, 0.0)
        for j in range(0, f2, PAIR):
            w1p_ref[:, j:j + PAIR] = _dot(w1_ref[0, :, j:j + PAIR], perm).astype(BF16)
        w2b_ref[...] = w2_ref[0].astype(BF16)

    h = _dot(xs_ref[...], w1p_ref[...]) + b1_ref[0]
    acts = []
    for j in range(0, f2, PAIR):
        glu = jnp.minimum(h[:, j:j + half], SWIGLU_LIMIT)
        lin = jnp.clip(h[:, j + half:j + PAIR], -SWIGLU_LIMIT, SWIGLU_LIMIT)
        acts.append((glu * jax.nn.sigmoid(SWIGLU_ALPHA * glu) * (lin + 1.0)).astype(BF16))
    o_ref[...] = _dot(jnp.concatenate(acts, axis=1), w2b_ref[...]) + b2_ref[0]


def _moe_experts(block_exp, xs, w1, b1, w2, b2):
    n_slots, d = xs.shape
    f2 = w1.shape[-1]
    f = f2 // 2
    blk = MOE_BLOCK
    n_blocks = n_slots // blk
    b1p = b1.reshape(N_EXPERTS, f2 // PAIR, PAIR // 2, 2).transpose(0, 1, 3, 2).reshape(N_EXPERTS, 1, f2)
    wspec = lambda shape: pl.BlockSpec((1,) + shape, lambda i, be: (be[i], 0, 0))
    return pl.pallas_call(
        _moe_kernel,
        grid_spec=pltpu.PrefetchScalarGridSpec(
            num_scalar_prefetch=1,
            grid=(n_blocks,),
            in_specs=[pl.BlockSpec((blk, d), lambda i, be: (i, 0)),
                      wspec((d, f2)), wspec((1, f2)), wspec((f, d)), wspec((1, d))],
            out_specs=pl.BlockSpec((blk, d), lambda i, be: (i, 0)),
            scratch_shapes=[pltpu.VMEM((d, f2), BF16), pltpu.VMEM((f, d), BF16)]),
        out_shape=jax.ShapeDtypeStruct((n_slots, d), F32),
        compiler_params=pltpu.CompilerParams(
            dimension_semantics=("arbitrary",), vmem_limit_bytes=VMEM_LIMIT),
        name="moe_experts",
    )(block_exp, xs, w1, b1p, w2, b2.reshape(N_EXPERTS, 1, d))


def _ln_rows(y, g, b):
    yc = y - jnp.mean(y, axis=-1, keepdims=True)
    var = jnp.mean(yc * yc, axis=-1, keepdims=True)
    return yc * lax.rsqrt(var + NORM_EPS) * g + b


def _merge_ln_kernel(oa_ref, ob_ref, gate_ref, x_ref, pa_ref, pb_ref, wo_ref, g_ref, b_ref, o_ref):
    d = x_ref.shape[-1]
    merged = (jax.nn.sigmoid(gate_ref[:, :d]) * _dot(oa_ref[...], pa_ref[...])
              + jax.nn.sigmoid(gate_ref[:, d:]) * _dot(ob_ref[...], pb_ref[...]))
    o_ref[...] = _ln_rows(DN_ALPHA * x_ref[...] + _dot(merged, wo_ref[...]), g_ref[...], b_ref[...])


def _merge_ln(o_a, o_b, p_gate, x, proj_a, proj_b, w_out, g, b, *, tm=512):
    t, d = x.shape
    row = lambda i: (i, 0)
    fixed = lambda i: (0, 0)
    wspec = pl.BlockSpec((d, d), fixed)
    return pl.pallas_call(
        _merge_ln_kernel,
        grid=(t // tm,),
        in_specs=[pl.BlockSpec((tm, d), row), pl.BlockSpec((tm, d), row), pl.BlockSpec((tm, 2 * d), row),
                  pl.BlockSpec((tm, d), row), wspec, wspec, wspec,
                  pl.BlockSpec((1, d), fixed), pl.BlockSpec((1, d), fixed)],
        out_specs=pl.BlockSpec((tm, d), row),
        out_shape=jax.ShapeDtypeStruct((t, d), F32),
        compiler_params=pltpu.CompilerParams(
            dimension_semantics=("arbitrary",), vmem_limit_bytes=VMEM_LIMIT),
        name="merge_ln",
    )(o_a, o_b, p_gate, x, proj_a.astype(BF16), proj_b.astype(BF16), w_out.astype(BF16),
      g.reshape(1, d), b.reshape(1, d))


def _combine_ln_kernel(x_ref, yg_ref, gate_ref, g_ref, b_ref, o_ref):
    gate = gate_ref[...]
    moe = gate[:, 0:1] * yg_ref[0]
    for kk in range(1, TOP_K):
        moe = moe + gate[:, kk:kk + 1] * yg_ref[kk]
    o_ref[...] = _ln_rows(DN_ALPHA * x_ref[...] + moe, g_ref[...], b_ref[...])


def _combine_ln(x, yg, gates, g, b, *, tm=512):
    t, d = x.shape
    row = lambda i: (i, 0)
    fixed = lambda i: (0, 0)
    return pl.pallas_call(
        _combine_ln_kernel,
        grid=(t // tm,),
        in_specs=[pl.BlockSpec((tm, d), row), pl.BlockSpec((TOP_K, tm, d), lambda i: (0, i, 0)),
                  pl.BlockSpec((tm, TOP_K), row),
                  pl.BlockSpec((1, d), fixed), pl.BlockSpec((1, d), fixed)],
        out_specs=pl.BlockSpec((tm, d), row),
        out_shape=jax.ShapeDtypeStruct((t, d), F32),
        compiler_params=pltpu.CompilerParams(
            dimension_semantics=("arbitrary",), vmem_limit_bytes=VMEM_LIMIT),
        name="combine_ln",
    )(x, yg, gates, g.reshape(1, d), b.reshape(1, d))


def _rw_heads(t):
    return t.reshape(t.shape[0], RW_HEADS, RW_HEAD)


def _moe(x2, router_w, router_b, w1, b1, w2, b2):
    t, d = x2.shape
    n_assign = t * TOP_K
    n_blocks = -(-n_assign // MOE_BLOCK) + N_EXPERTS
    n_slots = n_blocks * MOE_BLOCK
    npad = 128 - N_EXPERTS
    logits = _matmul_f32(x2, jnp.pad(router_w, ((0, 0), (0, npad))),
                         jnp.pad(router_b, (0, npad)))[:, :N_EXPERTS]
    top_val, top_idx = lax.top_k(logits, TOP_K)
    gates = jax.nn.softmax(top_val, axis=-1)
    e_flat = top_idx.reshape(-1).astype(jnp.int32)
    order = jnp.argsort(e_flat).astype(jnp.int32)
    rank = jnp.argsort(order).astype(jnp.int32)
    experts = jnp.arange(N_EXPERTS, dtype=jnp.int32)
    counts = jnp.bincount(e_flat, length=N_EXPERTS).astype(jnp.int32)
    starts = jnp.cumsum(counts) - counts
    padded = (counts + MOE_BLOCK - 1) // MOE_BLOCK * MOE_BLOCK
    pad_ends = jnp.cumsum(padded)
    pad_starts = pad_ends - padded
    shift = pad_starts - starts
    slot_of = rank + jnp.sum(jnp.where(e_flat[:, None] == experts, shift, 0), axis=1)
    block_start = jnp.arange(n_blocks, dtype=jnp.int32) * MOE_BLOCK
    block_exp = jnp.minimum(jnp.searchsorted(pad_ends, block_start, side='right'),
                            N_EXPERTS - 1).astype(jnp.int32)
    blk_hot = block_exp[:, None] == experts
    per_slot = lambda tab: jnp.repeat(jnp.sum(jnp.where(blk_hot, tab, 0), axis=1), MOE_BLOCK)
    slot = jnp.arange(n_slots, dtype=jnp.int32)
    valid = slot - per_slot(pad_starts) < per_slot(counts)
    src = jnp.clip(slot - per_slot(shift), 0, n_assign - 1)
    slot_tok = jnp.where(valid, order[src] // TOP_K, 0)
    y = _moe_experts(block_exp, x2[slot_tok], w1, b1, w2, b2)
    return y[slot_of.reshape(t, TOP_K).T.reshape(-1)].reshape(TOP_K, t, d), gates


def kernel(x, w_in, hg_lb_logits, hg_norm_w, rw_mu, rw_w0, rw_w_up, rw_a0, rw_a_up, rw_g_up,
           rw_k_k, rw_k_a, rw_r_k, rw_lnx_w, rw_lnx_b, rw_v_down, rw_v_up, rw_v0, proj_a, proj_b,
           w_out, ln1_g, ln1_b, router_w, router_b, moe_w1, moe_b1, moe_w2, moe_b2, ln2_g, ln2_b):
    bsz, s, d = x.shape
    t = bsz * s
    scan = dict(bsz=bsz, seq=s)
    lb_all = jnp.cumsum(jax.nn.softmax(hg_lb_logits.astype(F32), axis=0), axis=0)
    lb_all = lb_all - lb_all[0:1]
    hg_cols = 5 * HG_WIDTH
    rw_cols = 3 * RW_WIDTH + 2 * DECAY_LORA + AAA_LORA + GATE_LORA
    x2 = x.reshape(t, d)
    v_first = None
    for l in range(DEPTH):
        w_l = w_in[l].astype(BF16)
        p_hg = _matmul(x2, w_l[:, :hg_cols], tm=1024, tn=1024)
        p_gate = _matmul(x2, w_l[:, hg_cols + rw_cols:], tm=1024, tn=1024)
        w_rw = w_in[l][:, hg_cols:hg_cols + rw_cols]
        w_self, w_nb = w_rw * (1.0 - rw_mu[l]), w_rw * rw_mu[l]
        if l > 0:
            w_self = jnp.concatenate([w_self, rw_v_down[l - 1]], axis=1)
        pad_cols = lambda w: jnp.pad(w, ((0, 0), (0, 3456 - w.shape[1]))).astype(BF16)
        x3 = jnp.pad(x2.reshape(bsz, s, d), ((0, 0), (1, 1), (0, 0)))
        x_nb = (0.5 * (x3[:, :-2] + x3[:, 2:])).reshape(t, d)
        ps = _matmul2(x2, x_nb, pad_cols(w_self), pad_cols(w_nb))

        o_a = _hgrn_scan(p_hg, lb_all[l], o_fwd=_hgrn_scan(p_hg, lb_all[l], **scan),
                         norm_w=hg_norm_w[l], **scan)

        r, k, v = (ps[:, i * RW_WIDTH:(i + 1) * RW_WIDTH] for i in range(3))
        off = 3 * RW_WIDTH
        wl = (ps[:, off:off + DECAY_LORA], ps[:, off + DECAY_LORA:off + 2 * DECAY_LORA])
        off += 2 * DECAY_LORA
        a_lo = ps[:, off:off + AAA_LORA]
        g_lo = ps[:, off + AAA_LORA:off + AAA_LORA + GATE_LORA]
        lws = []
        for dd in range(2):
            w = -jax.nn.softplus(-(rw_w0[l, dd] + _matmul(jnp.tanh(wl[dd]), rw_w_up[l, dd]))) - 0.5
            lws.append(-jnp.exp(w))
        a = jax.nn.sigmoid(rw_a0[l] + _matmul(a_lo, rw_a_up[l]))
        g = _matmul(jax.nn.sigmoid(g_lo), rw_g_up[l])
        if l == 0:
            v_first = v
        else:
            xv = ps[:, rw_cols:rw_cols + rw_v_down.shape[-1]]
            v = v + (v_first - v) * jax.nn.sigmoid(rw_v0[l - 1] + _matmul(xv, rw_v_up[l - 1]))
        kk = _rw_heads(k * rw_k_k[l])
        kk = kk / jnp.maximum(jnp.sqrt(jnp.sum(kk * kk, axis=-1, keepdims=True)), 1e-12)
        kk = kk.reshape(t, RW_WIDTH)
        k = k * (1.0 + (a - 1.0) * rw_k_a[l])
        y_f, y_b = _rwkv_scan(r, k, v, kk, a, lws[0], lws[1], **scan)
        y = _rw_heads(y_f + y_b)
        yc = y - jnp.mean(y, axis=-1, keepdims=True)
        yn = yc * lax.rsqrt(jnp.mean(yc * yc, axis=-1, keepdims=True) + RW_LN_EPS)
        yn = yn * _rw_heads(rw_lnx_w[l][None])[0] + _rw_heads(rw_lnx_b[l][None])[0]
        bonus = jnp.sum(_rw_heads(r) * _rw_heads(k) * rw_r_k[l], axis=-1, keepdims=True) * _rw_heads(v)
        o_b = (yn + bonus).reshape(t, RW_WIDTH) * g

        x2 = _merge_ln(o_a, o_b, p_gate, x2, proj_a[l], proj_b[l], w_out[l], ln1_g[l], ln1_b[l])
        yg, gates = _moe(x2, router_w[l], router_b[l], moe_w1[l], moe_b1[l], moe_w2[l], moe_b2[l])
        x2 = _combine_ln(x2, yg, gates, ln2_g[l], ln2_b[l])
    return x2.reshape(bsz, s, d)
```

```python
import functools

import jax
import jax.numpy as jnp
from jax import lax
from jax.experimental import pallas as pl
from jax.experimental.pallas import tpu as pltpu

F32 = jnp.float32
BF16 = jnp.bfloat16

D_MODEL = 1024
DEPTH = 4
HG_HEADS = 8
HG_D = 128
HG_WIDTH = HG_HEADS * HG_D
RW_HEAD = 64
RW_HEADS = D_MODEL // RW_HEAD
RW_WIDTH = D_MODEL
DECAY_LORA = 64
AAA_LORA = 64
GATE_LORA = 128
N_EXPERTS = 32
TOP_K = 4
MOE_BLOCK = 256
SWIGLU_ALPHA = 1.702
SWIGLU_LIMIT = 7.0
NORM_EPS = 1e-5
RW_LN_EPS = 64e-5
DN_ALPHA = (2 * DEPTH) ** 0.25

CHUNK = 64
SUB = 16
RW_LANES = 256
RW_GROUPS = 2
VMEM_LIMIT = 56 * 1024 * 1024

_NT = (((1,), (1,)), ((), ()))
_TN = (((0,), (0,)), ((), ()))


def _dot(a, b):
    return jnp.dot(a.astype(BF16), b.astype(BF16), preferred_element_type=F32)


def _dot_nt(a, b):
    return lax.dot_general(a.astype(BF16), b.astype(BF16), _NT, preferred_element_type=F32)


def _dot_tn(a, b):
    return lax.dot_general(a.astype(BF16), b.astype(BF16), _TN, preferred_element_type=F32)


def _dot_f32(a, b):
    return jnp.dot(a, b, preferred_element_type=F32, precision=lax.Precision.HIGHEST)


def _mm_kernel(x_ref, w_ref, o_ref):
    o_ref[...] = _dot(x_ref[...], w_ref[...]).astype(o_ref.dtype)


def _matmul(x, w, *, tm=512, tn=512, out_dtype=F32):
    m, k = x.shape
    n = w.shape[1]
    tm = min(tm, m)
    tn = min(tn, n)
    assert m % tm == 0 and n % tn == 0
    return pl.pallas_call(
        _mm_kernel,
        grid=(n // tn, m // tm),
        in_specs=[pl.BlockSpec((tm, k), lambda j, i: (i, 0)),
                  pl.BlockSpec((k, tn), lambda j, i: (0, j))],
        out_specs=pl.BlockSpec((tm, tn), lambda j, i: (i, j)),
        out_shape=jax.ShapeDtypeStruct((m, n), out_dtype),
        compiler_params=pltpu.CompilerParams(
            dimension_semantics=("arbitrary", "arbitrary"), vmem_limit_bytes=VMEM_LIMIT),
        name="matmul",
    )(x, w)


def _mm2_kernel(x_ref, y_ref, wx_ref, wy_ref, o_ref):
    o_ref[...] = _dot(x_ref[...], wx_ref[...]) + _dot(y_ref[...], wy_ref[...])


def _matmul2(x, y, wx, wy, *, tm=1024, tn=1152):
    m, k = x.shape
    n = wx.shape[1]
    assert m % tm == 0 and n % tn == 0
    xspec = pl.BlockSpec((tm, k), lambda j, i: (i, 0))
    wspec = pl.BlockSpec((k, tn), lambda j, i: (0, j))
    return pl.pallas_call(
        _mm2_kernel,
        grid=(n // tn, m // tm),
        in_specs=[xspec, xspec, wspec, wspec],
        out_specs=pl.BlockSpec((tm, tn), lambda j, i: (i, j)),
        out_shape=jax.ShapeDtypeStruct((m, n), F32),
        compiler_params=pltpu.CompilerParams(
            dimension_semantics=("arbitrary", "arbitrary"), vmem_limit_bytes=VMEM_LIMIT),
        name="matmul2",
    )(x, y, wx, wy)


def _mm_f32_kernel(x_ref, w_ref, b_ref, o_ref):
    o_ref[...] = _dot_f32(x_ref[...], w_ref[...]) + b_ref[...]


def _matmul_f32(x, w, b, *, tm=512):
    m, k = x.shape
    n = w.shape[1]
    return pl.pallas_call(
        _mm_f32_kernel,
        grid=(m // tm,),
        in_specs=[pl.BlockSpec((tm, k), lambda i: (i, 0)),
                  pl.BlockSpec((k, n), lambda i: (0, 0)),
                  pl.BlockSpec((1, n), lambda i: (0, 0))],
        out_specs=pl.BlockSpec((tm, n), lambda i: (i, 0)),
        out_shape=jax.ShapeDtypeStruct((m, n), F32),
        compiler_params=pltpu.CompilerParams(
            dimension_semantics=("arbitrary",), vmem_limit_bytes=VMEM_LIMIT),
        name="matmul_f32",
    )(x, w, b.reshape(1, n))


def _hgrn_kernel(q_ref, z_ref, v_ref, lb_ref, *rest, reverse, n_chunks):
    if reverse:
        of_ref, og_ref, nw_ref, o_ref, st_ref = rest
    else:
        o_ref, st_ref = rest
    @pl.when(pl.program_id(2) == 0)
    def _():
        st_ref[...] = jnp.zeros_like(st_ref)

    c = CHUNK
    lb = lb_ref[...]
    row = lax.broadcasted_iota(jnp.int32, (c, c), 0)
    col = lax.broadcasted_iota(jnp.int32, (c, c), 1)
    tri = jnp.where((col >= row) if reverse else (col <= row), 1.0, 0.0).astype(F32)
    srow = lax.broadcasted_iota(jnp.int32, (SUB, 1), 0)

    splits = []
    size = c
    while size > SUB:
        for lo in range(0, c, size):
            splits.append((lo, lo + size // 2, lo + size))
        size //= 2

    order = range(n_chunks - 1, -1, -1) if reverse else range(n_chunks)
    for ci in order:
        r0 = ci * c
        z = z_ref[r0:r0 + c, :]
        f = lb + (1.0 - lb) * jax.nn.sigmoid(z)
        lf = jnp.log(f)
        kk = 1.0 - f
        q = q_ref[r0:r0 + c, :]
        v = v_ref[r0:r0 + c, :]
        b = _dot_f32(tri, lf)
        b_end = b[0:1, :] if reverse else b[c - 1:c, :]
        st = st_ref[...]
        o_ref[r0:r0 + c, :] = _dot_nt(q * jnp.exp(b), st)
        st_ref[...] = st * jnp.exp(b_end) + _dot_tn(v, kk * jnp.exp(b_end - b))

        for lo, mid, hi in splits:
            if reverse:
                anc = b[mid:mid + 1, :]
                qs, ks = slice(lo, mid), slice(mid, hi)
            else:
                anc = b[mid - 1:mid, :]
                qs, ks = slice(mid, hi), slice(lo, mid)
            s = _dot_nt(q[qs] * jnp.exp(b[qs] - anc), kk[ks] * jnp.exp(anc - b[ks]))
            o_ref[r0 + qs.start:r0 + qs.stop, :] += _dot(s, v[ks])

        for d0 in range(0, c, SUB):
            qb, bb, kb, vb = (t[d0:d0 + SUB] for t in (q, b, kk, v))
            acc = jnp.zeros((SUB, HG_D), F32)
            for s in range(SUB):
                mask = (srow <= s) if reverse else (srow >= s)
                e = jnp.where(mask, jnp.exp(jnp.minimum(bb - bb[s:s + 1, :], 0.0)), 0.0)
                w = jnp.sum(qb * kb[s:s + 1, :] * e, axis=-1, keepdims=True)
                acc = acc + w * vb[s:s + 1, :]
            o_ref[r0 + d0:r0 + d0 + SUB, :] += acc

        if reverse:
            o = o_ref[r0:r0 + c, :] + of_ref[r0:r0 + c, :]
            o = o * lax.rsqrt(jnp.mean(o * o, axis=-1, keepdims=True) + NORM_EPS) * nw_ref[...]
            o_ref[r0:r0 + c, :] = o * jax.nn.silu(og_ref[r0:r0 + c, :])


def _hgrn_scan(p_hg, lb, *, bsz, seq, o_fwd=None, norm_w=None, tb=512):
    reverse = o_fwd is not None
    t = bsz * seq
    tb = min(tb, seq)
    nblk = seq // tb
    h = HG_HEADS
    zoff = 2 * h if reverse else h

    def col(off):
        return pl.BlockSpec((tb, HG_D),
                            lambda b, hh, i: (b * nblk + (nblk - 1 - i if reverse else i), off + hh))

    vec = pl.BlockSpec((1, HG_D), lambda b, hh, i: (0, 0))
    in_specs = [col(0), col(zoff), col(3 * h), pl.BlockSpec((1, HG_D), lambda b, hh, i: (0, hh))]
    args = [p_hg, p_hg, p_hg, lb.reshape(1, -1)]
    if reverse:
        in_specs += [col(0), col(4 * h), vec]
        args += [o_fwd, p_hg, norm_w.reshape(1, -1)]
    return pl.pallas_call(
        functools.partial(_hgrn_kernel, reverse=reverse, n_chunks=tb // CHUNK),
        grid=(bsz, h, nblk),
        in_specs=in_specs,
        out_specs=col(0),
        out_shape=jax.ShapeDtypeStruct((t, h * HG_D), F32),
        scratch_shapes=[pltpu.VMEM((HG_D, HG_D), F32)],
        compiler_params=pltpu.CompilerParams(
            dimension_semantics=("arbitrary", "arbitrary", "arbitrary"),
            vmem_limit_bytes=VMEM_LIMIT),
        name="hgrn_bwd" if reverse else "hgrn_fwd",
    )(*args)


class _RwkvMasks:
    def __init__(self, reverse):
        c, n = CHUNK, RW_LANES
        m = (n // RW_HEAD) * c
        row = lax.broadcasted_iota(jnp.int32, (c, c), 0)
        col = lax.broadcasted_iota(jnp.int32, (c, c), 1)
        self.tri = jnp.where((col >= row) if reverse else (col <= row), 1.0, 0.0).astype(F32)
        self.stack = (lax.broadcasted_iota(jnp.int32, (m, n), 0) // c
                      == lax.broadcasted_iota(jnp.int32, (m, n), 1) // RW_HEAD)
        tr = lax.broadcasted_iota(jnp.int32, (c, m), 0)
        tc = lax.broadcasted_iota(jnp.int32, (c, m), 1) % c
        self.strict = (tc > tr) if reverse else (tc < tr)
        self.incl = (tc >= tr) if reverse else (tc <= tr)
        self.eye = jnp.where(tc == tr, 1.0, 0.0).astype(F32)
        self.diag = (lax.broadcasted_iota(jnp.int32, (n, n), 0) // RW_HEAD
                     == lax.broadcasted_iota(jnp.int32, (n, n), 1) // RW_HEAD)


def _rwkv_stack(t, masks):
    return jnp.where(masks.stack, jnp.concatenate([t] * (RW_LANES // RW_HEAD), axis=0), 0.0).astype(BF16)


def _lockstep(gens):
    gens = list(gens)
    while gens:
        alive = []
        for gen in gens:
            try:
                next(gen)
                alive.append(gen)
            except StopIteration:
                pass
        gens = alive


def _rwkv_prepare(p, refs, rows, lanes, masks, reverse):
    c = CHUNK
    m = (RW_LANES // RW_HEAD) * c
    stack = functools.partial(_rwkv_stack, masks=masks)
    r, k, v, kk, a, lw = (ref[rows, lanes] for ref in refs)
    lp = _dot_f32(masks.tri, lw)
    yield
    lp_end = lp[0:1, :] if reverse else lp[c - 1:c, :]
    pinv = jnp.exp(-lp)
    dec = jnp.exp(lp_end - lp)
    kb = kk * a
    ar = jnp.concatenate([-kk * jnp.exp(lp - lw), r * jnp.exp(lp)], axis=0).astype(BF16)
    bk = jnp.concatenate([stack(kb * pinv), stack(k * pinv)], axis=0)
    v_s = stack(v)
    sc = _dot_nt(ar, bk)
    yield
    ab = jnp.where(masks.strict, sc[:c, :m], 0.0)
    akrk = jnp.concatenate([jnp.where(masks.strict, sc[:c, m:], 0.0),
                            jnp.where(masks.incl, sc[c:, m:], 0.0)], axis=0)
    p.update(ar=ar, v=v.astype(BF16), akrk_v=_dot(akrk, v_s),
             rb=jnp.where(masks.incl, sc[c:, :m], 0.0).astype(BF16),
             pw=_dot(ab, stack(ab)), tinv=masks.eye + ab,
             bkp=jnp.concatenate([kb * dec, k * dec], axis=0).astype(BF16),
             gdec=jnp.exp(lp_end))


def _rwkv_double(p, masks, last):
    c = CHUNK
    pw_s = _rwkv_stack(p["pw"], masks)
    if last:
        p["tinv"] = p["tinv"] + _dot(p["tinv"], pw_s)
    else:
        z = _dot(jnp.concatenate([p["pw"], p["tinv"]], axis=0), pw_s)
        p["pw"], p["tinv"] = z[:c], p["tinv"] + z[c:]


def _rwkv_chain(preps, g_ref, y_ref, rows_seq, lanes, masks):
    c = CHUNK
    for p, rows in zip(preps, rows_seq):
        g = g_ref[...]
        arg = _dot_nt(p["ar"], g)
        yield
        u = _dot(p["tinv"], _rwkv_stack(arg[:c] + p["akrk_v"][:c], masks))
        yield
        y_ref[rows, lanes] = arg[c:] + p["akrk_v"][c:] + _dot(p["rb"], _rwkv_stack(u, masks))
        uv = jnp.concatenate([u.astype(BF16), p["v"]], axis=0)
        g_ref[...] = g * p["gdec"] + jnp.where(masks.diag, _dot_tn(uv, p["bkp"]), 0.0)
        yield


def _rwkv_kernel(*refs, n_chunks):
    in_f, in_b = refs[0:6], refs[6:12]
    y_f, y_b, g_f, g_b = refs[12:16]

    @pl.when(pl.program_id(2) == 0)
    def _():
        g_f[...] = jnp.zeros_like(g_f)
        g_b[...] = jnp.zeros_like(g_b)

    c = CHUNK
    rows = [slice(ci * c, (ci + 1) * c) for ci in range(n_chunks)]
    masks = (_RwkvMasks(False), _RwkvMasks(True))
    plan = []
    for gi in range(RW_GROUPS):
        lanes = slice(gi * RW_LANES, (gi + 1) * RW_LANES)
        plan.append((in_f, y_f, g_f.at[gi], masks[0], False, rows, lanes))
        plan.append((in_b, y_b, g_b.at[gi], masks[1], True, rows[::-1], lanes))
    preps = [[{} for _ in rows] for _ in plan]
    _lockstep(_rwkv_prepare(p, ins, rw, lanes, mk, rev)
              for (ins, _, _, mk, rev, rws, lanes), plist in zip(plan, preps)
              for p, rw in zip(plist, rws))
    n_double = c.bit_length() - 2
    for it in range(n_double):
        for chain, plist in zip(plan, preps):
            for p in plist:
                _rwkv_double(p, chain[3], last=it == n_double - 1)
    _lockstep(_rwkv_chain(plist, g_ref, y_ref, rws, lanes, mk)
              for (_, y_ref, g_ref, mk, _, rws, lanes), plist in zip(plan, preps))


def _rwkv_scan(r, k, v, kk, a, lw_f, lw_b, *, bsz, seq, tb=128):
    t = bsz * seq
    tb = min(tb, seq)
    nblk = seq // tb
    n = RW_LANES * RW_GROUPS
    spec_f = pl.BlockSpec((tb, n), lambda b, hh, i: (b * nblk + i, hh))
    spec_b = pl.BlockSpec((tb, n), lambda b, hh, i: (b * nblk + nblk - 1 - i, hh))
    out = jax.ShapeDtypeStruct((t, RW_WIDTH), F32)
    state = pltpu.VMEM((RW_GROUPS, RW_LANES, RW_LANES), F32)
    return pl.pallas_call(
        functools.partial(_rwkv_kernel, n_chunks=tb // CHUNK),
        grid=(bsz, RW_WIDTH // n, nblk),
        in_specs=[spec_f] * 6 + [spec_b] * 6,
        out_specs=[spec_f, spec_b],
        out_shape=[out, out],
        scratch_shapes=[state, state],
        compiler_params=pltpu.CompilerParams(
            dimension_semantics=("arbitrary", "arbitrary", "arbitrary"),
            vmem_limit_bytes=VMEM_LIMIT),
        name="rwkv",
    )(r, k, v, kk, a, lw_f, r, k, v, kk, a, lw_b)


PAIR = 256


def _moe_kernel(be_ref, xs_ref, w1_ref, b1_ref, w2_ref, b2_ref, o_ref, w1p_ref, w2b_ref):
    i = pl.program_id(0)
    f2 = w1_ref.shape[-1]
    half = PAIR // 2

    @pl.when((i == 0) | (be_ref[i] != be_ref[jnp.maximum(i - 1, 0)]))
    def _():
        src = lax.broadcasted_iota(jnp.int32, (PAIR, PAIR), 0)
        dst = lax.broadcasted_iota(jnp.int32, (PAIR, PAIR), 1)
        perm = jnp.where(src == jnp.where(dst < half, 2 * dst, 2 * (dst - half) + 1), 1.0, 0.0)
        for j in range(0, f2, PAIR):
            w1p_ref[:, j:j + PAIR] = _dot(w1_ref[0, :, j:j + PAIR], perm).astype(BF16)
        w2b_ref[...] = w2_ref[0].astype(BF16)

    h = _dot(xs_ref[...], w1p_ref[...]) + b1_ref[0]
    acts = []
    for j in range(0, f2, PAIR):
        glu = jnp.minimum(h[:, j:j + half], SWIGLU_LIMIT)
        lin = jnp.clip(h[:, j + half:j + PAIR], -SWIGLU_LIMIT, SWIGLU_LIMIT)
        acts.append((glu * jax.nn.sigmoid(SWIGLU_ALPHA * glu) * (lin + 1.0)).astype(BF16))
    o_ref[...] = _dot(jnp.concatenate(acts, axis=1), w2b_ref[...]) + b2_ref[0]


def _moe_experts(block_exp, xs, w1, b1, w2, b2):
    n_slots, d = xs.shape
    f2 = w1.shape[-1]
    f = f2 // 2
    blk = MOE_BLOCK
    n_blocks = n_slots // blk
    b1p = b1.reshape(N_EXPERTS, f2 // PAIR, PAIR // 2, 2).transpose(0, 1, 3, 2).reshape(N_EXPERTS, 1, f2)
    wspec = lambda shape: pl.BlockSpec((1,) + shape, lambda i, be: (be[i], 0, 0))
    return pl.pallas_call(
        _moe_kernel,
        grid_spec=pltpu.PrefetchScalarGridSpec(
            num_scalar_prefetch=1,
            grid=(n_blocks,),
            in_specs=[pl.BlockSpec((blk, d), lambda i, be: (i, 0)),
                      wspec((d, f2)), wspec((1, f2)), wspec((f, d)), wspec((1, d))],
            out_specs=pl.BlockSpec((blk, d), lambda i, be: (i, 0)),
            scratch_shapes=[pltpu.VMEM((d, f2), BF16), pltpu.VMEM((f, d), BF16)]),
        out_shape=jax.ShapeDtypeStruct((n_slots, d), F32),
        compiler_params=pltpu.CompilerParams(
            dimension_semantics=("arbitrary",), vmem_limit_bytes=VMEM_LIMIT),
        name="moe_experts",
    )(block_exp, xs, w1, b1p, w2, b2.reshape(N_EXPERTS, 1, d))


def _ln_rows(y, g, b):
    yc = y - jnp.mean(y, axis=-1, keepdims=True)
    var = jnp.mean(yc * yc, axis=-1, keepdims=True)
    return yc * lax.rsqrt(var + NORM_EPS) * g + b


def _merge_ln_kernel(oa_ref, ob_ref, gate_ref, x_ref, pa_ref, pb_ref, wo_ref, g_ref, b_ref, o_ref):
    d = x_ref.shape[-1]
    merged = (jax.nn.sigmoid(gate_ref[:, :d]) * _dot(oa_ref[...], pa_ref[...])
              + jax.nn.sigmoid(gate_ref[:, d:]) * _dot(ob_ref[...], pb_ref[...]))
    o_ref[...] = _ln_rows(DN_ALPHA * x_ref[...] + _dot(merged, wo_ref[...]), g_ref[...], b_ref[...])


def _merge_ln(o_a, o_b, p_gate, x, proj_a, proj_b, w_out, g, b, *, tm=512):
    t, d = x.shape
    row = lambda i: (i, 0)
    fixed = lambda i: (0, 0)
    wspec = pl.BlockSpec((d, d), fixed)
    return pl.pallas_call(
        _merge_ln_kernel,
        grid=(t // tm,),
        in_specs=[pl.BlockSpec((tm, d), row), pl.BlockSpec((tm, d), row), pl.BlockSpec((tm, 2 * d), row),
                  pl.BlockSpec((tm, d), row), wspec, wspec, wspec,
                  pl.BlockSpec((1, d), fixed), pl.BlockSpec((1, d), fixed)],
        out_specs=pl.BlockSpec((tm, d), row),
        out_shape=jax.ShapeDtypeStruct((t, d), F32),
        compiler_params=pltpu.CompilerParams(
            dimension_semantics=("arbitrary",), vmem_limit_bytes=VMEM_LIMIT),
        name="merge_ln",
    )(o_a, o_b, p_gate, x, proj_a.astype(BF16), proj_b.astype(BF16), w_out.astype(BF16),
      g.reshape(1, d), b.reshape(1, d))


def _combine_ln_kernel(x_ref, yg_ref, gate_ref, g_ref, b_ref, o_ref):
    gate = gate_ref[...]
    moe = gate[:, 0:1] * yg_ref[0]
    for kk in range(1, TOP_K):
        moe = moe + gate[:, kk:kk + 1] * yg_ref[kk]
    o_ref[...] = _ln_rows(DN_ALPHA * x_ref[...] + moe, g_ref[...], b_ref[...])


def _combine_ln(x, yg, gates, g, b, *, tm=512):
    t, d = x.shape
    row = lambda i: (i, 0)
    fixed = lambda i: (0, 0)
    return pl.pallas_call(
        _combine_ln_kernel,
        grid=(t // tm,),
        in_specs=[pl.BlockSpec((tm, d), row), pl.BlockSpec((TOP_K, tm, d), lambda i: (0, i, 0)),
                  pl.BlockSpec((tm, TOP_K), row),
                  pl.BlockSpec((1, d), fixed), pl.BlockSpec((1, d), fixed)],
        out_specs=pl.BlockSpec((tm, d), row),
        out_shape=jax.ShapeDtypeStruct((t, d), F32),
        compiler_params=pltpu.CompilerParams(
            dimension_semantics=("arbitrary",), vmem_limit_bytes=VMEM_LIMIT),
        name="combine_ln",
    )(x, yg, gates, g.reshape(1, d), b.reshape(1, d))


def _rw_heads(t):
    return t.reshape(t.shape[0], RW_HEADS, RW_HEAD)


def _moe(x2, router_w, router_b, w1, b1, w2, b2):
    t, d = x2.shape
    n_assign = t * TOP_K
    n_blocks = -(-n_assign // MOE_BLOCK) + N_EXPERTS
    n_slots = n_blocks * MOE_BLOCK
    npad = 128 - N_EXPERTS
    logits = _matmul_f32(x2, jnp.pad(router_w, ((0, 0), (0, npad))),
                         jnp.pad(router_b, (0, npad)))[:, :N_EXPERTS]
    top_val, top_idx = lax.top_k(logits, TOP_K)
    gates = jax.nn.softmax(top_val, axis=-1)
    e_flat = top_idx.reshape(-1).astype(jnp.int32)
    order = jnp.argsort(e_flat).astype(jnp.int32)
    rank = jnp.argsort(order).astype(jnp.int32)
    experts = jnp.arange(N_EXPERTS, dtype=jnp.int32)
    counts = jnp.bincount(e_flat, length=N_EXPERTS).astype(jnp.int32)
    starts = jnp.cumsum(counts) - counts
    padded = (counts + MOE_BLOCK - 1) // MOE_BLOCK * MOE_BLOCK
    pad_ends = jnp.cumsum(padded)
    pad_starts = pad_ends - padded
    shift = pad_starts - starts
    slot_of = rank + jnp.sum(jnp.where(e_flat[:, None] == experts, shift, 0), axis=1)
    block_start = jnp.arange(n_blocks, dtype=jnp.int32) * MOE_BLOCK
    block_exp = jnp.minimum(jnp.searchsorted(pad_ends, block_start, side='right'),
                            N_EXPERTS - 1).astype(jnp.int32)
    blk_hot = block_exp[:, None] == experts
    per_slot = lambda tab: jnp.repeat(jnp.sum(jnp.where(blk_hot, tab, 0), axis=1), MOE_BLOCK)
    slot = jnp.arange(n_slots, dtype=jnp.int32)
    valid = slot - per_slot(pad_starts) < per_slot(counts)
    src = jnp.clip(slot - per_slot(shift), 0, n_assign - 1)
    slot_tok = jnp.where(valid, order[src] // TOP_K, 0)
    y = _moe_experts(block_exp, x2[slot_tok], w1, b1, w2, b2)
    return y[slot_of.reshape(t, TOP_K).T.reshape(-1)].reshape(TOP_K, t, d), gates


def kernel(x, w_in, hg_lb_logits, hg_norm_w, rw_mu, rw_w0, rw_w_up, rw_a0, rw_a_up, rw_g_up,
           rw_k_k, rw_k_a, rw_r_k, rw_lnx_w, rw_lnx_b, rw_v_down, rw_v_up, rw_v0, proj_a, proj_b,
           w_out, ln1_g, ln1_b, router_w, router_b, moe_w1, moe_b1, moe_w2, moe_b2, ln2_g, ln2_b):
    bsz, s, d = x.shape
    t = bsz * s
    scan = dict(bsz=bsz, seq=s)
    lb_all = jnp.cumsum(jax.nn.softmax(hg_lb_logits.astype(F32), axis=0), axis=0)
    lb_all = lb_all - lb_all[0:1]
    hg_cols = 5 * HG_WIDTH
    rw_cols = 3 * RW_WIDTH + 2 * DECAY_LORA + AAA_LORA + GATE_LORA
    x2 = x.reshape(t, d)
    v_first = None
    for l in range(DEPTH):
        w_l = w_in[l].astype(BF16)
        p_hg = _matmul(x2, w_l[:, :hg_cols], tm=1024, tn=1024)
        p_gate = _matmul(x2, w_l[:, hg_cols + rw_cols:], tm=1024, tn=1024)
        w_rw = w_in[l][:, hg_cols:hg_cols + rw_cols]
        w_self, w_nb = w_rw * (1.0 - rw_mu[l]), w_rw * rw_mu[l]
        if l > 0:
            w_self = jnp.concatenate([w_self, rw_v_down[l - 1]], axis=1)
        pad_cols = lambda w: jnp.pad(w, ((0, 0), (0, 3456 - w.shape[1]))).astype(BF16)
        x3 = jnp.pad(x2.reshape(bsz, s, d), ((0, 0), (1, 1), (0, 0)))
        x_nb = (0.5 * (x3[:, :-2] + x3[:, 2:])).reshape(t, d)
        ps = _matmul2(x2, x_nb, pad_cols(w_self), pad_cols(w_nb))

        o_a = _hgrn_scan(p_hg, lb_all[l], o_fwd=_hgrn_scan(p_hg, lb_all[l], **scan),
                         norm_w=hg_norm_w[l], **scan)

        r, k, v = (ps[:, i * RW_WIDTH:(i + 1) * RW_WIDTH] for i in range(3))
        off = 3 * RW_WIDTH
        wl = (ps[:, off:off + DECAY_LORA], ps[:, off + DECAY_LORA:off + 2 * DECAY_LORA])
        off += 2 * DECAY_LORA
        a_lo = ps[:, off:off + AAA_LORA]
        g_lo = ps[:, off + AAA_LORA:off + AAA_LORA + GATE_LORA]
        lws = []
        for dd in range(2):
            w = -jax.nn.softplus(-(rw_w0[l, dd] + _matmul(jnp.tanh(wl[dd]), rw_w_up[l, dd]))) - 0.5
            lws.append(-jnp.exp(w))
        a = jax.nn.sigmoid(rw_a0[l] + _matmul(a_lo, rw_a_up[l]))
        g = _matmul(jax.nn.sigmoid(g_lo), rw_g_up[l])
        if l == 0:
            v_first = v
        else:
            xv = ps[:, rw_cols:rw_cols + rw_v_down.shape[-1]]
            v = v + (v_first - v) * jax.nn.sigmoid(rw_v0[l - 1] + _matmul(xv, rw_v_up[l - 1]))
        kk = _rw_heads(k * rw_k_k[l])
        kk = kk / jnp.maximum(jnp.sqrt(jnp.sum(kk * kk, axis=-1, keepdims=True)), 1e-12)
        kk = kk.reshape(t, RW_WIDTH)
        k = k * (1.0 + (a - 1.0) * rw_k_a[l])
        y_f, y_b = _rwkv_scan(r, k, v, kk, a, lws[0], lws[1], **scan)
        y = _rw_heads(y_f + y_b)
        yc = y - jnp.mean(y, axis=-1, keepdims=True)
        yn = yc * lax.rsqrt(jnp.mean(yc * yc, axis=-1, keepdims=True) + RW_LN_EPS)
        yn = yn * _rw_heads(rw_lnx_w[l][None])[0] + _rw_heads(rw_lnx_b[l][None])[0]
        bonus = jnp.sum(_rw_heads(r) * _rw_heads(k) * rw_r_k[l], axis=-1, keepdims=True) * _rw_heads(v)
        o_b = (yn + bonus).reshape(t, RW_WIDTH) * g

        x2 = _merge_ln(o_a, o_b, p_gate, x2, proj_a[l], proj_b[l], w_out[l], ln1_g[l], ln1_b[l])
        yg, gates = _moe(x2, router_w[l], router_b[l], moe_w1[l], moe_b1[l], moe_w2[l], moe_b2[l])
        x2 = _combine_ln(x2, yg, gates, ln2_g[l], ln2_b[l])
    return x2.reshape(bsz, s, d)
```

```python
import functools

import jax
import jax.numpy as jnp
from jax import lax
from jax.experimental import pallas as pl
from jax.experimental.pallas import tpu as pltpu

F32 = jnp.float32
BF16 = jnp.bfloat16

D_MODEL = 1024
DEPTH = 4
HG_HEADS = 8
HG_D = 128
HG_WIDTH = HG_HEADS * HG_D
RW_HEAD = 64
RW_HEADS = D_MODEL // RW_HEAD
RW_WIDTH = D_MODEL
DECAY_LORA = 64
AAA_LORA = 64
GATE_LORA = 128
N_EXPERTS = 32
TOP_K = 4
MOE_BLOCK = 256
SWIGLU_ALPHA = 1.702
SWIGLU_LIMIT = 7.0
NORM_EPS = 1e-5
RW_LN_EPS = 64e-5
DN_ALPHA = (2 * DEPTH) ** 0.25

CHUNK = 64
SUB = 16
RW_LANES = 256
RW_GROUPS = 2
VMEM_LIMIT = 56 * 1024 * 1024

_NT = (((1,), (1,)), ((), ()))
_TN = (((0,), (0,)), ((), ()))


def _dot(a, b):
    return jnp.dot(a.astype(BF16), b.astype(BF16), preferred_element_type=F32)


def _dot_nt(a, b):
    return lax.dot_general(a.astype(BF16), b.astype(BF16), _NT, preferred_element_type=F32)


def _dot_tn(a, b):
    return lax.dot_general(a.astype(BF16), b.astype(BF16), _TN, preferred_element_type=F32)


def _dot_f32(a, b):
    return jnp.dot(a, b, preferred_element_type=F32, precision=lax.Precision.HIGHEST)


def _mm_kernel(x_ref, w_ref, o_ref):
    o_ref[...] = _dot(x_ref[...], w_ref[...]).astype(o_ref.dtype)


def _matmul(x, w, *, tm=512, tn=512, out_dtype=F32):
    m, k = x.shape
    n = w.shape[1]
    tm = min(tm, m)
    tn = min(tn, n)
    assert m % tm == 0 and n % tn == 0
    return pl.pallas_call(
        _mm_kernel,
        grid=(n // tn, m // tm),
        in_specs=[pl.BlockSpec((tm, k), lambda j, i: (i, 0)),
                  pl.BlockSpec((k, tn), lambda j, i: (0, j))],
        out_specs=pl.BlockSpec((tm, tn), lambda j, i: (i, j)),
        out_shape=jax.ShapeDtypeStruct((m, n), out_dtype),
        compiler_params=pltpu.CompilerParams(
            dimension_semantics=("arbitrary", "arbitrary"), vmem_limit_bytes=VMEM_LIMIT),
        name="matmul",
    )(x, w)


def _mm2_kernel(x_ref, y_ref, wx_ref, wy_ref, o_ref):
    o_ref[...] = _dot(x_ref[...], wx_ref[...]) + _dot(y_ref[...], wy_ref[...])


def _matmul2(x, y, wx, wy, *, tm=1024, tn=1152):
    m, k = x.shape
    n = wx.shape[1]
    assert m % tm == 0 and n % tn == 0
    xspec = pl.BlockSpec((tm, k), lambda j, i: (i, 0))
    wspec = pl.BlockSpec((k, tn), lambda j, i: (0, j))
    return pl.pallas_call(
        _mm2_kernel,
        grid=(n // tn, m // tm),
        in_specs=[xspec, xspec, wspec, wspec],
        out_specs=pl.BlockSpec((tm, tn), lambda j, i: (i, j)),
        out_shape=jax.ShapeDtypeStruct((m, n), F32),
        compiler_params=pltpu.CompilerParams(
            dimension_semantics=("arbitrary", "arbitrary"), vmem_limit_bytes=VMEM_LIMIT),
        name="matmul2",
    )(x, y, wx, wy)


def _mm_f32_kernel(x_ref, w_ref, b_ref, o_ref):
    o_ref[...] = _dot_f32(x_ref[...], w_ref[...]) + b_ref[...]


def _matmul_f32(x, w, b, *, tm=512):
    m, k = x.shape
    n = w.shape[1]
    return pl.pallas_call(
        _mm_f32_kernel,
        grid=(m // tm,),
        in_specs=[pl.BlockSpec((tm, k), lambda i: (i, 0)),
                  pl.BlockSpec((k, n), lambda i: (0, 0)),
                  pl.BlockSpec((1, n), lambda i: (0, 0))],
        out_specs=pl.BlockSpec((tm, n), lambda i: (i, 0)),
        out_shape=jax.ShapeDtypeStruct((m, n), F32),
        compiler_params=pltpu.CompilerParams(
            dimension_semantics=("arbitrary",), vmem_limit_bytes=VMEM_LIMIT),
        name="matmul_f32",
    )(x, w, b.reshape(1, n))


def _hgrn_kernel(q_ref, z_ref, v_ref, lb_ref, *rest, reverse, n_chunks):
    if reverse:
        of_ref, og_ref, nw_ref, o_ref, st_ref = rest
    else:
        o_ref, st_ref = rest
    @pl.when(pl.program_id(2) == 0)
    def _():
        st_ref[...] = jnp.zeros_like(st_ref)

    c = CHUNK
    lb = lb_ref[...]
    row = lax.broadcasted_iota(jnp.int32, (c, c), 0)
    col = lax.broadcasted_iota(jnp.int32, (c, c), 1)
    tri = jnp.where((col >= row) if reverse else (col <= row), 1.0, 0.0).astype(F32)
    srow = lax.broadcasted_iota(jnp.int32, (SUB, 1), 0)

    splits = []
    size = c
    while size > SUB:
        for lo in range(0, c, size):
            splits.append((lo, lo + size // 2, lo + size))
        size //= 2

    order = range(n_chunks - 1, -1, -1) if reverse else range(n_chunks)
    for ci in order:
        r0 = ci * c
        z = z_ref[r0:r0 + c, :]
        f = lb + (1.0 - lb) * jax.nn.sigmoid(z)
        lf = jnp.log(f)
        kk = 1.0 - f
        q = q_ref[r0:r0 + c, :]
        v = v_ref[r0:r0 + c, :]
        b = _dot_f32(tri, lf)
        b_end = b[0:1, :] if reverse else b[c - 1:c, :]
        st = st_ref[...]
        o_ref[r0:r0 + c, :] = _dot_nt(q * jnp.exp(b), st)
        st_ref[...] = st * jnp.exp(b_end) + _dot_tn(v, kk * jnp.exp(b_end - b))

        for lo, mid, hi in splits:
            if reverse:
                anc = b[mid:mid + 1, :]
                qs, ks = slice(lo, mid), slice(mid, hi)
            else:
                anc = b[mid - 1:mid, :]
                qs, ks = slice(mid, hi), slice(lo, mid)
            s = _dot_nt(q[qs] * jnp.exp(b[qs] - anc), kk[ks] * jnp.exp(anc - b[ks]))
            o_ref[r0 + qs.start:r0 + qs.stop, :] += _dot(s, v[ks])

        for d0 in range(0, c, SUB):
            qb, bb, kb, vb = (t[d0:d0 + SUB] for t in (q, b, kk, v))
            acc = jnp.zeros((SUB, HG_D), F32)
            for s in range(SUB):
                mask = (srow <= s) if reverse else (srow >= s)
                e = jnp.where(mask, jnp.exp(jnp.minimum(bb - bb[s:s + 1, :], 0.0)), 0.0)
                w = jnp.sum(qb * kb[s:s + 1, :] * e, axis=-1, keepdims=True)
                acc = acc + w * vb[s:s + 1, :]
            o_ref[r0 + d0:r0 + d0 + SUB, :] += acc

        if reverse:
            o = o_ref[r0:r0 + c, :] + of_ref[r0:r0 + c, :]
            o = o * lax.rsqrt(jnp.mean(o * o, axis=-1, keepdims=True) + NORM_EPS) * nw_ref[...]
            o_ref[r0:r0 + c, :] = o * jax.nn.silu(og_ref[r0:r0 + c, :])


def _hgrn_scan(p_hg, lb, *, bsz, seq, o_fwd=None, norm_w=None, tb=512):
    reverse = o_fwd is not None
    t = bsz * seq
    tb = min(tb, seq)
    nblk = seq // tb
    h = HG_HEADS
    zoff = 2 * h if reverse else h

    def col(off):
        return pl.BlockSpec((tb, HG_D),
                            lambda b, hh, i: (b * nblk + (nblk - 1 - i if reverse else i), off + hh))

    vec = pl.BlockSpec((1, HG_D), lambda b, hh, i: (0, 0))
    in_specs = [col(0), col(zoff), col(3 * h), pl.BlockSpec((1, HG_D), lambda b, hh, i: (0, hh))]
    args = [p_hg, p_hg, p_hg, lb.reshape(1, -1)]
    if reverse:
        in_specs += [col(0), col(4 * h), vec]
        args += [o_fwd, p_hg, norm_w.reshape(1, -1)]
    return pl.pallas_call(
        functools.partial(_hgrn_kernel, reverse=reverse, n_chunks=tb // CHUNK),
        grid=(bsz, h, nblk),
        in_specs=in_specs,
        out_specs=col(0),
        out_shape=jax.ShapeDtypeStruct((t, h * HG_D), F32),
        scratch_shapes=[pltpu.VMEM((HG_D, HG_D), F32)],
        compiler_params=pltpu.CompilerParams(
            dimension_semantics=("arbitrary", "arbitrary", "arbitrary"),
            vmem_limit_bytes=VMEM_LIMIT),
        name="hgrn_bwd" if reverse else "hgrn_fwd",
    )(*args)


class _RwkvMasks:
    def __init__(self, reverse):
        c, n = CHUNK, RW_LANES
        m = (n // RW_HEAD) * c
        row = lax.broadcasted_iota(jnp.int32, (c, c), 0)
        col = lax.broadcasted_iota(jnp.int32, (c, c), 1)
        self.tri = jnp.where((col >= row) if reverse else (col <= row), 1.0, 0.0).astype(F32)
        self.stack = (lax.broadcasted_iota(jnp.int32, (m, n), 0) // c
                      == lax.broadcasted_iota(jnp.int32, (m, n), 1) // RW_HEAD)
        tr = lax.broadcasted_iota(jnp.int32, (c, m), 0)
        tc = lax.broadcasted_iota(jnp.int32, (c, m), 1) % c
        self.strict = (tc > tr) if reverse else (tc < tr)
        self.incl = (tc >= tr) if reverse else (tc <= tr)
        self.eye = jnp.where(tc == tr, 1.0, 0.0).astype(F32)
        self.diag = (lax.broadcasted_iota(jnp.int32, (n, n), 0) // RW_HEAD
                     == lax.broadcasted_iota(jnp.int32, (n, n), 1) // RW_HEAD)


def _rwkv_stack(t, masks):
    return jnp.where(masks.stack, jnp.concatenate([t] * (RW_LANES // RW_HEAD), axis=0), 0.0).astype(BF16)


def _lockstep(gens):
    gens = list(gens)
    while gens:
        alive = []
        for gen in gens:
            try:
                next(gen)
                alive.append(gen)
            except StopIteration:
                pass
        gens = alive


def _rwkv_prepare(p, refs, rows, lanes, masks, reverse):
    c = CHUNK
    m = (RW_LANES // RW_HEAD) * c
    stack = functools.partial(_rwkv_stack, masks=masks)
    r, k, v, kk, a, lw = (ref[rows, lanes] for ref in refs)
    lp = _dot_f32(masks.tri, lw)
    yield
    lp_end = lp[0:1, :] if reverse else lp[c - 1:c, :]
    pinv = jnp.exp(-lp)
    dec = jnp.exp(lp_end - lp)
    kb = kk * a
    ar = jnp.concatenate([-kk * jnp.exp(lp - lw), r * jnp.exp(lp)], axis=0).astype(BF16)
    bk = jnp.concatenate([stack(kb * pinv), stack(k * pinv)], axis=0)
    v_s = stack(v)
    sc = _dot_nt(ar, bk)
    yield
    ab = jnp.where(masks.strict, sc[:c, :m], 0.0)
    akrk = jnp.concatenate([jnp.where(masks.strict, sc[:c, m:], 0.0),
                            jnp.where(masks.incl, sc[c:, m:], 0.0)], axis=0)
    p.update(ar=ar, v=v.astype(BF16), akrk_v=_dot(akrk, v_s),
             rb=jnp.where(masks.incl, sc[c:, :m], 0.0).astype(BF16),
             pw=_dot(ab, stack(ab)), tinv=masks.eye + ab,
             bkp=jnp.concatenate([kb * dec, k * dec], axis=0).astype(BF16),
             gdec=jnp.exp(lp_end))


def _rwkv_double(p, masks, last):
    c = CHUNK
    pw_s = _rwkv_stack(p["pw"], masks)
    if last:
        p["tinv"] = p["tinv"] + _dot(p["tinv"], pw_s)
    else:
        z = _dot(jnp.concatenate([p["pw"], p["tinv"]], axis=0), pw_s)
        p["pw"], p["tinv"] = z[:c], p["tinv"] + z[c:]


def _rwkv_chain(preps, g_ref, y_ref, rows_seq, lanes, masks):
    c = CHUNK
    for p, rows in zip(preps, rows_seq):
        g = g_ref[...]
        arg = _dot_nt(p["ar"], g)
        yield
        u = _dot(p["tinv"], _rwkv_stack(arg[:c] + p["akrk_v"][:c], masks))
        yield
        y_ref[rows, lanes] = arg[c:] + p["akrk_v"][c:] + _dot(p["rb"], _rwkv_stack(u, masks))
        uv = jnp.concatenate([u.astype(BF16), p["v"]], axis=0)
        g_ref[...] = g * p["gdec"] + jnp.where(masks.diag, _dot_tn(uv, p["bkp"]), 0.0)
        yield


def _rwkv_kernel(*refs, n_chunks):
    in_f, in_b = refs[0:6], refs[6:12]
    y_f, y_b, g_f, g_b = refs[12:16]

    @pl.when(pl.program_id(2) == 0)
    def _():
        g_f[...] = jnp.zeros_like(g_f)
        g_b[...] = jnp.zeros_like(g_b)

    c = CHUNK
    rows = [slice(ci * c, (ci + 1) * c) for ci in range(n_chunks)]
    masks = (_RwkvMasks(False), _RwkvMasks(True))
    plan = []
    for gi in range(RW_GROUPS):
        lanes = slice(gi * RW_LANES, (gi + 1) * RW_LANES)
        plan.append((in_f, y_f, g_f.at[gi], masks[0], False, rows, lanes))
        plan.append((in_b, y_b, g_b.at[gi], masks[1], True, rows[::-1], lanes))
    preps = [[{} for _ in rows] for _ in plan]
    _lockstep(_rwkv_prepare(p, ins, rw, lanes, mk, rev)
              for (ins, _, _, mk, rev, rws, lanes), plist in zip(plan, preps)
              for p, rw in zip(plist, rws))
    n_double = c.bit_length() - 2
    for it in range(n_double):
        for chain, plist in zip(plan, preps):
            for p in plist:
                _rwkv_double(p, chain[3], last=it == n_double - 1)
    _lockstep(_rwkv_chain(plist, g_ref, y_ref, rws, lanes, mk)
              for (_, y_ref, g_ref, mk, _, rws, lanes), plist in zip(plan, preps))


def _rwkv_scan(r, k, v, kk, a, lw_f, lw_b, *, bsz, seq, tb=128):
    t = bsz * seq
    tb = min(tb, seq)
    nblk = seq // tb
    n = RW_LANES * RW_GROUPS
    spec_f = pl.BlockSpec((tb, n), lambda b, hh, i: (b * nblk + i, hh))
    spec_b = pl.BlockSpec((tb, n), lambda b, hh, i: (b * nblk + nblk - 1 - i, hh))
    out = jax.ShapeDtypeStruct((t, RW_WIDTH), F32)
    state = pltpu.VMEM((RW_GROUPS, RW_LANES, RW_LANES), F32)
    return pl.pallas_call(
        functools.partial(_rwkv_kernel, n_chunks=tb // CHUNK),
        grid=(bsz, RW_WIDTH // n, nblk),
        in_specs=[spec_f] * 6 + [spec_b] * 6,
        out_specs=[spec_f, spec_b],
        out_shape=[out, out],
        scratch_shapes=[state, state],
        compiler_params=pltpu.CompilerParams(
            dimension_semantics=("arbitrary", "arbitrary", "arbitrary"),
            vmem_limit_bytes=VMEM_LIMIT),
        name="rwkv",
    )(r, k, v, kk, a, lw_f, r, k, v, kk, a, lw_b)


LORA_COLS = 384


def _rwkv_prep_kernel(ps_ref, vf_ref, wup_ref, aup_ref, gup_ref, vup_ref, vec_ref, seg_ref,
                      lwf_ref, lwb_ref, a_ref, k2_ref, kk_ref, v2_ref, g_ref, *, mix):
    w = RW_WIDTH
    k = ps_ref[:, w:2 * w]
    v = ps_ref[:, 2 * w:3 * w]
    lora = ps_ref[:, 3 * w:3 * w + LORA_COLS]
    vec = vec_ref[...]
    for d, out_ref in ((0, lwf_ref), (1, lwb_ref)):
        z = vec[d:d + 1] + _dot(jnp.tanh(lora[:, d * DECAY_LORA:(d + 1) * DECAY_LORA]), wup_ref[d])
        softplus_neg = jnp.maximum(-z, 0.0) + jnp.log(1.0 + jnp.exp(-jnp.abs(z)))
        out_ref[...] = -jnp.exp(-softplus_neg - 0.5)
    off = 2 * DECAY_LORA
    a = jax.nn.sigmoid(vec[2:3] + _dot(lora[:, off:off + AAA_LORA], aup_ref[...]))
    off += AAA_LORA
    g_ref[...] = _dot(jax.nn.sigmoid(lora[:, off:off + GATE_LORA]), gup_ref[...])
    off += GATE_LORA
    if mix:
        v = v + (vf_ref[...] - v) * jax.nn.sigmoid(vec[3:4] + _dot(lora[:, off:LORA_COLS], vup_ref[...]))
    v2_ref[...] = v
    kx = k * vec[4:5]
    sq = kx * kx
    sq_hi = sq.astype(BF16)
    sq_lo = (sq - sq_hi.astype(F32)).astype(BF16)
    norm2 = _dot(sq_hi, seg_ref[...]) + _dot(sq_lo, seg_ref[...])
    kk_ref[...] = kx / jnp.maximum(jnp.sqrt(norm2), 1e-12)
    k2_ref[...] = k * (1.0 + (a - 1.0) * vec[5:6])
    a_ref[...] = a


def _rwkv_prep(ps, v_first, w_up, a_up, g_up, v_up, vec, *, tm=256):
    t = ps.shape[0]
    w = RW_WIDTH
    mix = v_first is not None
    if not mix:
        v_first = ps
        v_up = jnp.zeros((LORA_COLS - 2 * DECAY_LORA - AAA_LORA - GATE_LORA, w), F32)
    head = jnp.arange(w, dtype=jnp.int32) // RW_HEAD
    seg = (head[:, None] == head[None, :]).astype(BF16)
    row = lambda i: (i, 0)
    fixed2 = lambda i: (0, 0)
    full = lambda arr: pl.BlockSpec(arr.shape, (lambda i: (0, 0, 0)) if arr.ndim == 3 else fixed2)
    out = jax.ShapeDtypeStruct((t, w), F32)
    return pl.pallas_call(
        functools.partial(_rwkv_prep_kernel, mix=mix),
        grid=(t // tm,),
        in_specs=[pl.BlockSpec((tm, ps.shape[1]), row), pl.BlockSpec((tm, w), row),
                  full(w_up), full(a_up), full(g_up), full(v_up), full(vec), full(seg)],
        out_specs=[pl.BlockSpec((tm, w), row)] * 7,
        out_shape=[out] * 7,
        compiler_params=pltpu.CompilerParams(
            dimension_semantics=("arbitrary",), vmem_limit_bytes=VMEM_LIMIT),
        name="rwkv_prep",
    )(ps, v_first, w_up, a_up, g_up, v_up, vec, seg)


PAIR = 256


def _moe_kernel(be_ref, xs_ref, w1_ref, b1_ref, w2_ref, b2_ref, o_ref, w1p_ref, w2b_ref):
    i = pl.program_id(0)
    f2 = w1_ref.shape[-1]
    half = PAIR // 2

    @pl.when((i == 0) | (be_ref[i] != be_ref[jnp.maximum(i - 1, 0)]))
    def _():
        src = lax.broadcasted_iota(jnp.int32, (PAIR, PAIR), 0)
        dst = lax.broadcasted_iota(jnp.int32, (PAIR, PAIR), 1)
        perm = jnp.where(src == jnp.where(dst < half, 2 * dst, 2 * (dst - half) + 1), 1.0, 0.0)
        for j in range(0, f2, PAIR):
            w1p_ref[:, j:j + PAIR] = _dot(w1_ref[0, :, j:j + PAIR], perm).astype(BF16)
        w2b_ref[...] = w2_ref[0].astype(BF16)

    h = _dot(xs_ref[...], w1p_ref[...]) + b1_ref[0]
    acts = []
    for j in range(0, f2, PAIR):
        glu = jnp.minimum(h[:, j:j + half], SWIGLU_LIMIT)
        lin = jnp.clip(h[:, j + half:j + PAIR], -SWIGLU_LIMIT, SWIGLU_LIMIT)
        acts.append((glu * jax.nn.sigmoid(SWIGLU_ALPHA * glu) * (lin + 1.0)).astype(BF16))
    o_ref[...] = _dot(jnp.concatenate(acts, axis=1), w2b_ref[...]) + b2_ref[0]


def _moe_experts(block_exp, xs, w1, b1, w2, b2):
    n_slots, d = xs.shape
    f2 = w1.shape[-1]
    f = f2 // 2
    blk = MOE_BLOCK
    n_blocks = n_slots // blk
    b1p = b1.reshape(N_EXPERTS, f2 // PAIR, PAIR // 2, 2).transpose(0, 1, 3, 2).reshape(N_EXPERTS, 1, f2)
    wspec = lambda shape: pl.BlockSpec((1,) + shape, lambda i, be: (be[i], 0, 0))
    return pl.pallas_call(
        _moe_kernel,
        grid_spec=pltpu.PrefetchScalarGridSpec(
            num_scalar_prefetch=1,
            grid=(n_blocks,),
            in_specs=[pl.BlockSpec((blk, d), lambda i, be: (i, 0)),
                      wspec((d, f2)), wspec((1, f2)), wspec((f, d)), wspec((1, d))],
            out_specs=pl.BlockSpec((blk, d), lambda i, be: (i, 0)),
            scratch_shapes=[pltpu.VMEM((d, f2), BF16), pltpu.VMEM((f, d), BF16)]),
        out_shape=jax.ShapeDtypeStruct((n_slots, d), F32),
        compiler_params=pltpu.CompilerParams(
            dimension_semantics=("arbitrary",), vmem_limit_bytes=VMEM_LIMIT),
        name="moe_experts",
    )(block_exp, xs, w1, b1p, w2, b2.reshape(N_EXPERTS, 1, d))


def _ln_rows(y, g, b):
    yc = y - jnp.mean(y, axis=-1, keepdims=True)
    var = jnp.mean(yc * yc, axis=-1, keepdims=True)
    return yc * lax.rsqrt(var + NORM_EPS) * g + b


def _merge_ln_kernel(oa_ref, ob_ref, gate_ref, x_ref, pa_ref, pb_ref, wo_ref, g_ref, b_ref, o_ref):
    d = x_ref.shape[-1]
    merged = (jax.nn.sigmoid(gate_ref[:, :d]) * _dot(oa_ref[...], pa_ref[...])
              + jax.nn.sigmoid(gate_ref[:, d:]) * _dot(ob_ref[...], pb_ref[...]))
    o_ref[...] = _ln_rows(DN_ALPHA * x_ref[...] + _dot(merged, wo_ref[...]), g_ref[...], b_ref[...])


def _merge_ln(o_a, o_b, p_gate, x, proj_a, proj_b, w_out, g, b, *, tm=512):
    t, d = x.shape
    row = lambda i: (i, 0)
    fixed = lambda i: (0, 0)
    wspec = pl.BlockSpec((d, d), fixed)
    return pl.pallas_call(
        _merge_ln_kernel,
        grid=(t // tm,),
        in_specs=[pl.BlockSpec((tm, d), row), pl.BlockSpec((tm, d), row), pl.BlockSpec((tm, 2 * d), row),
                  pl.BlockSpec((tm, d), row), wspec, wspec, wspec,
                  pl.BlockSpec((1, d), fixed), pl.BlockSpec((1, d), fixed)],
        out_specs=pl.BlockSpec((tm, d), row),
        out_shape=jax.ShapeDtypeStruct((t, d), F32),
        compiler_params=pltpu.CompilerParams(
            dimension_semantics=("arbitrary",), vmem_limit_bytes=VMEM_LIMIT),
        name="merge_ln",
    )(o_a, o_b, p_gate, x, proj_a.astype(BF16), proj_b.astype(BF16), w_out.astype(BF16),
      g.reshape(1, d), b.reshape(1, d))


def _combine_ln_kernel(x_ref, yg_ref, gate_ref, g_ref, b_ref, o_ref):
    gate = gate_ref[...]
    moe = gate[:, 0:1] * yg_ref[0]
    for kk in range(1, TOP_K):
        moe = moe + gate[:, kk:kk + 1] * yg_ref[kk]
    o_ref[...] = _ln_rows(DN_ALPHA * x_ref[...] + moe, g_ref[...], b_ref[...])


def _combine_ln(x, yg, gates, g, b, *, tm=512):
    t, d = x.shape
    row = lambda i: (i, 0)
    fixed = lambda i: (0, 0)
    return pl.pallas_call(
        _combine_ln_kernel,
        grid=(t // tm,),
        in_specs=[pl.BlockSpec((tm, d), row), pl.BlockSpec((TOP_K, tm, d), lambda i: (0, i, 0)),
                  pl.BlockSpec((tm, TOP_K), row),
                  pl.BlockSpec((1, d), fixed), pl.BlockSpec((1, d), fixed)],
        out_specs=pl.BlockSpec((tm, d), row),
        out_shape=jax.ShapeDtypeStruct((t, d), F32),
        compiler_params=pltpu.CompilerParams(
            dimension_semantics=("arbitrary",), vmem_limit_bytes=VMEM_LIMIT),
        name="combine_ln",
    )(x, yg, gates, g.reshape(1, d), b.reshape(1, d))


def _rw_heads(t):
    return t.reshape(t.shape[0], RW_HEADS, RW_HEAD)


def _moe(x2, router_w, router_b, w1, b1, w2, b2):
    t, d = x2.shape
    n_assign = t * TOP_K
    n_blocks = -(-n_assign // MOE_BLOCK) + N_EXPERTS
    n_slots = n_blocks * MOE_BLOCK
    npad = 128 - N_EXPERTS
    logits = _matmul_f32(x2, jnp.pad(router_w, ((0, 0), (0, npad))),
                         jnp.pad(router_b, (0, npad)))[:, :N_EXPERTS]
    top_val, top_idx = lax.top_k(logits, TOP_K)
    gates = jax.nn.softmax(top_val, axis=-1)
    e_flat = top_idx.reshape(-1).astype(jnp.int32)
    order = jnp.argsort(e_flat).astype(jnp.int32)
    rank = jnp.argsort(order).astype(jnp.int32)
    experts = jnp.arange(N_EXPERTS, dtype=jnp.int32)
    counts = jnp.bincount(e_flat, length=N_EXPERTS).astype(jnp.int32)
    starts = jnp.cumsum(counts) - counts
    padded = (counts + MOE_BLOCK - 1) // MOE_BLOCK * MOE_BLOCK
    pad_ends = jnp.cumsum(padded)
    pad_starts = pad_ends - padded
    shift = pad_starts - starts
    slot_of = rank + jnp.sum(jnp.where(e_flat[:, None] == experts, shift, 0), axis=1)
    block_start = jnp.arange(n_blocks, dtype=jnp.int32) * MOE_BLOCK
    block_exp = jnp.minimum(jnp.searchsorted(pad_ends, block_start, side='right'),
                            N_EXPERTS - 1).astype(jnp.int32)
    blk_hot = block_exp[:, None] == experts
    per_slot = lambda tab: jnp.repeat(jnp.sum(jnp.where(blk_hot, tab, 0), axis=1), MOE_BLOCK)
    slot = jnp.arange(n_slots, dtype=jnp.int32)
    valid = slot - per_slot(pad_starts) < per_slot(counts)
    src = jnp.clip(slot - per_slot(shift), 0, n_assign - 1)
    slot_tok = jnp.where(valid, order[src] // TOP_K, 0)
    y = _moe_experts(block_exp, x2[slot_tok], w1, b1, w2, b2)
    return y[slot_of.reshape(t, TOP_K).T.reshape(-1)].reshape(TOP_K, t, d), gates


def kernel(x, w_in, hg_lb_logits, hg_norm_w, rw_mu, rw_w0, rw_w_up, rw_a0, rw_a_up, rw_g_up,
           rw_k_k, rw_k_a, rw_r_k, rw_lnx_w, rw_lnx_b, rw_v_down, rw_v_up, rw_v0, proj_a, proj_b,
           w_out, ln1_g, ln1_b, router_w, router_b, moe_w1, moe_b1, moe_w2, moe_b2, ln2_g, ln2_b):
    bsz, s, d = x.shape
    t = bsz * s
    scan = dict(bsz=bsz, seq=s)
    lb_all = jnp.cumsum(jax.nn.softmax(hg_lb_logits.astype(F32), axis=0), axis=0)
    lb_all = lb_all - lb_all[0:1]
    hg_cols = 5 * HG_WIDTH
    rw_cols = 3 * RW_WIDTH + 2 * DECAY_LORA + AAA_LORA + GATE_LORA
    x2 = x.reshape(t, d)
    v_first = None
    for l in range(DEPTH):
        w_l = w_in[l].astype(BF16)
        p_hg = _matmul(x2, w_l[:, :hg_cols], tm=1024, tn=1024)
        p_gate = _matmul(x2, w_l[:, hg_cols + rw_cols:], tm=1024, tn=1024)
        w_rw = w_in[l][:, hg_cols:hg_cols + rw_cols]
        w_self, w_nb = w_rw * (1.0 - rw_mu[l]), w_rw * rw_mu[l]
        if l > 0:
            w_self = jnp.concatenate([w_self, rw_v_down[l - 1]], axis=1)
        pad_cols = lambda w: jnp.pad(w, ((0, 0), (0, 3456 - w.shape[1]))).astype(BF16)
        x3 = jnp.pad(x2.reshape(bsz, s, d), ((0, 0), (1, 1), (0, 0)))
        x_nb = (0.5 * (x3[:, :-2] + x3[:, 2:])).reshape(t, d)
        ps = _matmul2(x2, x_nb, pad_cols(w_self), pad_cols(w_nb))

        o_a = _hgrn_scan(p_hg, lb_all[l], o_fwd=_hgrn_scan(p_hg, lb_all[l], **scan),
                         norm_w=hg_norm_w[l], **scan)

        r = ps[:, :RW_WIDTH]
        zeros = jnp.zeros((RW_WIDTH,), F32)
        if l == 0:
            v_first = ps[:, 2 * RW_WIDTH:3 * RW_WIDTH]
            mixing = dict(v_first=None, v_up=None)
            v0 = zeros
        else:
            v_up = rw_v_up[l - 1]
            mixing = dict(v_first=v_first, v_up=jnp.pad(v_up, ((0, 3456 - rw_cols - v_up.shape[0]), (0, 0))))
            v0 = rw_v0[l - 1]
        vec = jnp.stack([rw_w0[l, 0], rw_w0[l, 1], rw_a0[l], v0, rw_k_k[l], rw_k_a[l], zeros, zeros])
        lw_f, lw_b, a, k, kk, v, g = _rwkv_prep(ps, w_up=rw_w_up[l], a_up=rw_a_up[l], g_up=rw_g_up[l],
                                                vec=vec, **mixing)
        y_f, y_b = _rwkv_scan(ps, k, v, kk, a, lw_f, lw_b, **scan)
        y = _rw_heads(y_f + y_b)
        yc = y - jnp.mean(y, axis=-1, keepdims=True)
        yn = yc * lax.rsqrt(jnp.mean(yc * yc, axis=-1, keepdims=True) + RW_LN_EPS)
        yn = yn * _rw_heads(rw_lnx_w[l][None])[0] + _rw_heads(rw_lnx_b[l][None])[0]
        bonus = jnp.sum(_rw_heads(r) * _rw_heads(k) * rw_r_k[l], axis=-1, keepdims=True) * _rw_heads(v)
        o_b = (yn + bonus).reshape(t, RW_WIDTH) * g

        x2 = _merge_ln(o_a, o_b, p_gate, x2, proj_a[l], proj_b[l], w_out[l], ln1_g[l], ln1_b[l])
        yg, gates = _moe(x2, router_w[l], router_b[l], moe_w1[l], moe_b1[l], moe_w2[l], moe_b2[l])
        x2 = _combine_ln(x2, yg, gates, ln2_g[l], ln2_b[l])
    return x2.reshape(bsz, s, d)
```

```python
import functools

import jax
import jax.numpy as jnp
from jax import lax
from jax.experimental import pallas as pl
from jax.experimental.pallas import tpu as pltpu

F32 = jnp.float32
BF16 = jnp.bfloat16

D_MODEL = 1024
DEPTH = 4
HG_HEADS = 8
HG_D = 128
HG_WIDTH = HG_HEADS * HG_D
RW_HEAD = 64
RW_HEADS = D_MODEL // RW_HEAD
RW_WIDTH = D_MODEL
DECAY_LORA = 64
AAA_LORA = 64
GATE_LORA = 128
N_EXPERTS = 32
TOP_K = 4
MOE_BLOCK = 256
SWIGLU_ALPHA = 1.702
SWIGLU_LIMIT = 7.0
NORM_EPS = 1e-5
RW_LN_EPS = 64e-5
DN_ALPHA = (2 * DEPTH) ** 0.25

CHUNK = 64
SUB = 16
DIAG_LOG_RANGE = 60.0
RW_LANES = 256
RW_GROUPS = 2
VMEM_LIMIT = 56 * 1024 * 1024

_NT = (((1,), (1,)), ((), ()))
_TN = (((0,), (0,)), ((), ()))


def _dot(a, b):
    return jnp.dot(a.astype(BF16), b.astype(BF16), preferred_element_type=F32)


def _dot_nt(a, b):
    return lax.dot_general(a.astype(BF16), b.astype(BF16), _NT, preferred_element_type=F32)


def _dot_tn(a, b):
    return lax.dot_general(a.astype(BF16), b.astype(BF16), _TN, preferred_element_type=F32)


def _dot_f32(a, b):
    return jnp.dot(a, b, preferred_element_type=F32, precision=lax.Precision.HIGHEST)


def _mm_kernel(x_ref, w_ref, o_ref):
    o_ref[...] = _dot(x_ref[...], w_ref[...]).astype(o_ref.dtype)


def _matmul(x, w, *, tm=512, tn=512, out_dtype=F32):
    m, k = x.shape
    n = w.shape[1]
    tm = min(tm, m)
    tn = min(tn, n)
    assert m % tm == 0 and n % tn == 0
    return pl.pallas_call(
        _mm_kernel,
        grid=(n // tn, m // tm),
        in_specs=[pl.BlockSpec((tm, k), lambda j, i: (i, 0)),
                  pl.BlockSpec((k, tn), lambda j, i: (0, j))],
        out_specs=pl.BlockSpec((tm, tn), lambda j, i: (i, j)),
        out_shape=jax.ShapeDtypeStruct((m, n), out_dtype),
        compiler_params=pltpu.CompilerParams(
            dimension_semantics=("arbitrary", "arbitrary"), vmem_limit_bytes=VMEM_LIMIT),
        name="matmul",
    )(x, w)


def _mm2_kernel(x_ref, y_ref, wx_ref, wy_ref, o_ref):
    o_ref[...] = _dot(x_ref[...], wx_ref[...]) + _dot(y_ref[...], wy_ref[...])


def _matmul2(x, y, wx, wy, *, tm=1024, tn=1152):
    m, k = x.shape
    n = wx.shape[1]
    assert m % tm == 0 and n % tn == 0
    xspec = pl.BlockSpec((tm, k), lambda j, i: (i, 0))
    wspec = pl.BlockSpec((k, tn), lambda j, i: (0, j))
    return pl.pallas_call(
        _mm2_kernel,
        grid=(n // tn, m // tm),
        in_specs=[xspec, xspec, wspec, wspec],
        out_specs=pl.BlockSpec((tm, tn), lambda j, i: (i, j)),
        out_shape=jax.ShapeDtypeStruct((m, n), F32),
        compiler_params=pltpu.CompilerParams(
            dimension_semantics=("arbitrary", "arbitrary"), vmem_limit_bytes=VMEM_LIMIT),
        name="matmul2",
    )(x, y, wx, wy)


def _mm_f32_kernel(x_ref, w_ref, b_ref, o_ref):
    o_ref[...] = _dot_f32(x_ref[...], w_ref[...]) + b_ref[...]


def _matmul_f32(x, w, b, *, tm=512):
    m, k = x.shape
    n = w.shape[1]
    return pl.pallas_call(
        _mm_f32_kernel,
        grid=(m // tm,),
        in_specs=[pl.BlockSpec((tm, k), lambda i: (i, 0)),
                  pl.BlockSpec((k, n), lambda i: (0, 0)),
                  pl.BlockSpec((1, n), lambda i: (0, 0))],
        out_specs=pl.BlockSpec((tm, n), lambda i: (i, 0)),
        out_shape=jax.ShapeDtypeStruct((m, n), F32),
        compiler_params=pltpu.CompilerParams(
            dimension_semantics=("arbitrary",), vmem_limit_bytes=VMEM_LIMIT),
        name="matmul_f32",
    )(x, w, b.reshape(1, n))


def _hgrn_kernel(q_ref, z_ref, v_ref, lb_ref, *rest, reverse, n_chunks):
    st_ref = rest[-1]

    @pl.when(pl.program_id(2) == 0)
    def _():
        st_ref[...] = jnp.zeros_like(st_ref)

    lb = lb_ref[...]
    f = lb + (1.0 - lb) * jax.nn.sigmoid(z_ref[...])
    lf = jnp.log(f)
    sub_total = jnp.sum(lf.reshape(lf.shape[0] // SUB, SUB, HG_D), axis=1)
    factorable = jnp.min(sub_total) > -DIAG_LOG_RANGE
    args = (q_ref, f, lf, v_ref) + rest

    @pl.when(factorable)
    def _():
        _hgrn_block(*args, reverse=reverse, n_chunks=n_chunks, pairwise=False)

    @pl.when(jnp.logical_not(factorable))
    def _():
        _hgrn_block(*args, reverse=reverse, n_chunks=n_chunks, pairwise=True)


def _hgrn_block(q_ref, f_blk, lf_blk, v_ref, *rest, reverse, n_chunks, pairwise):
    if reverse:
        of_ref, og_ref, nw_ref, o_ref, st_ref = rest
    else:
        o_ref, st_ref = rest
    c = CHUNK
    row = lax.broadcasted_iota(jnp.int32, (c, c), 0)
    col = lax.broadcasted_iota(jnp.int32, (c, c), 1)
    causal = (col >= row) if reverse else (col <= row)
    tri = jnp.where(causal, 1.0, 0.0).astype(F32)
    diag_mask = causal & (row // SUB == col // SUB)
    srow = lax.broadcasted_iota(jnp.int32, (SUB, 1), 0)

    splits = []
    size = c
    while size > SUB:
        for lo in range(0, c, size):
            splits.append((lo, lo + size // 2, lo + size))
        size //= 2

    order = list(range(n_chunks - 1, -1, -1) if reverse else range(n_chunks))
    chunks = [dict(rows=slice(ci * c, (ci + 1) * c)) for ci in order]
    for ch in chunks:
        rows = ch["rows"]
        ch.update(lf=lf_blk[rows], kk=1.0 - f_blk[rows], q=q_ref[rows, :], v=v_ref[rows, :])
        ch["b"] = _dot_f32(tri, ch["lf"])
    for ch in chunks:
        q, kk, v, b, lf = ch["q"], ch["kk"], ch["v"], ch["b"], ch["lf"]
        b_end = b[0:1, :] if reverse else b[c - 1:c, :]
        ch["qdec"] = (q * jnp.exp(b)).astype(BF16)
        ch["dec"] = jnp.exp(b_end)
        ch["kv"] = _dot_tn(v, kk * jnp.exp(b_end - b))
        scores = []
        for lo, mid, hi in splits:
            if reverse:
                anc = b[mid:mid + 1, :]
                qs, ks = slice(lo, mid), slice(mid, hi)
            else:
                anc = b[mid - 1:mid, :]
                qs, ks = slice(mid, hi), slice(lo, mid)
            scores.append((qs, ks, _dot_nt(q[qs] * jnp.exp(b[qs] - anc), kk[ks] * jnp.exp(anc - b[ks]))))
        ch["scores"] = scores
        if not pairwise:
            edge = b - lf
            anc = jnp.concatenate(
                [jnp.broadcast_to(edge[d0 + SUB - 1:d0 + SUB] if reverse else edge[d0:d0 + 1], (SUB, HG_D))
                 for d0 in range(0, c, SUB)], axis=0)
            ch["diag"] = jnp.where(diag_mask, _dot_nt(q * jnp.exp(b - anc), kk * jnp.exp(anc - b)), 0.0)
    for ch in chunks:
        v = ch["v"]
        parts = [None] * (c // SUB)
        for qs, ks, s in ch["scores"]:
            contrib = _dot(s, v[ks])
            for j in range(qs.start // SUB, qs.stop // SUB):
                piece = contrib[j * SUB - qs.start:(j + 1) * SUB - qs.start]
                parts[j] = piece if parts[j] is None else parts[j] + piece
        intra = jnp.concatenate([jnp.zeros((SUB, HG_D), F32) if p is None else p for p in parts], axis=0)
        if pairwise:
            accs = []
            for d0 in range(0, c, SUB):
                qb, bb, kb, vb = (t[d0:d0 + SUB] for t in (ch["q"], ch["b"], ch["kk"], v))
                acc = jnp.zeros((SUB, HG_D), F32)
                for s in range(SUB):
                    mask = (srow <= s) if reverse else (srow >= s)
                    e = jnp.where(mask, jnp.exp(jnp.minimum(bb - bb[s:s + 1, :], 0.0)), 0.0)
                    w = jnp.sum(qb * kb[s:s + 1, :] * e, axis=-1, keepdims=True)
                    acc = acc + w * vb[s:s + 1, :]
                accs.append(acc)
            ch["intra"] = intra + jnp.concatenate(accs, axis=0)
        else:
            ch["intra"] = intra + _dot(ch["diag"], v)
    st = st_ref[...]
    for ch in chunks:
        ch["st"] = st
        st = st * ch["dec"] + ch["kv"]
    st_ref[...] = st
    for ch in chunks:
        rows = ch["rows"]
        o = ch["intra"] + _dot_nt(ch["qdec"], ch["st"])
        if reverse:
            o = o + of_ref[rows, :]
            o = o * lax.rsqrt(jnp.mean(o * o, axis=-1, keepdims=True) + NORM_EPS) * nw_ref[...]
            o = o * jax.nn.silu(og_ref[rows, :])
        o_ref[rows, :] = o


def _hgrn_scan(p_hg, lb, *, bsz, seq, o_fwd=None, norm_w=None, tb=512):
    reverse = o_fwd is not None
    t = bsz * seq
    tb = min(tb, seq)
    nblk = seq // tb
    h = HG_HEADS
    zoff = 2 * h if reverse else h

    def col(off):
        return pl.BlockSpec((tb, HG_D),
                            lambda b, hh, i: (b * nblk + (nblk - 1 - i if reverse else i), off + hh))

    vec = pl.BlockSpec((1, HG_D), lambda b, hh, i: (0, 0))
    in_specs = [col(0), col(zoff), col(3 * h), pl.BlockSpec((1, HG_D), lambda b, hh, i: (0, hh))]
    args = [p_hg, p_hg, p_hg, lb.reshape(1, -1)]
    if reverse:
        in_specs += [col(0), col(4 * h), vec]
        args += [o_fwd, p_hg, norm_w.reshape(1, -1)]
    return pl.pallas_call(
        functools.partial(_hgrn_kernel, reverse=reverse, n_chunks=tb // CHUNK),
        grid=(bsz, h, nblk),
        in_specs=in_specs,
        out_specs=col(0),
        out_shape=jax.ShapeDtypeStruct((t, h * HG_D), F32),
        scratch_shapes=[pltpu.VMEM((HG_D, HG_D), F32)],
        compiler_params=pltpu.CompilerParams(
            dimension_semantics=("arbitrary", "arbitrary", "arbitrary"),
            vmem_limit_bytes=VMEM_LIMIT),
        name="hgrn_bwd" if reverse else "hgrn_fwd",
    )(*args)


class _RwkvMasks:
    def __init__(self, reverse):
        c, n = CHUNK, RW_LANES
        m = (n // RW_HEAD) * c
        row = lax.broadcasted_iota(jnp.int32, (c, c), 0)
        col = lax.broadcasted_iota(jnp.int32, (c, c), 1)
        self.tri = jnp.where((col >= row) if reverse else (col <= row), 1.0, 0.0).astype(F32)
        self.stack = (lax.broadcasted_iota(jnp.int32, (m, n), 0) // c
                      == lax.broadcasted_iota(jnp.int32, (m, n), 1) // RW_HEAD)
        tr = lax.broadcasted_iota(jnp.int32, (c, m), 0)
        tc = lax.broadcasted_iota(jnp.int32, (c, m), 1) % c
        self.strict = (tc > tr) if reverse else (tc < tr)
        self.incl = (tc >= tr) if reverse else (tc <= tr)
        self.eye = jnp.where(tc == tr, 1.0, 0.0).astype(F32)
        self.diag = (lax.broadcasted_iota(jnp.int32, (n, n), 0) // RW_HEAD
                     == lax.broadcasted_iota(jnp.int32, (n, n), 1) // RW_HEAD)


def _rwkv_stack(t, masks):
    return jnp.where(masks.stack, jnp.concatenate([t] * (RW_LANES // RW_HEAD), axis=0), 0.0).astype(BF16)


def _lockstep(gens):
    gens = list(gens)
    while gens:
        alive = []
        for gen in gens:
            try:
                next(gen)
                alive.append(gen)
            except StopIteration:
                pass
        gens = alive


def _rwkv_prepare(p, refs, rows, lanes, masks, reverse):
    c = CHUNK
    m = (RW_LANES // RW_HEAD) * c
    stack = functools.partial(_rwkv_stack, masks=masks)
    r, k, v, kk, a, lw = (ref[rows, lanes] for ref in refs)
    lp = _dot_f32(masks.tri, lw)
    yield
    lp_end = lp[0:1, :] if reverse else lp[c - 1:c, :]
    pinv = jnp.exp(-lp)
    dec = jnp.exp(lp_end - lp)
    kb = kk * a
    ar = jnp.concatenate([-kk * jnp.exp(lp - lw), r * jnp.exp(lp)], axis=0).astype(BF16)
    bk = jnp.concatenate([stack(kb * pinv), stack(k * pinv)], axis=0)
    v_s = stack(v)
    sc = _dot_nt(ar, bk)
    yield
    ab = jnp.where(masks.strict, sc[:c, :m], 0.0)
    akrk = jnp.concatenate([jnp.where(masks.strict, sc[:c, m:], 0.0),
                            jnp.where(masks.incl, sc[c:, m:], 0.0)], axis=0)
    p.update(ar=ar, v=v.astype(BF16), akrk_v=_dot(akrk, v_s),
             rb=jnp.where(masks.incl, sc[c:, :m], 0.0).astype(BF16),
             pw=_dot(ab, stack(ab)), tinv=masks.eye + ab,
             bkp=jnp.concatenate([kb * dec, k * dec], axis=0).astype(BF16),
             gdec=jnp.exp(lp_end))


def _rwkv_double(p, masks, last):
    c = CHUNK
    pw_s = _rwkv_stack(p["pw"], masks)
    if last:
        p["tinv"] = p["tinv"] + _dot(p["tinv"], pw_s)
    else:
        z = _dot(jnp.concatenate([p["pw"], p["tinv"]], axis=0), pw_s)
        p["pw"], p["tinv"] = z[:c], p["tinv"] + z[c:]


def _rwkv_chain(preps, g_ref, y_ref, rows_seq, lanes, masks):
    c = CHUNK
    for p, rows in zip(preps, rows_seq):
        g = g_ref[...]
        arg = _dot_nt(p["ar"], g)
        yield
        u = _dot(p["tinv"], _rwkv_stack(arg[:c] + p["akrk_v"][:c], masks))
        yield
        y_ref[rows, lanes] = arg[c:] + p["akrk_v"][c:] + _dot(p["rb"], _rwkv_stack(u, masks))
        uv = jnp.concatenate([u.astype(BF16), p["v"]], axis=0)
        g_ref[...] = g * p["gdec"] + jnp.where(masks.diag, _dot_tn(uv, p["bkp"]), 0.0)
        yield


def _rwkv_kernel(*refs, n_chunks):
    in_f, in_b = refs[0:6], refs[6:12]
    y_f, y_b, g_f, g_b = refs[12:16]

    @pl.when(pl.program_id(2) == 0)
    def _():
        g_f[...] = jnp.zeros_like(g_f)
        g_b[...] = jnp.zeros_like(g_b)

    c = CHUNK
    rows = [slice(ci * c, (ci + 1) * c) for ci in range(n_chunks)]
    masks = (_RwkvMasks(False), _RwkvMasks(True))
    plan = []
    for gi in range(RW_GROUPS):
        lanes = slice(gi * RW_LANES, (gi + 1) * RW_LANES)
        plan.append((in_f, y_f, g_f.at[gi], masks[0], False, rows, lanes))
        plan.append((in_b, y_b, g_b.at[gi], masks[1], True, rows[::-1], lanes))
    preps = [[{} for _ in rows] for _ in plan]
    _lockstep(_rwkv_prepare(p, ins, rw, lanes, mk, rev)
              for (ins, _, _, mk, rev, rws, lanes), plist in zip(plan, preps)
              for p, rw in zip(plist, rws))
    n_double = c.bit_length() - 2
    for it in range(n_double):
        for chain, plist in zip(plan, preps):
            for p in plist:
                _rwkv_double(p, chain[3], last=it == n_double - 1)
    _lockstep(_rwkv_chain(plist, g_ref, y_ref, rws, lanes, mk)
              for (_, y_ref, g_ref, mk, _, rws, lanes), plist in zip(plan, preps))


def _rwkv_scan(r, k, v, kk, a, lw_f, lw_b, *, bsz, seq, tb=128):
    t = bsz * seq
    tb = min(tb, seq)
    nblk = seq // tb
    n = RW_LANES * RW_GROUPS
    spec_f = pl.BlockSpec((tb, n), lambda b, hh, i: (b * nblk + i, hh))
    spec_b = pl.BlockSpec((tb, n), lambda b, hh, i: (b * nblk + nblk - 1 - i, hh))
    out = jax.ShapeDtypeStruct((t, RW_WIDTH), F32)
    state = pltpu.VMEM((RW_GROUPS, RW_LANES, RW_LANES), F32)
    return pl.pallas_call(
        functools.partial(_rwkv_kernel, n_chunks=tb // CHUNK),
        grid=(bsz, RW_WIDTH // n, nblk),
        in_specs=[spec_f] * 6 + [spec_b] * 6,
        out_specs=[spec_f, spec_b],
        out_shape=[out, out],
        scratch_shapes=[state, state],
        compiler_params=pltpu.CompilerParams(
            dimension_semantics=("arbitrary", "arbitrary", "arbitrary"),
            vmem_limit_bytes=VMEM_LIMIT),
        name="rwkv",
    )(r, k, v, kk, a, lw_f, r, k, v, kk, a, lw_b)


LORA_COLS = 384


def _rwkv_prep_kernel(ps_ref, vf_ref, wup_ref, aup_ref, gup_ref, vup_ref, vec_ref, seg_ref,
                      lwf_ref, lwb_ref, a_ref, k2_ref, kk_ref, v2_ref, g_ref, *, mix):
    w = RW_WIDTH
    k = ps_ref[:, w:2 * w]
    v = ps_ref[:, 2 * w:3 * w]
    lora = ps_ref[:, 3 * w:3 * w + LORA_COLS]
    vec = vec_ref[...]
    for d, out_ref in ((0, lwf_ref), (1, lwb_ref)):
        z = vec[d:d + 1] + _dot(jnp.tanh(lora[:, d * DECAY_LORA:(d + 1) * DECAY_LORA]), wup_ref[d])
        softplus_neg = jnp.maximum(-z, 0.0) + jnp.log(1.0 + jnp.exp(-jnp.abs(z)))
        out_ref[...] = -jnp.exp(-softplus_neg - 0.5)
    off = 2 * DECAY_LORA
    a = jax.nn.sigmoid(vec[2:3] + _dot(lora[:, off:off + AAA_LORA], aup_ref[...]))
    off += AAA_LORA
    g_ref[...] = _dot(jax.nn.sigmoid(lora[:, off:off + GATE_LORA]), gup_ref[...])
    off += GATE_LORA
    if mix:
        v = v + (vf_ref[...] - v) * jax.nn.sigmoid(vec[3:4] + _dot(lora[:, off:LORA_COLS], vup_ref[...]))
    v2_ref[...] = v
    kx = k * vec[4:5]
    sq = kx * kx
    sq_hi = sq.astype(BF16)
    sq_lo = (sq - sq_hi.astype(F32)).astype(BF16)
    norm2 = _dot(sq_hi, seg_ref[...]) + _dot(sq_lo, seg_ref[...])
    kk_ref[...] = kx / jnp.maximum(jnp.sqrt(norm2), 1e-12)
    k2_ref[...] = k * (1.0 + (a - 1.0) * vec[5:6])
    a_ref[...] = a


def _rwkv_prep(ps, v_first, w_up, a_up, g_up, v_up, vec, *, tm=256):
    t = ps.shape[0]
    w = RW_WIDTH
    mix = v_first is not None
    if not mix:
        v_first = ps
        v_up = jnp.zeros((LORA_COLS - 2 * DECAY_LORA - AAA_LORA - GATE_LORA, w), F32)
    head = jnp.arange(w, dtype=jnp.int32) // RW_HEAD
    seg = (head[:, None] == head[None, :]).astype(BF16)
    row = lambda i: (i, 0)
    fixed2 = lambda i: (0, 0)
    full = lambda arr: pl.BlockSpec(arr.shape, (lambda i: (0, 0, 0)) if arr.ndim == 3 else fixed2)
    out = jax.ShapeDtypeStruct((t, w), F32)
    return pl.pallas_call(
        functools.partial(_rwkv_prep_kernel, mix=mix),
        grid=(t // tm,),
        in_specs=[pl.BlockSpec((tm, ps.shape[1]), row), pl.BlockSpec((tm, w), row),
                  full(w_up), full(a_up), full(g_up), full(v_up), full(vec), full(seg)],
        out_specs=[pl.BlockSpec((tm, w), row)] * 7,
        out_shape=[out] * 7,
        compiler_params=pltpu.CompilerParams(
            dimension_semantics=("arbitrary",), vmem_limit_bytes=VMEM_LIMIT),
        name="rwkv_prep",
    )(ps, v_first, w_up, a_up, g_up, v_up, vec, seg)


PAIR = 256


def _moe_kernel(be_ref, xs_ref, w1_ref, b1_ref, w2_ref, b2_ref, o_ref, w1p_ref, w2b_ref):
    i = pl.program_id(0)
    f2 = w1_ref.shape[-1]
    half = PAIR // 2

    @pl.when((i == 0) | (be_ref[i] != be_ref[jnp.maximum(i - 1, 0)]))
    def _():
        src = lax.broadcasted_iota(jnp.int32, (PAIR, PAIR), 0)
        dst = lax.broadcasted_iota(jnp.int32, (PAIR, PAIR), 1)
        perm = jnp.where(src == jnp.where(dst < half, 2 * dst, 2 * (dst - half) + 1), 1.0, 0.0)
        for j in range(0, f2, PAIR):
            w1p_ref[:, j:j + PAIR] = _dot(w1_ref[0, :, j:j + PAIR], perm).astype(BF16)
        w2b_ref[...] = w2_ref[0].astype(BF16)

    h = _dot(xs_ref[...], w1p_ref[...]) + b1_ref[0]
    acts = []
    for j in range(0, f2, PAIR):
        glu = jnp.minimum(h[:, j:j + half], SWIGLU_LIMIT)
        lin = jnp.clip(h[:, j + half:j + PAIR], -SWIGLU_LIMIT, SWIGLU_LIMIT)
        acts.append((glu * jax.nn.sigmoid(SWIGLU_ALPHA * glu) * (lin + 1.0)).astype(BF16))
    o_ref[...] = _dot(jnp.concatenate(acts, axis=1), w2b_ref[...]) + b2_ref[0]


def _moe_experts(block_exp, xs, w1, b1, w2, b2):
    n_slots, d = xs.shape
    f2 = w1.shape[-1]
    f = f2 // 2
    blk = MOE_BLOCK
    n_blocks = n_slots // blk
    b1p = b1.reshape(N_EXPERTS, f2 // PAIR, PAIR // 2, 2).transpose(0, 1, 3, 2).reshape(N_EXPERTS, 1, f2)
    wspec = lambda shape: pl.BlockSpec((1,) + shape, lambda i, be: (be[i], 0, 0))
    return pl.pallas_call(
        _moe_kernel,
        grid_spec=pltpu.PrefetchScalarGridSpec(
            num_scalar_prefetch=1,
            grid=(n_blocks,),
            in_specs=[pl.BlockSpec((blk, d), lambda i, be: (i, 0)),
                      wspec((d, f2)), wspec((1, f2)), wspec((f, d)), wspec((1, d))],
            out_specs=pl.BlockSpec((blk, d), lambda i, be: (i, 0)),
            scratch_shapes=[pltpu.VMEM((d, f2), BF16), pltpu.VMEM((f, d), BF16)]),
        out_shape=jax.ShapeDtypeStruct((n_slots, d), F32),
        compiler_params=pltpu.CompilerParams(
            dimension_semantics=("arbitrary",), vmem_limit_bytes=VMEM_LIMIT),
        name="moe_experts",
    )(block_exp, xs, w1, b1p, w2, b2.reshape(N_EXPERTS, 1, d))


def _ln_rows(y, g, b):
    yc = y - jnp.mean(y, axis=-1, keepdims=True)
    var = jnp.mean(yc * yc, axis=-1, keepdims=True)
    return yc * lax.rsqrt(var + NORM_EPS) * g + b


def _merge_ln_kernel(oa_ref, ob_ref, gate_ref, x_ref, pa_ref, pb_ref, wo_ref, g_ref, b_ref, o_ref):
    d = x_ref.shape[-1]
    merged = (jax.nn.sigmoid(gate_ref[:, :d]) * _dot(oa_ref[...], pa_ref[...])
              + jax.nn.sigmoid(gate_ref[:, d:]) * _dot(ob_ref[...], pb_ref[...]))
    o_ref[...] = _ln_rows(DN_ALPHA * x_ref[...] + _dot(merged, wo_ref[...]), g_ref[...], b_ref[...])


def _merge_ln(o_a, o_b, p_gate, x, proj_a, proj_b, w_out, g, b, *, tm=512):
    t, d = x.shape
    row = lambda i: (i, 0)
    fixed = lambda i: (0, 0)
    wspec = pl.BlockSpec((d, d), fixed)
    return pl.pallas_call(
        _merge_ln_kernel,
        grid=(t // tm,),
        in_specs=[pl.BlockSpec((tm, d), row), pl.BlockSpec((tm, d), row), pl.BlockSpec((tm, 2 * d), row),
                  pl.BlockSpec((tm, d), row), wspec, wspec, wspec,
                  pl.BlockSpec((1, d), fixed), pl.BlockSpec((1, d), fixed)],
        out_specs=pl.BlockSpec((tm, d), row),
        out_shape=jax.ShapeDtypeStruct((t, d), F32),
        compiler_params=pltpu.CompilerParams(
            dimension_semantics=("arbitrary",), vmem_limit_bytes=VMEM_LIMIT),
        name="merge_ln",
    )(o_a, o_b, p_gate, x, proj_a.astype(BF16), proj_b.astype(BF16), w_out.astype(BF16),
      g.reshape(1, d), b.reshape(1, d))


def _combine_ln_kernel(x_ref, yg_ref, gate_ref, g_ref, b_ref, o_ref):
    gate = gate_ref[...]
    moe = gate[:, 0:1] * yg_ref[0]
    for kk in range(1, TOP_K):
        moe = moe + gate[:, kk:kk + 1] * yg_ref[kk]
    o_ref[...] = _ln_rows(DN_ALPHA * x_ref[...] + moe, g_ref[...], b_ref[...])


def _combine_ln(x, yg, gates, g, b, *, tm=512):
    t, d = x.shape
    row = lambda i: (i, 0)
    fixed = lambda i: (0, 0)
    return pl.pallas_call(
        _combine_ln_kernel,
        grid=(t // tm,),
        in_specs=[pl.BlockSpec((tm, d), row), pl.BlockSpec((TOP_K, tm, d), lambda i: (0, i, 0)),
                  pl.BlockSpec((tm, TOP_K), row),
                  pl.BlockSpec((1, d), fixed), pl.BlockSpec((1, d), fixed)],
        out_specs=pl.BlockSpec((tm, d), row),
        out_shape=jax.ShapeDtypeStruct((t, d), F32),
        compiler_params=pltpu.CompilerParams(
            dimension_semantics=("arbitrary",), vmem_limit_bytes=VMEM_LIMIT),
        name="combine_ln",
    )(x, yg, gates, g.reshape(1, d), b.reshape(1, d))


def _rw_heads(t):
    return t.reshape(t.shape[0], RW_HEADS, RW_HEAD)


def _moe(x2, router_w, router_b, w1, b1, w2, b2):
    t, d = x2.shape
    n_assign = t * TOP_K
    n_blocks = -(-n_assign // MOE_BLOCK) + N_EXPERTS
    n_slots = n_blocks * MOE_BLOCK
    npad = 128 - N_EXPERTS
    logits = _matmul_f32(x2, jnp.pad(router_w, ((0, 0), (0, npad))),
                         jnp.pad(router_b, (0, npad)))[:, :N_EXPERTS]
    top_val, top_idx = lax.top_k(logits, TOP_K)
    gates = jax.nn.softmax(top_val, axis=-1)
    e_flat = top_idx.reshape(-1).astype(jnp.int32)
    order = jnp.argsort(e_flat).astype(jnp.int32)
    rank = jnp.argsort(order).astype(jnp.int32)
    experts = jnp.arange(N_EXPERTS, dtype=jnp.int32)
    counts = jnp.bincount(e_flat, length=N_EXPERTS).astype(jnp.int32)
    starts = jnp.cumsum(counts) - counts
    padded = (counts + MOE_BLOCK - 1) // MOE_BLOCK * MOE_BLOCK
    pad_ends = jnp.cumsum(padded)
    pad_starts = pad_ends - padded
    shift = pad_starts - starts
    slot_of = rank + jnp.sum(jnp.where(e_flat[:, None] == experts, shift, 0), axis=1)
    block_start = jnp.arange(n_blocks, dtype=jnp.int32) * MOE_BLOCK
    block_exp = jnp.minimum(jnp.searchsorted(pad_ends, block_start, side='right'),
                            N_EXPERTS - 1).astype(jnp.int32)
    blk_hot = block_exp[:, None] == experts
    per_slot = lambda tab: jnp.repeat(jnp.sum(jnp.where(blk_hot, tab, 0), axis=1), MOE_BLOCK)
    slot = jnp.arange(n_slots, dtype=jnp.int32)
    valid = slot - per_slot(pad_starts) < per_slot(counts)
    src = jnp.clip(slot - per_slot(shift), 0, n_assign - 1)
    slot_tok = jnp.where(valid, order[src] // TOP_K, 0)
    y = _moe_experts(block_exp, x2[slot_tok], w1, b1, w2, b2)
    return y[slot_of.reshape(t, TOP_K).T.reshape(-1)].reshape(TOP_K, t, d), gates


def kernel(x, w_in, hg_lb_logits, hg_norm_w, rw_mu, rw_w0, rw_w_up, rw_a0, rw_a_up, rw_g_up,
           rw_k_k, rw_k_a, rw_r_k, rw_lnx_w, rw_lnx_b, rw_v_down, rw_v_up, rw_v0, proj_a, proj_b,
           w_out, ln1_g, ln1_b, router_w, router_b, moe_w1, moe_b1, moe_w2, moe_b2, ln2_g, ln2_b):
    bsz, s, d = x.shape
    t = bsz * s
    scan = dict(bsz=bsz, seq=s)
    lb_all = jnp.cumsum(jax.nn.softmax(hg_lb_logits.astype(F32), axis=0), axis=0)
    lb_all = lb_all - lb_all[0:1]
    hg_cols = 5 * HG_WIDTH
    rw_cols = 3 * RW_WIDTH + 2 * DECAY_LORA + AAA_LORA + GATE_LORA
    x2 = x.reshape(t, d)
    v_first = None
    for l in range(DEPTH):
        w_l = w_in[l].astype(BF16)
        p_hg = _matmul(x2, w_l[:, :hg_cols], tm=1024, tn=1024)
        p_gate = _matmul(x2, w_l[:, hg_cols + rw_cols:], tm=1024, tn=1024)
        w_rw = w_in[l][:, hg_cols:hg_cols + rw_cols]
        w_self, w_nb = w_rw * (1.0 - rw_mu[l]), w_rw * rw_mu[l]
        if l > 0:
            w_self = jnp.concatenate([w_self, rw_v_down[l - 1]], axis=1)
        pad_cols = lambda w: jnp.pad(w, ((0, 0), (0, 3456 - w.shape[1]))).astype(BF16)
        x3 = jnp.pad(x2.reshape(bsz, s, d), ((0, 0), (1, 1), (0, 0)))
        x_nb = (0.5 * (x3[:, :-2] + x3[:, 2:])).reshape(t, d)
        ps = _matmul2(x2, x_nb, pad_cols(w_self), pad_cols(w_nb))

        o_a = _hgrn_scan(p_hg, lb_all[l], o_fwd=_hgrn_scan(p_hg, lb_all[l], **scan),
                         norm_w=hg_norm_w[l], **scan)

        r = ps[:, :RW_WIDTH]
        zeros = jnp.zeros((RW_WIDTH,), F32)
        if l == 0:
            v_first = ps[:, 2 * RW_WIDTH:3 * RW_WIDTH]
            mixing = dict(v_first=None, v_up=None)
            v0 = zeros
        else:
            v_up = rw_v_up[l - 1]
            mixing = dict(v_first=v_first, v_up=jnp.pad(v_up, ((0, 3456 - rw_cols - v_up.shape[0]), (0, 0))))
            v0 = rw_v0[l - 1]
        vec = jnp.stack([rw_w0[l, 0], rw_w0[l, 1], rw_a0[l], v0, rw_k_k[l], rw_k_a[l], zeros, zeros])
        lw_f, lw_b, a, k, kk, v, g = _rwkv_prep(ps, w_up=rw_w_up[l], a_up=rw_a_up[l], g_up=rw_g_up[l],
                                                vec=vec, **mixing)
        y_f, y_b = _rwkv_scan(ps, k, v, kk, a, lw_f, lw_b, **scan)
        y = _rw_heads(y_f + y_b)
        yc = y - jnp.mean(y, axis=-1, keepdims=True)
        yn = yc * lax.rsqrt(jnp.mean(yc * yc, axis=-1, keepdims=True) + RW_LN_EPS)
        yn = yn * _rw_heads(rw_lnx_w[l][None])[0] + _rw_heads(rw_lnx_b[l][None])[0]
        bonus = jnp.sum(_rw_heads(r) * _rw_heads(k) * rw_r_k[l], axis=-1, keepdims=True) * _rw_heads(v)
        o_b = (yn + bonus).reshape(t, RW_WIDTH) * g

        x2 = _merge_ln(o_a, o_b, p_gate, x2, proj_a[l], proj_b[l], w_out[l], ln1_g[l], ln1_b[l])
        yg, gates = _moe(x2, router_w[l], router_b[l], moe_w1[l], moe_b1[l], moe_w2[l], moe_b2[l])
        x2 = _combine_ln(x2, yg, gates, ln2_g[l], ln2_b[l])
    return x2.reshape(bsz, s, d)
```

```python
import functools

import jax
import jax.numpy as jnp
from jax import lax
from jax.experimental import pallas as pl
from jax.experimental.pallas import tpu as pltpu

F32 = jnp.float32
BF16 = jnp.bfloat16

D_MODEL = 1024
DEPTH = 4
HG_HEADS = 8
HG_D = 128
HG_WIDTH = HG_HEADS * HG_D
RW_HEAD = 64
RW_HEADS = D_MODEL // RW_HEAD
RW_WIDTH = D_MODEL
DECAY_LORA = 64
AAA_LORA = 64
GATE_LORA = 128
N_EXPERTS = 32
TOP_K = 4
MOE_BLOCK = 256
SWIGLU_ALPHA = 1.702
SWIGLU_LIMIT = 7.0
NORM_EPS = 1e-5
RW_LN_EPS = 64e-5
DN_ALPHA = (2 * DEPTH) ** 0.25

CHUNK = 64
SUB = 16
DIAG_LOG_RANGE = 60.0
RW_LANES = 256
RW_GROUPS = 2
VMEM_LIMIT = 56 * 1024 * 1024

_NT = (((1,), (1,)), ((), ()))
_TN = (((0,), (0,)), ((), ()))


def _dot(a, b):
    return jnp.dot(a.astype(BF16), b.astype(BF16), preferred_element_type=F32)


def _dot_nt(a, b):
    return lax.dot_general(a.astype(BF16), b.astype(BF16), _NT, preferred_element_type=F32)


def _dot_tn(a, b):
    return lax.dot_general(a.astype(BF16), b.astype(BF16), _TN, preferred_element_type=F32)


def _dot_f32(a, b):
    return jnp.dot(a, b, preferred_element_type=F32, precision=lax.Precision.HIGHEST)


def _mm_kernel(x_ref, w_ref, o_ref):
    o_ref[...] = _dot(x_ref[...], w_ref[...]).astype(o_ref.dtype)


def _matmul(x, w, *, tm=512, tn=512, out_dtype=F32):
    m, k = x.shape
    n = w.shape[1]
    tm = min(tm, m)
    tn = min(tn, n)
    assert m % tm == 0 and n % tn == 0
    return pl.pallas_call(
        _mm_kernel,
        grid=(n // tn, m // tm),
        in_specs=[pl.BlockSpec((tm, k), lambda j, i: (i, 0)),
                  pl.BlockSpec((k, tn), lambda j, i: (0, j))],
        out_specs=pl.BlockSpec((tm, tn), lambda j, i: (i, j)),
        out_shape=jax.ShapeDtypeStruct((m, n), out_dtype),
        compiler_params=pltpu.CompilerParams(
            dimension_semantics=("arbitrary", "arbitrary"), vmem_limit_bytes=VMEM_LIMIT),
        name="matmul",
    )(x, w)


def _mm2_kernel(x_ref, y_ref, wx_ref, wy_ref, o_ref):
    o_ref[...] = _dot(x_ref[...], wx_ref[...]) + _dot(y_ref[...], wy_ref[...])


def _matmul2(x, y, wx, wy, *, tm=1024, tn=1152):
    m, k = x.shape
    n = wx.shape[1]
    assert m % tm == 0 and n % tn == 0
    xspec = pl.BlockSpec((tm, k), lambda j, i: (i, 0))
    wspec = pl.BlockSpec((k, tn), lambda j, i: (0, j))
    return pl.pallas_call(
        _mm2_kernel,
        grid=(n // tn, m // tm),
        in_specs=[xspec, xspec, wspec, wspec],
        out_specs=pl.BlockSpec((tm, tn), lambda j, i: (i, j)),
        out_shape=jax.ShapeDtypeStruct((m, n), F32),
        compiler_params=pltpu.CompilerParams(
            dimension_semantics=("arbitrary", "arbitrary"), vmem_limit_bytes=VMEM_LIMIT),
        name="matmul2",
    )(x, y, wx, wy)


LANES = 128
NEG_BIG = -3.0e38


def _router_kernel(x_ref, w_ref, b_ref, idx_ref, gate_ref):
    logits = _dot_f32(x_ref[...], w_ref[...]) + b_ref[...]
    lane = lax.broadcasted_iota(jnp.int32, logits.shape, 1).astype(F32)
    cur = logits
    vals, idxs = [], []
    for _ in range(TOP_K):
        top = jnp.max(cur, axis=-1, keepdims=True)
        idx = jnp.min(jnp.where(cur == top, lane, float(LANES)), axis=-1, keepdims=True)
        vals.append(top)
        idxs.append(idx)
        cur = jnp.where(lane == idx, NEG_BIG, cur)
    exps = [jnp.exp(v - vals[0]) for v in vals]
    total = exps[0]
    for e in exps[1:]:
        total = total + e
    idx_out = jnp.zeros_like(logits)
    gate_out = jnp.zeros_like(logits)
    for j in range(TOP_K):
        idx_out = jnp.where(lane == float(j), idxs[j], idx_out)
        gate_out = jnp.where(lane == float(j), exps[j] / total, gate_out)
    idx_ref[...] = idx_out.astype(jnp.int32)
    gate_ref[...] = gate_out


def _router(x, w, b, *, tm=512):
    m, k = x.shape
    pad = LANES - N_EXPERTS
    w = jnp.pad(w, ((0, 0), (0, pad)))
    b = jnp.pad(b, (0, pad), constant_values=NEG_BIG).reshape(1, LANES)
    row = pl.BlockSpec((tm, LANES), lambda i: (i, 0))
    idx, gates = pl.pallas_call(
        _router_kernel,
        grid=(m // tm,),
        in_specs=[pl.BlockSpec((tm, k), lambda i: (i, 0)),
                  pl.BlockSpec((k, LANES), lambda i: (0, 0)),
                  pl.BlockSpec((1, LANES), lambda i: (0, 0))],
        out_specs=[row, row],
        out_shape=[jax.ShapeDtypeStruct((m, LANES), jnp.int32), jax.ShapeDtypeStruct((m, LANES), F32)],
        compiler_params=pltpu.CompilerParams(
            dimension_semantics=("arbitrary",), vmem_limit_bytes=VMEM_LIMIT),
        name="router",
    )(x, w, b)
    return idx[:, :TOP_K], gates[:, :TOP_K]


def _hgrn_kernel(q_ref, z_ref, v_ref, lb_ref, *rest, reverse, n_chunks):
    st_ref = rest[-1]

    @pl.when(pl.program_id(2) == 0)
    def _():
        st_ref[...] = jnp.zeros_like(st_ref)

    lb = lb_ref[...]
    f = lb + (1.0 - lb) * jax.nn.sigmoid(z_ref[...])
    lf = jnp.log(f)
    sub_total = jnp.sum(lf.reshape(lf.shape[0] // SUB, SUB, HG_D), axis=1)
    factorable = jnp.min(sub_total) > -DIAG_LOG_RANGE
    args = (q_ref, f, lf, v_ref) + rest

    @pl.when(factorable)
    def _():
        _hgrn_block(*args, reverse=reverse, n_chunks=n_chunks, pairwise=False)

    @pl.when(jnp.logical_not(factorable))
    def _():
        _hgrn_block(*args, reverse=reverse, n_chunks=n_chunks, pairwise=True)


def _hgrn_block(q_ref, f_blk, lf_blk, v_ref, *rest, reverse, n_chunks, pairwise):
    if reverse:
        of_ref, og_ref, nw_ref, o_ref, st_ref = rest
    else:
        o_ref, st_ref = rest
    c = CHUNK
    row = lax.broadcasted_iota(jnp.int32, (c, c), 0)
    col = lax.broadcasted_iota(jnp.int32, (c, c), 1)
    causal = (col >= row) if reverse else (col <= row)
    tri = jnp.where(causal, 1.0, 0.0).astype(F32)
    diag_mask = causal & (row // SUB == col // SUB)
    srow = lax.broadcasted_iota(jnp.int32, (SUB, 1), 0)

    splits = []
    size = c
    while size > SUB:
        for lo in range(0, c, size):
            splits.append((lo, lo + size // 2, lo + size))
        size //= 2

    order = list(range(n_chunks - 1, -1, -1) if reverse else range(n_chunks))
    chunks = [dict(rows=slice(ci * c, (ci + 1) * c)) for ci in order]
    for ch in chunks:
        rows = ch["rows"]
        ch.update(lf=lf_blk[rows], kk=1.0 - f_blk[rows], q=q_ref[rows, :], v=v_ref[rows, :])
        ch["b"] = _dot_f32(tri, ch["lf"])
    for ch in chunks:
        q, kk, v, b, lf = ch["q"], ch["kk"], ch["v"], ch["b"], ch["lf"]
        b_end = b[0:1, :] if reverse else b[c - 1:c, :]
        ch["qdec"] = (q * jnp.exp(b)).astype(BF16)
        ch["dec"] = jnp.exp(b_end)
        ch["kv"] = _dot_tn(v, kk * jnp.exp(b_end - b))
        scores = []
        for lo, mid, hi in splits:
            if reverse:
                anc = b[mid:mid + 1, :]
                qs, ks = slice(lo, mid), slice(mid, hi)
            else:
                anc = b[mid - 1:mid, :]
                qs, ks = slice(mid, hi), slice(lo, mid)
            scores.append((qs, ks, _dot_nt(q[qs] * jnp.exp(b[qs] - anc), kk[ks] * jnp.exp(anc - b[ks]))))
        ch["scores"] = scores
        if not pairwise:
            edge = b - lf
            anc = jnp.concatenate(
                [jnp.broadcast_to(edge[d0 + SUB - 1:d0 + SUB] if reverse else edge[d0:d0 + 1], (SUB, HG_D))
                 for d0 in range(0, c, SUB)], axis=0)
            ch["diag"] = jnp.where(diag_mask, _dot_nt(q * jnp.exp(b - anc), kk * jnp.exp(anc - b)), 0.0)
    for ch in chunks:
        v = ch["v"]
        parts = [None] * (c // SUB)
        for qs, ks, s in ch["scores"]:
            contrib = _dot(s, v[ks])
            for j in range(qs.start // SUB, qs.stop // SUB):
                piece = contrib[j * SUB - qs.start:(j + 1) * SUB - qs.start]
                parts[j] = piece if parts[j] is None else parts[j] + piece
        intra = jnp.concatenate([jnp.zeros((SUB, HG_D), F32) if p is None else p for p in parts], axis=0)
        if pairwise:
            accs = []
            for d0 in range(0, c, SUB):
                qb, bb, kb, vb = (t[d0:d0 + SUB] for t in (ch["q"], ch["b"], ch["kk"], v))
                acc = jnp.zeros((SUB, HG_D), F32)
                for s in range(SUB):
                    mask = (srow <= s) if reverse else (srow >= s)
                    e = jnp.where(mask, jnp.exp(jnp.minimum(bb - bb[s:s + 1, :], 0.0)), 0.0)
                    w = jnp.sum(qb * kb[s:s + 1, :] * e, axis=-1, keepdims=True)
                    acc = acc + w * vb[s:s + 1, :]
                accs.append(acc)
            ch["intra"] = intra + jnp.concatenate(accs, axis=0)
        else:
            ch["intra"] = intra + _dot(ch["diag"], v)
    st = st_ref[...]
    for ch in chunks:
        ch["st"] = st
        st = st * ch["dec"] + ch["kv"]
    st_ref[...] = st
    for ch in chunks:
        rows = ch["rows"]
        o = ch["intra"] + _dot_nt(ch["qdec"], ch["st"])
        if reverse:
            o = o + of_ref[rows, :]
            o = o * lax.rsqrt(jnp.mean(o * o, axis=-1, keepdims=True) + NORM_EPS) * nw_ref[...]
            o = o * jax.nn.silu(og_ref[rows, :])
        o_ref[rows, :] = o


def _hgrn_scan(p_hg, lb, *, bsz, seq, o_fwd=None, norm_w=None, tb=512):
    reverse = o_fwd is not None
    t = bsz * seq
    tb = min(tb, seq)
    nblk = seq // tb
    h = HG_HEADS
    zoff = 2 * h if reverse else h

    def col(off):
        return pl.BlockSpec((tb, HG_D),
                            lambda b, hh, i: (b * nblk + (nblk - 1 - i if reverse else i), off + hh))

    vec = pl.BlockSpec((1, HG_D), lambda b, hh, i: (0, 0))
    in_specs = [col(0), col(zoff), col(3 * h), pl.BlockSpec((1, HG_D), lambda b, hh, i: (0, hh))]
    args = [p_hg, p_hg, p_hg, lb.reshape(1, -1)]
    if reverse:
        in_specs += [col(0), col(4 * h), vec]
        args += [o_fwd, p_hg, norm_w.reshape(1, -1)]
    return pl.pallas_call(
        functools.partial(_hgrn_kernel, reverse=reverse, n_chunks=tb // CHUNK),
        grid=(bsz, h, nblk),
        in_specs=in_specs,
        out_specs=col(0),
        out_shape=jax.ShapeDtypeStruct((t, h * HG_D), F32),
        scratch_shapes=[pltpu.VMEM((HG_D, HG_D), F32)],
        compiler_params=pltpu.CompilerParams(
            dimension_semantics=("arbitrary", "arbitrary", "arbitrary"),
            vmem_limit_bytes=VMEM_LIMIT),
        name="hgrn_bwd" if reverse else "hgrn_fwd",
    )(*args)


class _RwkvMasks:
    def __init__(self, reverse):
        c, n = CHUNK, RW_LANES
        m = (n // RW_HEAD) * c
        row = lax.broadcasted_iota(jnp.int32, (c, c), 0)
        col = lax.broadcasted_iota(jnp.int32, (c, c), 1)
        self.tri = jnp.where((col >= row) if reverse else (col <= row), 1.0, 0.0).astype(F32)
        self.stack = (lax.broadcasted_iota(jnp.int32, (m, n), 0) // c
                      == lax.broadcasted_iota(jnp.int32, (m, n), 1) // RW_HEAD)
        tr = lax.broadcasted_iota(jnp.int32, (c, m), 0)
        tc = lax.broadcasted_iota(jnp.int32, (c, m), 1) % c
        self.strict = (tc > tr) if reverse else (tc < tr)
        self.incl = (tc >= tr) if reverse else (tc <= tr)
        self.eye = jnp.where(tc == tr, 1.0, 0.0).astype(F32)
        self.diag = (lax.broadcasted_iota(jnp.int32, (n, n), 0) // RW_HEAD
                     == lax.broadcasted_iota(jnp.int32, (n, n), 1) // RW_HEAD)


def _rwkv_stack(t, masks):
    return jnp.where(masks.stack, jnp.concatenate([t] * (RW_LANES // RW_HEAD), axis=0), 0.0).astype(BF16)


def _lockstep(gens):
    gens = list(gens)
    while gens:
        alive = []
        for gen in gens:
            try:
                next(gen)
                alive.append(gen)
            except StopIteration:
                pass
        gens = alive


def _rwkv_prepare(p, refs, rows, lanes, masks, reverse):
    c = CHUNK
    m = (RW_LANES // RW_HEAD) * c
    stack = functools.partial(_rwkv_stack, masks=masks)
    r, k, v, kk, a, lw = (ref[rows, lanes] for ref in refs)
    lp = _dot_f32(masks.tri, lw)
    yield
    lp_end = lp[0:1, :] if reverse else lp[c - 1:c, :]
    pinv = jnp.exp(-lp)
    dec = jnp.exp(lp_end - lp)
    kb = kk * a
    ar = jnp.concatenate([-kk * jnp.exp(lp - lw), r * jnp.exp(lp)], axis=0).astype(BF16)
    bk = jnp.concatenate([stack(kb * pinv), stack(k * pinv)], axis=0)
    v_s = stack(v)
    sc = _dot_nt(ar, bk)
    yield
    ab = jnp.where(masks.strict, sc[:c, :m], 0.0)
    akrk = jnp.concatenate([jnp.where(masks.strict, sc[:c, m:], 0.0),
                            jnp.where(masks.incl, sc[c:, m:], 0.0)], axis=0)
    p.update(ar=ar, v=v.astype(BF16), akrk_v=_dot(akrk, v_s),
             rb=jnp.where(masks.incl, sc[c:, :m], 0.0).astype(BF16),
             pw=_dot(ab, stack(ab)), tinv=masks.eye + ab,
             bkp=jnp.concatenate([kb * dec, k * dec], axis=0).astype(BF16),
             gdec=jnp.exp(lp_end))


def _rwkv_double(p, masks, last):
    c = CHUNK
    pw_s = _rwkv_stack(p["pw"], masks)
    if last:
        p["tinv"] = p["tinv"] + _dot(p["tinv"], pw_s)
    else:
        z = _dot(jnp.concatenate([p["pw"], p["tinv"]], axis=0), pw_s)
        p["pw"], p["tinv"] = z[:c], p["tinv"] + z[c:]


def _rwkv_chain(preps, g_ref, y_ref, rows_seq, lanes, masks):
    c = CHUNK
    for p, rows in zip(preps, rows_seq):
        g = g_ref[...]
        arg = _dot_nt(p["ar"], g)
        yield
        u = _dot(p["tinv"], _rwkv_stack(arg[:c] + p["akrk_v"][:c], masks))
        yield
        y_ref[rows, lanes] = arg[c:] + p["akrk_v"][c:] + _dot(p["rb"], _rwkv_stack(u, masks))
        uv = jnp.concatenate([u.astype(BF16), p["v"]], axis=0)
        g_ref[...] = g * p["gdec"] + jnp.where(masks.diag, _dot_tn(uv, p["bkp"]), 0.0)
        yield


def _rwkv_kernel(*refs, n_chunks):
    in_f, in_b = refs[0:6], refs[6:12]
    y_f, y_b, g_f, g_b = refs[12:16]

    @pl.when(pl.program_id(2) == 0)
    def _():
        g_f[...] = jnp.zeros_like(g_f)
        g_b[...] = jnp.zeros_like(g_b)

    c = CHUNK
    rows = [slice(ci * c, (ci + 1) * c) for ci in range(n_chunks)]
    masks = (_RwkvMasks(False), _RwkvMasks(True))
    plan = []
    for gi in range(RW_GROUPS):
        lanes = slice(gi * RW_LANES, (gi + 1) * RW_LANES)
        plan.append((in_f, y_f, g_f.at[gi], masks[0], False, rows, lanes))
        plan.append((in_b, y_b, g_b.at[gi], masks[1], True, rows[::-1], lanes))
    preps = [[{} for _ in rows] for _ in plan]
    _lockstep(_rwkv_prepare(p, ins, rw, lanes, mk, rev)
              for (ins, _, _, mk, rev, rws, lanes), plist in zip(plan, preps)
              for p, rw in zip(plist, rws))
    n_double = c.bit_length() - 2
    for it in range(n_double):
        for chain, plist in zip(plan, preps):
            for p in plist:
                _rwkv_double(p, chain[3], last=it == n_double - 1)
    _lockstep(_rwkv_chain(plist, g_ref, y_ref, rws, lanes, mk)
              for (_, y_ref, g_ref, mk, _, rws, lanes), plist in zip(plan, preps))


def _rwkv_scan(r, k, v, kk, a, lw_f, lw_b, *, bsz, seq, tb=128):
    t = bsz * seq
    tb = min(tb, seq)
    nblk = seq // tb
    n = RW_LANES * RW_GROUPS
    spec_f = pl.BlockSpec((tb, n), lambda b, hh, i: (b * nblk + i, hh))
    spec_b = pl.BlockSpec((tb, n), lambda b, hh, i: (b * nblk + nblk - 1 - i, hh))
    out = jax.ShapeDtypeStruct((t, RW_WIDTH), F32)
    state = pltpu.VMEM((RW_GROUPS, RW_LANES, RW_LANES), F32)
    return pl.pallas_call(
        functools.partial(_rwkv_kernel, n_chunks=tb // CHUNK),
        grid=(bsz, RW_WIDTH // n, nblk),
        in_specs=[spec_f] * 6 + [spec_b] * 6,
        out_specs=[spec_f, spec_b],
        out_shape=[out, out],
        scratch_shapes=[state, state],
        compiler_params=pltpu.CompilerParams(
            dimension_semantics=("arbitrary", "arbitrary", "arbitrary"),
            vmem_limit_bytes=VMEM_LIMIT),
        name="rwkv",
    )(r, k, v, kk, a, lw_f, r, k, v, kk, a, lw_b)


LORA_COLS = 384


def _rwkv_prep_kernel(ps_ref, vf_ref, wup_ref, aup_ref, gup_ref, vup_ref, vec_ref, seg_ref,
                      lwf_ref, lwb_ref, a_ref, k2_ref, kk_ref, v2_ref, g_ref, *, mix):
    w = RW_WIDTH
    k = ps_ref[:, w:2 * w]
    v = ps_ref[:, 2 * w:3 * w]
    lora = ps_ref[:, 3 * w:3 * w + LORA_COLS]
    vec = vec_ref[...]
    for d, out_ref in ((0, lwf_ref), (1, lwb_ref)):
        z = vec[d:d + 1] + _dot(jnp.tanh(lora[:, d * DECAY_LORA:(d + 1) * DECAY_LORA]), wup_ref[d])
        softplus_neg = jnp.maximum(-z, 0.0) + jnp.log(1.0 + jnp.exp(-jnp.abs(z)))
        out_ref[...] = -jnp.exp(-softplus_neg - 0.5)
    off = 2 * DECAY_LORA
    a = jax.nn.sigmoid(vec[2:3] + _dot(lora[:, off:off + AAA_LORA], aup_ref[...]))
    off += AAA_LORA
    g_ref[...] = _dot(jax.nn.sigmoid(lora[:, off:off + GATE_LORA]), gup_ref[...])
    off += GATE_LORA
    if mix:
        v = v + (vf_ref[...] - v) * jax.nn.sigmoid(vec[3:4] + _dot(lora[:, off:LORA_COLS], vup_ref[...]))
    v2_ref[...] = v
    kx = k * vec[4:5]
    sq = kx * kx
    sq_hi = sq.astype(BF16)
    sq_lo = (sq - sq_hi.astype(F32)).astype(BF16)
    norm2 = _dot(sq_hi, seg_ref[...]) + _dot(sq_lo, seg_ref[...])
    kk_ref[...] = kx / jnp.maximum(jnp.sqrt(norm2), 1e-12)
    k2_ref[...] = k * (1.0 + (a - 1.0) * vec[5:6])
    a_ref[...] = a


def _rwkv_prep(ps, v_first, w_up, a_up, g_up, v_up, vec, *, tm=256):
    t = ps.shape[0]
    w = RW_WIDTH
    mix = v_first is not None
    if not mix:
        v_first = ps
        v_up = jnp.zeros((LORA_COLS - 2 * DECAY_LORA - AAA_LORA - GATE_LORA, w), F32)
    head = jnp.arange(w, dtype=jnp.int32) // RW_HEAD
    seg = (head[:, None] == head[None, :]).astype(BF16)
    row = lambda i: (i, 0)
    fixed2 = lambda i: (0, 0)
    full = lambda arr: pl.BlockSpec(arr.shape, (lambda i: (0, 0, 0)) if arr.ndim == 3 else fixed2)
    out = jax.ShapeDtypeStruct((t, w), F32)
    return pl.pallas_call(
        functools.partial(_rwkv_prep_kernel, mix=mix),
        grid=(t // tm,),
        in_specs=[pl.BlockSpec((tm, ps.shape[1]), row), pl.BlockSpec((tm, w), row),
                  full(w_up), full(a_up), full(g_up), full(v_up), full(vec), full(seg)],
        out_specs=[pl.BlockSpec((tm, w), row)] * 7,
        out_shape=[out] * 7,
        compiler_params=pltpu.CompilerParams(
            dimension_semantics=("arbitrary",), vmem_limit_bytes=VMEM_LIMIT),
        name="rwkv_prep",
    )(ps, v_first, w_up, a_up, g_up, v_up, vec, seg)


PAIR = 256


def _moe_kernel(be_ref, xs_ref, w1_ref, b1_ref, w2_ref, b2_ref, o_ref, w1t_ref, w2t_ref):
    i = pl.program_id(0)
    f2 = w1_ref.shape[-1]
    half = PAIR // 2

    @pl.when((i == 0) | (be_ref[i] != be_ref[jnp.maximum(i - 1, 0)]))
    def _():
        dst = lax.broadcasted_iota(jnp.int32, (PAIR, PAIR), 0)
        src = lax.broadcasted_iota(jnp.int32, (PAIR, PAIR), 1)
        perm_t = jnp.where(src == jnp.where(dst < half, 2 * dst, 2 * (dst - half) + 1), 1.0, 0.0)
        for j in range(0, f2, PAIR):
            w1t_ref[j:j + PAIR, :] = _dot_nt(perm_t, w1_ref[:, j:j + PAIR]).astype(BF16)
        w2t_ref[...] = w2_ref[...].T.astype(BF16)

    h_t = _dot_nt(w1t_ref[...], xs_ref[...]) + b1_ref[...]
    acts = []
    for j in range(0, f2, PAIR):
        glu = jnp.minimum(h_t[j:j + half], SWIGLU_LIMIT)
        lin = jnp.clip(h_t[j + half:j + PAIR], -SWIGLU_LIMIT, SWIGLU_LIMIT)
        acts.append((glu * jax.nn.sigmoid(SWIGLU_ALPHA * glu) * (lin + 1.0)).astype(BF16))
    y_t = _dot(w2t_ref[...], jnp.concatenate(acts, axis=0))
    o_ref[...] = y_t.T + b2_ref[...]


def _moe_experts(block_exp, xs, w1, b1, w2, b2, *, layer):
    n_slots, d = xs.shape
    f2 = w1.shape[-1]
    f = f2 // 2
    blk = MOE_BLOCK
    n_blocks = n_slots // blk
    b1p = b1.reshape(DEPTH, N_EXPERTS, f2 // PAIR, PAIR // 2, 2).swapaxes(-1, -2)
    b1p = b1p.reshape(DEPTH, N_EXPERTS, f2, 1)
    wspec = lambda shape: pl.BlockSpec((None, None) + shape, lambda i, be: (layer, be[i], 0, 0))
    return pl.pallas_call(
        _moe_kernel,
        grid_spec=pltpu.PrefetchScalarGridSpec(
            num_scalar_prefetch=1,
            grid=(n_blocks,),
            in_specs=[pl.BlockSpec((blk, d), lambda i, be: (i, 0)),
                      wspec((d, f2)), wspec((f2, 1)), wspec((f, d)), wspec((1, d))],
            out_specs=pl.BlockSpec((blk, d), lambda i, be: (i, 0)),
            scratch_shapes=[pltpu.VMEM((f2, d), BF16), pltpu.VMEM((d, f), BF16)]),
        out_shape=jax.ShapeDtypeStruct((n_slots, d), F32),
        compiler_params=pltpu.CompilerParams(
            dimension_semantics=("arbitrary",), vmem_limit_bytes=VMEM_LIMIT),
        name="moe_experts",
    )(block_exp, xs, w1, b1p, w2, b2.reshape(DEPTH, N_EXPERTS, 1, d))


def _ln_rows(y, g, b):
    yc = y - jnp.mean(y, axis=-1, keepdims=True)
    var = jnp.mean(yc * yc, axis=-1, keepdims=True)
    return yc * lax.rsqrt(var + NORM_EPS) * g + b


def _merge_ln_kernel(oa_ref, ob_ref, gate_ref, x_ref, pa_ref, pb_ref, wo_ref, g_ref, b_ref, o_ref):
    d = x_ref.shape[-1]
    merged = (jax.nn.sigmoid(gate_ref[:, :d]) * _dot(oa_ref[...], pa_ref[...])
              + jax.nn.sigmoid(gate_ref[:, d:]) * _dot(ob_ref[...], pb_ref[...]))
    o_ref[...] = _ln_rows(DN_ALPHA * x_ref[...] + _dot(merged, wo_ref[...]), g_ref[...], b_ref[...])


def _merge_ln(o_a, o_b, p_gate, x, proj_a, proj_b, w_out, g, b, *, tm=512):
    t, d = x.shape
    row = lambda i: (i, 0)
    fixed = lambda i: (0, 0)
    wspec = pl.BlockSpec((d, d), fixed)
    return pl.pallas_call(
        _merge_ln_kernel,
        grid=(t // tm,),
        in_specs=[pl.BlockSpec((tm, d), row), pl.BlockSpec((tm, d), row), pl.BlockSpec((tm, 2 * d), row),
                  pl.BlockSpec((tm, d), row), wspec, wspec, wspec,
                  pl.BlockSpec((1, d), fixed), pl.BlockSpec((1, d), fixed)],
        out_specs=pl.BlockSpec((tm, d), row),
        out_shape=jax.ShapeDtypeStruct((t, d), F32),
        compiler_params=pltpu.CompilerParams(
            dimension_semantics=("arbitrary",), vmem_limit_bytes=VMEM_LIMIT),
        name="merge_ln",
    )(o_a, o_b, p_gate, x, proj_a.astype(BF16), proj_b.astype(BF16), w_out.astype(BF16),
      g.reshape(1, d), b.reshape(1, d))


def _combine_ln_kernel(x_ref, yg_ref, gate_ref, g_ref, b_ref, o_ref):
    gate = gate_ref[...]
    moe = gate[:, 0:1] * yg_ref[0]
    for kk in range(1, TOP_K):
        moe = moe + gate[:, kk:kk + 1] * yg_ref[kk]
    o_ref[...] = _ln_rows(DN_ALPHA * x_ref[...] + moe, g_ref[...], b_ref[...])


def _combine_ln(x, yg, gates, g, b, *, tm=512):
    t, d = x.shape
    row = lambda i: (i, 0)
    fixed = lambda i: (0, 0)
    return pl.pallas_call(
        _combine_ln_kernel,
        grid=(t // tm,),
        in_specs=[pl.BlockSpec((tm, d), row), pl.BlockSpec((TOP_K, tm, d), lambda i: (0, i, 0)),
                  pl.BlockSpec((tm, TOP_K), row),
                  pl.BlockSpec((1, d), fixed), pl.BlockSpec((1, d), fixed)],
        out_specs=pl.BlockSpec((tm, d), row),
        out_shape=jax.ShapeDtypeStruct((t, d), F32),
        compiler_params=pltpu.CompilerParams(
            dimension_semantics=("arbitrary",), vmem_limit_bytes=VMEM_LIMIT),
        name="combine_ln",
    )(x, yg, gates, g.reshape(1, d), b.reshape(1, d))


def _rw_heads(t):
    return t.reshape(t.shape[0], RW_HEADS, RW_HEAD)


def _moe(x2, router_w, router_b, w1, b1, w2, b2, *, layer):
    t, d = x2.shape
    n_assign = t * TOP_K
    n_blocks = -(-n_assign // MOE_BLOCK) + N_EXPERTS
    n_slots = n_blocks * MOE_BLOCK
    top_idx, gates = _router(x2, router_w, router_b)
    e_flat = top_idx.reshape(-1)
    order = jnp.argsort(e_flat).astype(jnp.int32)
    rank = jnp.argsort(order).astype(jnp.int32)
    experts = jnp.arange(N_EXPERTS, dtype=jnp.int32)
    counts = jnp.bincount(e_flat, length=N_EXPERTS).astype(jnp.int32)
    starts = jnp.cumsum(counts) - counts
    padded = (counts + MOE_BLOCK - 1) // MOE_BLOCK * MOE_BLOCK
    pad_ends = jnp.cumsum(padded)
    pad_starts = pad_ends - padded
    shift = pad_starts - starts
    slot_of = rank + jnp.sum(jnp.where(e_flat[:, None] == experts, shift, 0), axis=1)
    block_start = jnp.arange(n_blocks, dtype=jnp.int32) * MOE_BLOCK
    block_exp = jnp.minimum(jnp.sum(pad_ends[None, :] <= block_start[:, None], axis=1),
                            N_EXPERTS - 1).astype(jnp.int32)
    blk_hot = block_exp[:, None] == experts
    per_slot = lambda tab: jnp.repeat(jnp.sum(jnp.where(blk_hot, tab, 0), axis=1), MOE_BLOCK)
    slot = jnp.arange(n_slots, dtype=jnp.int32)
    valid = slot - per_slot(pad_starts) < per_slot(counts)
    src = jnp.clip(slot - per_slot(shift), 0, n_assign - 1)
    slot_tok = jnp.where(valid, order[src] // TOP_K, 0)
    y = _moe_experts(block_exp, x2[slot_tok], w1, b1, w2, b2, layer=layer)
    return y[slot_of.reshape(t, TOP_K).T.reshape(-1)].reshape(TOP_K, t, d), gates


def kernel(x, w_in, hg_lb_logits, hg_norm_w, rw_mu, rw_w0, rw_w_up, rw_a0, rw_a_up, rw_g_up,
           rw_k_k, rw_k_a, rw_r_k, rw_lnx_w, rw_lnx_b, rw_v_down, rw_v_up, rw_v0, proj_a, proj_b,
           w_out, ln1_g, ln1_b, router_w, router_b, moe_w1, moe_b1, moe_w2, moe_b2, ln2_g, ln2_b):
    bsz, s, d = x.shape
    t = bsz * s
    scan = dict(bsz=bsz, seq=s)
    lb_all = jnp.cumsum(jax.nn.softmax(hg_lb_logits.astype(F32), axis=0), axis=0)
    lb_all = lb_all - lb_all[0:1]
    hg_cols = 5 * HG_WIDTH
    rw_cols = 3 * RW_WIDTH + 2 * DECAY_LORA + AAA_LORA + GATE_LORA
    x2 = x.reshape(t, d)
    v_first = None
    for l in range(DEPTH):
        w_l = w_in[l].astype(BF16)
        p_hg = _matmul(x2, w_l[:, :hg_cols], tm=1024, tn=1024)
        p_gate = _matmul(x2, w_l[:, hg_cols + rw_cols:], tm=1024, tn=1024)
        w_rw = w_in[l][:, hg_cols:hg_cols + rw_cols]
        w_self, w_nb = w_rw * (1.0 - rw_mu[l]), w_rw * rw_mu[l]
        if l > 0:
            w_self = jnp.concatenate([w_self, rw_v_down[l - 1]], axis=1)
        pad_cols = lambda w: jnp.pad(w, ((0, 0), (0, 3456 - w.shape[1]))).astype(BF16)
        x3 = jnp.pad(x2.reshape(bsz, s, d), ((0, 0), (1, 1), (0, 0)))
        x_nb = (0.5 * (x3[:, :-2] + x3[:, 2:])).reshape(t, d)
        ps = _matmul2(x2, x_nb, pad_cols(w_self), pad_cols(w_nb))

        o_a = _hgrn_scan(p_hg, lb_all[l], o_fwd=_hgrn_scan(p_hg, lb_all[l], **scan),
                         norm_w=hg_norm_w[l], **scan)

        r = ps[:, :RW_WIDTH]
        zeros = jnp.zeros((RW_WIDTH,), F32)
        if l == 0:
            v_first = ps[:, 2 * RW_WIDTH:3 * RW_WIDTH]
            mixing = dict(v_first=None, v_up=None)
            v0 = zeros
        else:
            v_up = rw_v_up[l - 1]
            mixing = dict(v_first=v_first, v_up=jnp.pad(v_up, ((0, 3456 - rw_cols - v_up.shape[0]), (0, 0))))
            v0 = rw_v0[l - 1]
        vec = jnp.stack([rw_w0[l, 0], rw_w0[l, 1], rw_a0[l], v0, rw_k_k[l], rw_k_a[l], zeros, zeros])
        lw_f, lw_b, a, k, kk, v, g = _rwkv_prep(ps, w_up=rw_w_up[l], a_up=rw_a_up[l], g_up=rw_g_up[l],
                                                vec=vec, **mixing)
        y_f, y_b = _rwkv_scan(ps, k, v, kk, a, lw_f, lw_b, **scan)
        y = _rw_heads(y_f + y_b)
        yc = y - jnp.mean(y, axis=-1, keepdims=True)
        yn = yc * lax.rsqrt(jnp.mean(yc * yc, axis=-1, keepdims=True) + RW_LN_EPS)
        yn = yn * _rw_heads(rw_lnx_w[l][None])[0] + _rw_heads(rw_lnx_b[l][None])[0]
        bonus = jnp.sum(_rw_heads(r) * _rw_heads(k) * rw_r_k[l], axis=-1, keepdims=True) * _rw_heads(v)
        o_b = (yn + bonus).reshape(t, RW_WIDTH) * g

        x2 = _merge_ln(o_a, o_b, p_gate, x2, proj_a[l], proj_b[l], w_out[l], ln1_g[l], ln1_b[l])
        yg, gates = _moe(x2, router_w[l], router_b[l], moe_w1, moe_b1, moe_w2, moe_b2, layer=l)
        x2 = _combine_ln(x2, yg, gates, ln2_g[l], ln2_b[l])
    return x2.reshape(bsz, s, d)
```

```python
import functools

import jax
import jax.numpy as jnp
from jax import lax
from jax.experimental import pallas as pl
from jax.experimental.pallas import tpu as pltpu

F32 = jnp.float32
BF16 = jnp.bfloat16

D_MODEL = 1024
DEPTH = 4
HG_HEADS = 8
HG_D = 128
HG_WIDTH = HG_HEADS * HG_D
RW_HEAD = 64
RW_HEADS = D_MODEL // RW_HEAD
RW_WIDTH = D_MODEL
DECAY_LORA = 64
AAA_LORA = 64
GATE_LORA = 128
N_EXPERTS = 32
TOP_K = 4
MOE_BLOCK = 512
SWIGLU_ALPHA = 1.702
SWIGLU_LIMIT = 7.0
NORM_EPS = 1e-5
RW_LN_EPS = 64e-5
DN_ALPHA = (2 * DEPTH) ** 0.25

CHUNK = 64
SUB = 16
DIAG_LOG_RANGE = 60.0
RW_LANES = 256
RW_GROUPS = 2
VMEM_LIMIT = 56 * 1024 * 1024

_NT = (((1,), (1,)), ((), ()))
_TN = (((0,), (0,)), ((), ()))


def _dot(a, b):
    return jnp.dot(a.astype(BF16), b.astype(BF16), preferred_element_type=F32)


def _dot_nt(a, b):
    return lax.dot_general(a.astype(BF16), b.astype(BF16), _NT, preferred_element_type=F32)


def _dot_tn(a, b):
    return lax.dot_general(a.astype(BF16), b.astype(BF16), _TN, preferred_element_type=F32)


def _dot_f32(a, b):
    return jnp.dot(a, b, preferred_element_type=F32, precision=lax.Precision.HIGHEST)


def _mm_kernel(x_ref, w_ref, o_ref):
    o_ref[...] = _dot(x_ref[...], w_ref[...]).astype(o_ref.dtype)


def _matmul(x, w, *, tm=512, tn=512, out_dtype=F32):
    m, k = x.shape
    n = w.shape[1]
    tm = min(tm, m)
    tn = min(tn, n)
    assert m % tm == 0 and n % tn == 0
    return pl.pallas_call(
        _mm_kernel,
        grid=(n // tn, m // tm),
        in_specs=[pl.BlockSpec((tm, k), lambda j, i: (i, 0)),
                  pl.BlockSpec((k, tn), lambda j, i: (0, j))],
        out_specs=pl.BlockSpec((tm, tn), lambda j, i: (i, j)),
        out_shape=jax.ShapeDtypeStruct((m, n), out_dtype),
        compiler_params=pltpu.CompilerParams(
            dimension_semantics=("arbitrary", "arbitrary"), vmem_limit_bytes=VMEM_LIMIT),
        name="matmul",
    )(x, w)


def _mm2_kernel(x_ref, y_ref, wx_ref, wy_ref, o_ref):
    o_ref[...] = _dot(x_ref[...], wx_ref[...]) + _dot(y_ref[...], wy_ref[...])


def _matmul2(x, y, wx, wy, *, tm=1024, tn=1152):
    m, k = x.shape
    n = wx.shape[1]
    assert m % tm == 0 and n % tn == 0
    xspec = pl.BlockSpec((tm, k), lambda j, i: (i, 0))
    wspec = pl.BlockSpec((k, tn), lambda j, i: (0, j))
    return pl.pallas_call(
        _mm2_kernel,
        grid=(n // tn, m // tm),
        in_specs=[xspec, xspec, wspec, wspec],
        out_specs=pl.BlockSpec((tm, tn), lambda j, i: (i, j)),
        out_shape=jax.ShapeDtypeStruct((m, n), F32),
        compiler_params=pltpu.CompilerParams(
            dimension_semantics=("arbitrary", "arbitrary"), vmem_limit_bytes=VMEM_LIMIT),
        name="matmul2",
    )(x, y, wx, wy)


LANES = 128
NEG_BIG = -3.0e38


def _router_kernel(x_ref, w_ref, b_ref, idx_ref, gate_ref):
    logits = _dot_f32(x_ref[...], w_ref[...]) + b_ref[...]
    lane = lax.broadcasted_iota(jnp.int32, logits.shape, 1).astype(F32)
    cur = logits
    vals, idxs = [], []
    for _ in range(TOP_K):
        top = jnp.max(cur, axis=-1, keepdims=True)
        idx = jnp.min(jnp.where(cur == top, lane, float(LANES)), axis=-1, keepdims=True)
        vals.append(top)
        idxs.append(idx)
        cur = jnp.where(lane == idx, NEG_BIG, cur)
    exps = [jnp.exp(v - vals[0]) for v in vals]
    total = exps[0]
    for e in exps[1:]:
        total = total + e
    idx_out = jnp.zeros_like(logits)
    gate_out = jnp.zeros_like(logits)
    for j in range(TOP_K):
        idx_out = jnp.where(lane == float(j), idxs[j], idx_out)
        gate_out = jnp.where(lane == float(j), exps[j] / total, gate_out)
    idx_ref[...] = idx_out.astype(jnp.int32)
    gate_ref[...] = gate_out


def _router(x, w, b, *, tm=512):
    m, k = x.shape
    pad = LANES - N_EXPERTS
    w = jnp.pad(w, ((0, 0), (0, pad)))
    b = jnp.pad(b, (0, pad), constant_values=NEG_BIG).reshape(1, LANES)
    row = pl.BlockSpec((tm, LANES), lambda i: (i, 0))
    idx, gates = pl.pallas_call(
        _router_kernel,
        grid=(m // tm,),
        in_specs=[pl.BlockSpec((tm, k), lambda i: (i, 0)),
                  pl.BlockSpec((k, LANES), lambda i: (0, 0)),
                  pl.BlockSpec((1, LANES), lambda i: (0, 0))],
        out_specs=[row, row],
        out_shape=[jax.ShapeDtypeStruct((m, LANES), jnp.int32), jax.ShapeDtypeStruct((m, LANES), F32)],
        compiler_params=pltpu.CompilerParams(
            dimension_semantics=("arbitrary",), vmem_limit_bytes=VMEM_LIMIT),
        name="router",
    )(x, w, b)
    return idx[:, :TOP_K], gates[:, :TOP_K]


def _hgrn_kernel(q_ref, z_ref, v_ref, lb_ref, *rest, reverse, n_chunks):
    st_ref = rest[-1]

    @pl.when(pl.program_id(2) == 0)
    def _():
        st_ref[...] = jnp.zeros_like(st_ref)

    lb = lb_ref[...]
    f = lb + (1.0 - lb) * jax.nn.sigmoid(z_ref[...])
    lf = jnp.log(f)
    sub_total = jnp.sum(lf.reshape(lf.shape[0] // SUB, SUB, HG_D), axis=1)
    factorable = jnp.min(sub_total) > -DIAG_LOG_RANGE
    args = (q_ref, f, lf, v_ref) + rest

    @pl.when(factorable)
    def _():
        _hgrn_block(*args, reverse=reverse, n_chunks=n_chunks, pairwise=False)

    @pl.when(jnp.logical_not(factorable))
    def _():
        _hgrn_block(*args, reverse=reverse, n_chunks=n_chunks, pairwise=True)


def _hgrn_block(q_ref, f_blk, lf_blk, v_ref, *rest, reverse, n_chunks, pairwise):
    if reverse:
        of_ref, og_ref, nw_ref, o_ref, st_ref = rest
    else:
        o_ref, st_ref = rest
    c = CHUNK
    row = lax.broadcasted_iota(jnp.int32, (c, c), 0)
    col = lax.broadcasted_iota(jnp.int32, (c, c), 1)
    causal = (col >= row) if reverse else (col <= row)
    tri = jnp.where(causal, 1.0, 0.0).astype(F32)
    diag_mask = causal & (row // SUB == col // SUB)
    srow = lax.broadcasted_iota(jnp.int32, (SUB, 1), 0)

    splits = []
    size = c
    while size > SUB:
        for lo in range(0, c, size):
            splits.append((lo, lo + size // 2, lo + size))
        size //= 2

    order = list(range(n_chunks - 1, -1, -1) if reverse else range(n_chunks))
    chunks = [dict(rows=slice(ci * c, (ci + 1) * c)) for ci in order]
    for ch in chunks:
        rows = ch["rows"]
        ch.update(lf=lf_blk[rows], kk=1.0 - f_blk[rows], q=q_ref[rows, :], v=v_ref[rows, :])
        ch["b"] = _dot_f32(tri, ch["lf"])
    for ch in chunks:
        q, kk, v, b, lf = ch["q"], ch["kk"], ch["v"], ch["b"], ch["lf"]
        b_end = b[0:1, :] if reverse else b[c - 1:c, :]
        ch["qdec"] = (q * jnp.exp(b)).astype(BF16)
        ch["dec"] = jnp.exp(b_end)
        ch["kv"] = _dot_tn(v, kk * jnp.exp(b_end - b))
        scores = []
        for lo, mid, hi in splits:
            if reverse:
                anc = b[mid:mid + 1, :]
                qs, ks = slice(lo, mid), slice(mid, hi)
            else:
                anc = b[mid - 1:mid, :]
                qs, ks = slice(mid, hi), slice(lo, mid)
            scores.append((qs, ks, _dot_nt(q[qs] * jnp.exp(b[qs] - anc), kk[ks] * jnp.exp(anc - b[ks]))))
        ch["scores"] = scores
        if not pairwise:
            edge = b - lf
            anc = jnp.concatenate(
                [jnp.broadcast_to(edge[d0 + SUB - 1:d0 + SUB] if reverse else edge[d0:d0 + 1], (SUB, HG_D))
                 for d0 in range(0, c, SUB)], axis=0)
            ch["diag"] = jnp.where(diag_mask, _dot_nt(q * jnp.exp(b - anc), kk * jnp.exp(anc - b)), 0.0)
    for ch in chunks:
        v = ch["v"]
        parts = [None] * (c // SUB)
        for qs, ks, s in ch["scores"]:
            contrib = _dot(s, v[ks])
            for j in range(qs.start // SUB, qs.stop // SUB):
                piece = contrib[j * SUB - qs.start:(j + 1) * SUB - qs.start]
                parts[j] = piece if parts[j] is None else parts[j] + piece
        intra = jnp.concatenate([jnp.zeros((SUB, HG_D), F32) if p is None else p for p in parts], axis=0)
        if pairwise:
            accs = []
            for d0 in range(0, c, SUB):
                qb, bb, kb, vb = (t[d0:d0 + SUB] for t in (ch["q"], ch["b"], ch["kk"], v))
                acc = jnp.zeros((SUB, HG_D), F32)
                for s in range(SUB):
                    mask = (srow <= s) if reverse else (srow >= s)
                    e = jnp.where(mask, jnp.exp(jnp.minimum(bb - bb[s:s + 1, :], 0.0)), 0.0)
                    w = jnp.sum(qb * kb[s:s + 1, :] * e, axis=-1, keepdims=True)
                    acc = acc + w * vb[s:s + 1, :]
                accs.append(acc)
            ch["intra"] = intra + jnp.concatenate(accs, axis=0)
        else:
            ch["intra"] = intra + _dot(ch["diag"], v)
    st = st_ref[...]
    for ch in chunks:
        ch["st"] = st
        st = st * ch["dec"] + ch["kv"]
    st_ref[...] = st
    for ch in chunks:
        rows = ch["rows"]
        o = ch["intra"] + _dot_nt(ch["qdec"], ch["st"])
        if reverse:
            o = o + of_ref[rows, :]
            o = o * lax.rsqrt(jnp.mean(o * o, axis=-1, keepdims=True) + NORM_EPS) * nw_ref[...]
            o = o * jax.nn.silu(og_ref[rows, :])
        o_ref[rows, :] = o


def _hgrn_scan(p_hg, lb, *, bsz, seq, o_fwd=None, norm_w=None, tb=512):
    reverse = o_fwd is not None
    t = bsz * seq
    tb = min(tb, seq)
    nblk = seq // tb
    h = HG_HEADS
    zoff = 2 * h if reverse else h

    def col(off):
        return pl.BlockSpec((tb, HG_D),
                            lambda b, hh, i: (b * nblk + (nblk - 1 - i if reverse else i), off + hh))

    vec = pl.BlockSpec((1, HG_D), lambda b, hh, i: (0, 0))
    in_specs = [col(0), col(zoff), col(3 * h), pl.BlockSpec((1, HG_D), lambda b, hh, i: (0, hh))]
    args = [p_hg, p_hg, p_hg, lb.reshape(1, -1)]
    if reverse:
        in_specs += [col(0), col(4 * h), vec]
        args += [o_fwd, p_hg, norm_w.reshape(1, -1)]
    return pl.pallas_call(
        functools.partial(_hgrn_kernel, reverse=reverse, n_chunks=tb // CHUNK),
        grid=(bsz, h, nblk),
        in_specs=in_specs,
        out_specs=col(0),
        out_shape=jax.ShapeDtypeStruct((t, h * HG_D), F32),
        scratch_shapes=[pltpu.VMEM((HG_D, HG_D), F32)],
        compiler_params=pltpu.CompilerParams(
            dimension_semantics=("arbitrary", "arbitrary", "arbitrary"),
            vmem_limit_bytes=VMEM_LIMIT),
        name="hgrn_bwd" if reverse else "hgrn_fwd",
    )(*args)


class _RwkvMasks:
    def __init__(self, reverse):
        c, n = CHUNK, RW_LANES
        m = (n // RW_HEAD) * c
        row = lax.broadcasted_iota(jnp.int32, (c, c), 0)
        col = lax.broadcasted_iota(jnp.int32, (c, c), 1)
        self.tri = jnp.where((col >= row) if reverse else (col <= row), 1.0, 0.0).astype(F32)
        self.stack = (lax.broadcasted_iota(jnp.int32, (m, n), 0) // c
                      == lax.broadcasted_iota(jnp.int32, (m, n), 1) // RW_HEAD)
        tr = lax.broadcasted_iota(jnp.int32, (c, m), 0)
        tc = lax.broadcasted_iota(jnp.int32, (c, m), 1) % c
        self.strict = (tc > tr) if reverse else (tc < tr)
        self.incl = (tc >= tr) if reverse else (tc <= tr)
        self.eye = jnp.where(tc == tr, 1.0, 0.0).astype(F32)
        self.diag = (lax.broadcasted_iota(jnp.int32, (n, n), 0) // RW_HEAD
                     == lax.broadcasted_iota(jnp.int32, (n, n), 1) // RW_HEAD)


def _rwkv_stack(t, masks):
    return jnp.where(masks.stack, jnp.concatenate([t] * (RW_LANES // RW_HEAD), axis=0), 0.0).astype(BF16)


def _lockstep(gens):
    gens = list(gens)
    while gens:
        alive = []
        for gen in gens:
            try:
                next(gen)
                alive.append(gen)
            except StopIteration:
                pass
        gens = alive


def _rwkv_prepare(p, refs, rows, lanes, masks, reverse):
    c = CHUNK
    m = (RW_LANES // RW_HEAD) * c
    stack = functools.partial(_rwkv_stack, masks=masks)
    r, k, v, kk, a, lw = (ref[rows, lanes] for ref in refs)
    lp = _dot_f32(masks.tri, lw)
    yield
    lp_end = lp[0:1, :] if reverse else lp[c - 1:c, :]
    pinv = jnp.exp(-lp)
    dec = jnp.exp(lp_end - lp)
    kb = kk * a
    ar = jnp.concatenate([-kk * jnp.exp(lp - lw), r * jnp.exp(lp)], axis=0).astype(BF16)
    bk = jnp.concatenate([stack(kb * pinv), stack(k * pinv)], axis=0)
    v_s = stack(v)
    sc = _dot_nt(ar, bk)
    yield
    ab = jnp.where(masks.strict, sc[:c, :m], 0.0)
    akrk = jnp.concatenate([jnp.where(masks.strict, sc[:c, m:], 0.0),
                            jnp.where(masks.incl, sc[c:, m:], 0.0)], axis=0)
    p.update(ar=ar, v=v.astype(BF16), akrk_v=_dot(akrk, v_s),
             rb=jnp.where(masks.incl, sc[c:, :m], 0.0).astype(BF16),
             pw=_dot(ab, stack(ab)), tinv=masks.eye + ab,
             bkp=jnp.concatenate([kb * dec, k * dec], axis=0).astype(BF16),
             gdec=jnp.exp(lp_end))


def _rwkv_double(p, masks, last):
    c = CHUNK
    pw_s = _rwkv_stack(p["pw"], masks)
    if last:
        p["tinv"] = p["tinv"] + _dot(p["tinv"], pw_s)
    else:
        z = _dot(jnp.concatenate([p["pw"], p["tinv"]], axis=0), pw_s)
        p["pw"], p["tinv"] = z[:c], p["tinv"] + z[c:]


def _rwkv_chain(preps, g_ref, y_ref, rows_seq, lanes, masks):
    c = CHUNK
    for p, rows in zip(preps, rows_seq):
        g = g_ref[...]
        arg = _dot_nt(p["ar"], g)
        yield
        u = _dot(p["tinv"], _rwkv_stack(arg[:c] + p["akrk_v"][:c], masks))
        yield
        y_ref[rows, lanes] = arg[c:] + p["akrk_v"][c:] + _dot(p["rb"], _rwkv_stack(u, masks))
        uv = jnp.concatenate([u.astype(BF16), p["v"]], axis=0)
        g_ref[...] = g * p["gdec"] + jnp.where(masks.diag, _dot_tn(uv, p["bkp"]), 0.0)
        yield


def _rwkv_kernel(*refs, n_chunks):
    in_f, in_b = refs[0:6], refs[6:12]
    y_f, y_b, g_f, g_b = refs[12:16]

    @pl.when(pl.program_id(2) == 0)
    def _():
        g_f[...] = jnp.zeros_like(g_f)
        g_b[...] = jnp.zeros_like(g_b)

    c = CHUNK
    rows = [slice(ci * c, (ci + 1) * c) for ci in range(n_chunks)]
    masks = (_RwkvMasks(False), _RwkvMasks(True))
    plan = []
    for gi in range(RW_GROUPS):
        lanes = slice(gi * RW_LANES, (gi + 1) * RW_LANES)
        plan.append((in_f, y_f, g_f.at[gi], masks[0], False, rows, lanes))
        plan.append((in_b, y_b, g_b.at[gi], masks[1], True, rows[::-1], lanes))
    preps = [[{} for _ in rows] for _ in plan]
    _lockstep(_rwkv_prepare(p, ins, rw, lanes, mk, rev)
              for (ins, _, _, mk, rev, rws, lanes), plist in zip(plan, preps)
              for p, rw in zip(plist, rws))
    n_double = c.bit_length() - 2
    for it in range(n_double):
        for chain, plist in zip(plan, preps):
            for p in plist:
                _rwkv_double(p, chain[3], last=it == n_double - 1)
    _lockstep(_rwkv_chain(plist, g_ref, y_ref, rws, lanes, mk)
              for (_, y_ref, g_ref, mk, _, rws, lanes), plist in zip(plan, preps))


def _rwkv_scan(r, k, v, kk, a, lw_f, lw_b, *, bsz, seq, tb=128):
    t = bsz * seq
    tb = min(tb, seq)
    nblk = seq // tb
    n = RW_LANES * RW_GROUPS
    spec_f = pl.BlockSpec((tb, n), lambda b, hh, i: (b * nblk + i, hh))
    spec_b = pl.BlockSpec((tb, n), lambda b, hh, i: (b * nblk + nblk - 1 - i, hh))
    out = jax.ShapeDtypeStruct((t, RW_WIDTH), F32)
    state = pltpu.VMEM((RW_GROUPS, RW_LANES, RW_LANES), F32)
    return pl.pallas_call(
        functools.partial(_rwkv_kernel, n_chunks=tb // CHUNK),
        grid=(bsz, RW_WIDTH // n, nblk),
        in_specs=[spec_f] * 6 + [spec_b] * 6,
        out_specs=[spec_f, spec_b],
        out_shape=[out, out],
        scratch_shapes=[state, state],
        compiler_params=pltpu.CompilerParams(
            dimension_semantics=("arbitrary", "arbitrary", "arbitrary"),
            vmem_limit_bytes=VMEM_LIMIT),
        name="rwkv",
    )(r, k, v, kk, a, lw_f, r, k, v, kk, a, lw_b)


LORA_COLS = 384


def _rwkv_prep_kernel(ps_ref, vf_ref, wup_ref, aup_ref, gup_ref, vup_ref, vec_ref, seg_ref,
                      lwf_ref, lwb_ref, a_ref, k2_ref, kk_ref, v2_ref, g_ref, *, mix):
    w = RW_WIDTH
    k = ps_ref[:, w:2 * w]
    v = ps_ref[:, 2 * w:3 * w]
    lora = ps_ref[:, 3 * w:3 * w + LORA_COLS]
    vec = vec_ref[...]
    for d, out_ref in ((0, lwf_ref), (1, lwb_ref)):
        z = vec[d:d + 1] + _dot(jnp.tanh(lora[:, d * DECAY_LORA:(d + 1) * DECAY_LORA]), wup_ref[d])
        softplus_neg = jnp.maximum(-z, 0.0) + jnp.log(1.0 + jnp.exp(-jnp.abs(z)))
        out_ref[...] = -jnp.exp(-softplus_neg - 0.5)
    off = 2 * DECAY_LORA
    a = jax.nn.sigmoid(vec[2:3] + _dot(lora[:, off:off + AAA_LORA], aup_ref[...]))
    off += AAA_LORA
    g_ref[...] = _dot(jax.nn.sigmoid(lora[:, off:off + GATE_LORA]), gup_ref[...])
    off += GATE_LORA
    if mix:
        v = v + (vf_ref[...] - v) * jax.nn.sigmoid(vec[3:4] + _dot(lora[:, off:LORA_COLS], vup_ref[...]))
    v2_ref[...] = v
    kx = k * vec[4:5]
    sq = kx * kx
    sq_hi = sq.astype(BF16)
    sq_lo = (sq - sq_hi.astype(F32)).astype(BF16)
    norm2 = _dot(sq_hi, seg_ref[...]) + _dot(sq_lo, seg_ref[...])
    kk_ref[...] = kx / jnp.maximum(jnp.sqrt(norm2), 1e-12)
    k2_ref[...] = k * (1.0 + (a - 1.0) * vec[5:6])
    a_ref[...] = a


def _rwkv_prep(ps, v_first, w_up, a_up, g_up, v_up, vec, *, tm=256):
    t = ps.shape[0]
    w = RW_WIDTH
    mix = v_first is not None
    if not mix:
        v_first = ps
        v_up = jnp.zeros((LORA_COLS - 2 * DECAY_LORA - AAA_LORA - GATE_LORA, w), F32)
    head = jnp.arange(w, dtype=jnp.int32) // RW_HEAD
    seg = (head[:, None] == head[None, :]).astype(BF16)
    row = lambda i: (i, 0)
    fixed2 = lambda i: (0, 0)
    full = lambda arr: pl.BlockSpec(arr.shape, (lambda i: (0, 0, 0)) if arr.ndim == 3 else fixed2)
    out = jax.ShapeDtypeStruct((t, w), F32)
    return pl.pallas_call(
        functools.partial(_rwkv_prep_kernel, mix=mix),
        grid=(t // tm,),
        in_specs=[pl.BlockSpec((tm, ps.shape[1]), row), pl.BlockSpec((tm, w), row),
                  full(w_up), full(a_up), full(g_up), full(v_up), full(vec), full(seg)],
        out_specs=[pl.BlockSpec((tm, w), row)] * 7,
        out_shape=[out] * 7,
        compiler_params=pltpu.CompilerParams(
            dimension_semantics=("arbitrary",), vmem_limit_bytes=VMEM_LIMIT),
        name="rwkv_prep",
    )(ps, v_first, w_up, a_up, g_up, v_up, vec, seg)


PAIR = 256


def _moe_kernel(be_ref, xs_ref, w1_ref, b1_ref, w2_ref, b2_ref, o_ref, w1p_ref, w2b_ref):
    i = pl.program_id(0)
    f2 = w1_ref.shape[-1]
    half = PAIR // 2

    @pl.when((i == 0) | (be_ref[i] != be_ref[jnp.maximum(i - 1, 0)]))
    def _():
        src = lax.broadcasted_iota(jnp.int32, (PAIR, PAIR), 0)
        dst = lax.broadcasted_iota(jnp.int32, (PAIR, PAIR), 1)
        perm = jnp.where(src == jnp.where(dst < half, 2 * dst, 2 * (dst - half) + 1), 1.0, 0.0)
        for j in range(0, f2, PAIR):
            w1p_ref[:, j:j + PAIR] = _dot(w1_ref[:, j:j + PAIR], perm).astype(BF16)
        w2b_ref[...] = w2_ref[...].astype(BF16)

    h = _dot(xs_ref[...], w1p_ref[...]) + b1_ref[...]
    acts = []
    for j in range(0, f2, PAIR):
        glu = jnp.minimum(h[:, j:j + half], SWIGLU_LIMIT)
        lin = jnp.clip(h[:, j + half:j + PAIR], -SWIGLU_LIMIT, SWIGLU_LIMIT)
        acts.append((glu * jax.nn.sigmoid(SWIGLU_ALPHA * glu) * (lin + 1.0)).astype(BF16))
    o_ref[...] = _dot(jnp.concatenate(acts, axis=1), w2b_ref[...]) + b2_ref[...]


def _moe_experts(block_exp, xs, w1, b1, w2, b2, *, layer):
    n_slots, d = xs.shape
    f2 = w1.shape[-1]
    f = f2 // 2
    blk = MOE_BLOCK
    n_blocks = n_slots // blk
    b1p = b1.reshape(DEPTH, N_EXPERTS, f2 // PAIR, PAIR // 2, 2).swapaxes(-1, -2)
    b1p = b1p.reshape(DEPTH, N_EXPERTS, 1, f2)
    wspec = lambda shape: pl.BlockSpec((None, None) + shape, lambda i, be: (layer, be[i], 0, 0))
    return pl.pallas_call(
        _moe_kernel,
        grid_spec=pltpu.PrefetchScalarGridSpec(
            num_scalar_prefetch=1,
            grid=(n_blocks,),
            in_specs=[pl.BlockSpec((blk, d), lambda i, be: (i, 0)),
                      wspec((d, f2)), wspec((1, f2)), wspec((f, d)), wspec((1, d))],
            out_specs=pl.BlockSpec((blk, d), lambda i, be: (i, 0)),
            scratch_shapes=[pltpu.VMEM((d, f2), BF16), pltpu.VMEM((f, d), BF16)]),
        out_shape=jax.ShapeDtypeStruct((n_slots, d), F32),
        compiler_params=pltpu.CompilerParams(
            dimension_semantics=("arbitrary",), vmem_limit_bytes=VMEM_LIMIT),
        name="moe_experts",
    )(block_exp, xs, w1, b1p, w2, b2.reshape(DEPTH, N_EXPERTS, 1, d))


def _ln_rows(y, g, b):
    yc = y - jnp.mean(y, axis=-1, keepdims=True)
    var = jnp.mean(yc * yc, axis=-1, keepdims=True)
    return yc * lax.rsqrt(var + NORM_EPS) * g + b


def _merge_ln_kernel(oa_ref, ob_ref, gate_ref, x_ref, pa_ref, pb_ref, wo_ref, g_ref, b_ref, o_ref):
    d = x_ref.shape[-1]
    merged = (jax.nn.sigmoid(gate_ref[:, :d]) * _dot(oa_ref[...], pa_ref[...])
              + jax.nn.sigmoid(gate_ref[:, d:]) * _dot(ob_ref[...], pb_ref[...]))
    o_ref[...] = _ln_rows(DN_ALPHA * x_ref[...] + _dot(merged, wo_ref[...]), g_ref[...], b_ref[...])


def _merge_ln(o_a, o_b, p_gate, x, proj_a, proj_b, w_out, g, b, *, tm=512):
    t, d = x.shape
    row = lambda i: (i, 0)
    fixed = lambda i: (0, 0)
    wspec = pl.BlockSpec((d, d), fixed)
    return pl.pallas_call(
        _merge_ln_kernel,
        grid=(t // tm,),
        in_specs=[pl.BlockSpec((tm, d), row), pl.BlockSpec((tm, d), row), pl.BlockSpec((tm, 2 * d), row),
                  pl.BlockSpec((tm, d), row), wspec, wspec, wspec,
                  pl.BlockSpec((1, d), fixed), pl.BlockSpec((1, d), fixed)],
        out_specs=pl.BlockSpec((tm, d), row),
        out_shape=jax.ShapeDtypeStruct((t, d), F32),
        compiler_params=pltpu.CompilerParams(
            dimension_semantics=("arbitrary",), vmem_limit_bytes=VMEM_LIMIT),
        name="merge_ln",
    )(o_a, o_b, p_gate, x, proj_a.astype(BF16), proj_b.astype(BF16), w_out.astype(BF16),
      g.reshape(1, d), b.reshape(1, d))


def _combine_ln_kernel(x_ref, yg_ref, gate_ref, g_ref, b_ref, o_ref):
    gate = gate_ref[...]
    moe = gate[:, 0:1] * yg_ref[0]
    for kk in range(1, TOP_K):
        moe = moe + gate[:, kk:kk + 1] * yg_ref[kk]
    o_ref[...] = _ln_rows(DN_ALPHA * x_ref[...] + moe, g_ref[...], b_ref[...])


def _combine_ln(x, yg, gates, g, b, *, tm=512):
    t, d = x.shape
    row = lambda i: (i, 0)
    fixed = lambda i: (0, 0)
    return pl.pallas_call(
        _combine_ln_kernel,
        grid=(t // tm,),
        in_specs=[pl.BlockSpec((tm, d), row), pl.BlockSpec((TOP_K, tm, d), lambda i: (0, i, 0)),
                  pl.BlockSpec((tm, TOP_K), row),
                  pl.BlockSpec((1, d), fixed), pl.BlockSpec((1, d), fixed)],
        out_specs=pl.BlockSpec((tm, d), row),
        out_shape=jax.ShapeDtypeStruct((t, d), F32),
        compiler_params=pltpu.CompilerParams(
            dimension_semantics=("arbitrary",), vmem_limit_bytes=VMEM_LIMIT),
        name="combine_ln",
    )(x, yg, gates, g.reshape(1, d), b.reshape(1, d))


def _rw_heads(t):
    return t.reshape(t.shape[0], RW_HEADS, RW_HEAD)


def _moe(x2, router_w, router_b, w1, b1, w2, b2, *, layer):
    t, d = x2.shape
    n_assign = t * TOP_K
    n_blocks = -(-n_assign // MOE_BLOCK) + N_EXPERTS
    n_slots = n_blocks * MOE_BLOCK
    top_idx, gates = _router(x2, router_w, router_b)
    e_flat = top_idx.reshape(-1)
    order = jnp.argsort(e_flat).astype(jnp.int32)
    rank = jnp.argsort(order).astype(jnp.int32)
    experts = jnp.arange(N_EXPERTS, dtype=jnp.int32)
    counts = jnp.bincount(e_flat, length=N_EXPERTS).astype(jnp.int32)
    starts = jnp.cumsum(counts) - counts
    padded = (counts + MOE_BLOCK - 1) // MOE_BLOCK * MOE_BLOCK
    pad_ends = jnp.cumsum(padded)
    pad_starts = pad_ends - padded
    shift = pad_starts - starts
    slot_of = rank + jnp.sum(jnp.where(e_flat[:, None] == experts, shift, 0), axis=1)
    block_start = jnp.arange(n_blocks, dtype=jnp.int32) * MOE_BLOCK
    block_exp = jnp.minimum(jnp.sum(pad_ends[None, :] <= block_start[:, None], axis=1),
                            N_EXPERTS - 1).astype(jnp.int32)
    blk_hot = block_exp[:, None] == experts
    per_slot = lambda tab: jnp.repeat(jnp.sum(jnp.where(blk_hot, tab, 0), axis=1), MOE_BLOCK)
    slot = jnp.arange(n_slots, dtype=jnp.int32)
    valid = slot - per_slot(pad_starts) < per_slot(counts)
    src = jnp.clip(slot - per_slot(shift), 0, n_assign - 1)
    slot_tok = jnp.where(valid, order[src] // TOP_K, 0)
    y = _moe_experts(block_exp, x2[slot_tok], w1, b1, w2, b2, layer=layer)
    return y[slot_of.reshape(t, TOP_K).T.reshape(-1)].reshape(TOP_K, t, d), gates


def kernel(x, w_in, hg_lb_logits, hg_norm_w, rw_mu, rw_w0, rw_w_up, rw_a0, rw_a_up, rw_g_up,
           rw_k_k, rw_k_a, rw_r_k, rw_lnx_w, rw_lnx_b, rw_v_down, rw_v_up, rw_v0, proj_a, proj_b,
           w_out, ln1_g, ln1_b, router_w, router_b, moe_w1, moe_b1, moe_w2, moe_b2, ln2_g, ln2_b):
    bsz, s, d = x.shape
    t = bsz * s
    scan = dict(bsz=bsz, seq=s)
    lb_all = jnp.cumsum(jax.nn.softmax(hg_lb_logits.astype(F32), axis=0), axis=0)
    lb_all = lb_all - lb_all[0:1]
    hg_cols = 5 * HG_WIDTH
    rw_cols = 3 * RW_WIDTH + 2 * DECAY_LORA + AAA_LORA + GATE_LORA
    x2 = x.reshape(t, d)
    v_first = None
    for l in range(DEPTH):
        w_l = w_in[l].astype(BF16)
        p_hg = _matmul(x2, w_l[:, :hg_cols], tm=1024, tn=1024)
        p_gate = _matmul(x2, w_l[:, hg_cols + rw_cols:], tm=1024, tn=1024)
        w_rw = w_in[l][:, hg_cols:hg_cols + rw_cols]
        w_self, w_nb = w_rw * (1.0 - rw_mu[l]), w_rw * rw_mu[l]
        if l > 0:
            w_self = jnp.concatenate([w_self, rw_v_down[l - 1]], axis=1)
        pad_cols = lambda w: jnp.pad(w, ((0, 0), (0, 3456 - w.shape[1]))).astype(BF16)
        x3 = jnp.pad(x2.reshape(bsz, s, d), ((0, 0), (1, 1), (0, 0)))
        x_nb = (0.5 * (x3[:, :-2] + x3[:, 2:])).reshape(t, d)
        ps = _matmul2(x2, x_nb, pad_cols(w_self), pad_cols(w_nb))

        o_a = _hgrn_scan(p_hg, lb_all[l], o_fwd=_hgrn_scan(p_hg, lb_all[l], **scan),
                         norm_w=hg_norm_w[l], **scan)

        r = ps[:, :RW_WIDTH]
        zeros = jnp.zeros((RW_WIDTH,), F32)
        if l == 0:
            v_first = ps[:, 2 * RW_WIDTH:3 * RW_WIDTH]
            mixing = dict(v_first=None, v_up=None)
            v0 = zeros
        else:
            v_up = rw_v_up[l - 1]
            mixing = dict(v_first=v_first, v_up=jnp.pad(v_up, ((0, 3456 - rw_cols - v_up.shape[0]), (0, 0))))
            v0 = rw_v0[l - 1]
        vec = jnp.stack([rw_w0[l, 0], rw_w0[l, 1], rw_a0[l], v0, rw_k_k[l], rw_k_a[l], zeros, zeros])
        lw_f, lw_b, a, k, kk, v, g = _rwkv_prep(ps, w_up=rw_w_up[l], a_up=rw_a_up[l], g_up=rw_g_up[l],
                                                vec=vec, **mixing)
        y_f, y_b = _rwkv_scan(ps, k, v, kk, a, lw_f, lw_b, **scan)
        y = _rw_heads(y_f + y_b)
        yc = y - jnp.mean(y, axis=-1, keepdims=True)
        yn = yc * lax.rsqrt(jnp.mean(yc * yc, axis=-1, keepdims=True) + RW_LN_EPS)
        yn = yn * _rw_heads(rw_lnx_w[l][None])[0] + _rw_heads(rw_lnx_b[l][None])[0]
        bonus = jnp.sum(_rw_heads(r) * _rw_heads(k) * rw_r_k[l], axis=-1, keepdims=True) * _rw_heads(v)
        o_b = (yn + bonus).reshape(t, RW_WIDTH) * g

        x2 = _merge_ln(o_a, o_b, p_gate, x2, proj_a[l], proj_b[l], w_out[l], ln1_g[l], ln1_b[l])
        yg, gates = _moe(x2, router_w[l], router_b[l], moe_w1, moe_b1, moe_w2, moe_b2, layer=l)
        x2 = _combine_ln(x2, yg, gates, ln2_g[l], ln2_b[l])
    return x2.reshape(bsz, s, d)
```

```python
import functools

import jax
import jax.numpy as jnp
from jax import lax
from jax.experimental import pallas as pl
from jax.experimental.pallas import tpu as pltpu

F32 = jnp.float32
BF16 = jnp.bfloat16

D_MODEL = 1024
DEPTH = 4
HG_HEADS = 8
HG_D = 128
HG_WIDTH = HG_HEADS * HG_D
RW_HEAD = 64
RW_HEADS = D_MODEL // RW_HEAD
RW_WIDTH = D_MODEL
DECAY_LORA = 64
AAA_LORA = 64
GATE_LORA = 128
N_EXPERTS = 32
TOP_K = 4
MOE_BLOCK = 512
SWIGLU_ALPHA = 1.702
SWIGLU_LIMIT = 7.0
NORM_EPS = 1e-5
RW_LN_EPS = 64e-5
DN_ALPHA = (2 * DEPTH) ** 0.25

CHUNK = 64
SUB = 16
DIAG_LOG_RANGE = 60.0
RW_LANES = 256
RW_GROUPS = 2
VMEM_LIMIT = 56 * 1024 * 1024

_NT = (((1,), (1,)), ((), ()))
_TN = (((0,), (0,)), ((), ()))


def _dot(a, b):
    return jnp.dot(a.astype(BF16), b.astype(BF16), preferred_element_type=F32)


def _dot_nt(a, b):
    return lax.dot_general(a.astype(BF16), b.astype(BF16), _NT, preferred_element_type=F32)


def _dot_tn(a, b):
    return lax.dot_general(a.astype(BF16), b.astype(BF16), _TN, preferred_element_type=F32)


def _dot_f32(a, b):
    return jnp.dot(a, b, preferred_element_type=F32, precision=lax.Precision.HIGHEST)


def _mm_kernel(x_ref, w_ref, o_ref):
    o_ref[...] = _dot(x_ref[...], w_ref[...]).astype(o_ref.dtype)


def _matmul(x, w, *, tm=512, tn=512, out_dtype=F32):
    m, k = x.shape
    n = w.shape[1]
    tm = min(tm, m)
    tn = min(tn, n)
    assert m % tm == 0 and n % tn == 0
    return pl.pallas_call(
        _mm_kernel,
        grid=(n // tn, m // tm),
        in_specs=[pl.BlockSpec((tm, k), lambda j, i: (i, 0)),
                  pl.BlockSpec((k, tn), lambda j, i: (0, j))],
        out_specs=pl.BlockSpec((tm, tn), lambda j, i: (i, j)),
        out_shape=jax.ShapeDtypeStruct((m, n), out_dtype),
        compiler_params=pltpu.CompilerParams(
            dimension_semantics=("arbitrary", "arbitrary"), vmem_limit_bytes=VMEM_LIMIT),
        name="matmul",
    )(x, w)


def _mm2_kernel(x_ref, y_ref, wx_ref, wy_ref, o_ref):
    o_ref[...] = _dot(x_ref[...], wx_ref[...]) + _dot(y_ref[...], wy_ref[...])


def _matmul2(x, y, wx, wy, *, tm=1024, tn=1152):
    m, k = x.shape
    n = wx.shape[1]
    assert m % tm == 0 and n % tn == 0
    xspec = pl.BlockSpec((tm, k), lambda j, i: (i, 0))
    wspec = pl.BlockSpec((k, tn), lambda j, i: (0, j))
    return pl.pallas_call(
        _mm2_kernel,
        grid=(n // tn, m // tm),
        in_specs=[xspec, xspec, wspec, wspec],
        out_specs=pl.BlockSpec((tm, tn), lambda j, i: (i, j)),
        out_shape=jax.ShapeDtypeStruct((m, n), F32),
        compiler_params=pltpu.CompilerParams(
            dimension_semantics=("arbitrary", "arbitrary"), vmem_limit_bytes=VMEM_LIMIT),
        name="matmul2",
    )(x, y, wx, wy)


LANES = 128
NEG_BIG = -3.0e38


def _router_kernel(x_ref, w_ref, b_ref, idx_ref, gate_ref):
    logits = _dot_f32(x_ref[...], w_ref[...]) + b_ref[...]
    lane = lax.broadcasted_iota(jnp.int32, logits.shape, 1).astype(F32)
    cur = logits
    vals, idxs = [], []
    for _ in range(TOP_K):
        top = jnp.max(cur, axis=-1, keepdims=True)
        idx = jnp.min(jnp.where(cur == top, lane, float(LANES)), axis=-1, keepdims=True)
        vals.append(top)
        idxs.append(idx)
        cur = jnp.where(lane == idx, NEG_BIG, cur)
    exps = [jnp.exp(v - vals[0]) for v in vals]
    total = exps[0]
    for e in exps[1:]:
        total = total + e
    idx_out = jnp.zeros_like(logits)
    gate_out = jnp.zeros_like(logits)
    for j in range(TOP_K):
        idx_out = jnp.where(lane == float(j), idxs[j], idx_out)
        gate_out = jnp.where(lane == float(j), exps[j] / total, gate_out)
    idx_ref[...] = idx_out.astype(jnp.int32)
    gate_ref[...] = gate_out


def _router(x, w, b, *, tm=512):
    m, k = x.shape
    pad = LANES - N_EXPERTS
    w = jnp.pad(w, ((0, 0), (0, pad)))
    b = jnp.pad(b, (0, pad), constant_values=NEG_BIG).reshape(1, LANES)
    row = pl.BlockSpec((tm, LANES), lambda i: (i, 0))
    idx, gates = pl.pallas_call(
        _router_kernel,
        grid=(m // tm,),
        in_specs=[pl.BlockSpec((tm, k), lambda i: (i, 0)),
                  pl.BlockSpec((k, LANES), lambda i: (0, 0)),
                  pl.BlockSpec((1, LANES), lambda i: (0, 0))],
        out_specs=[row, row],
        out_shape=[jax.ShapeDtypeStruct((m, LANES), jnp.int32), jax.ShapeDtypeStruct((m, LANES), F32)],
        compiler_params=pltpu.CompilerParams(
            dimension_semantics=("arbitrary",), vmem_limit_bytes=VMEM_LIMIT),
        name="router",
    )(x, w, b)
    return idx[:, :TOP_K], gates[:, :TOP_K]


def _hgrn_kernel(q_ref, z_ref, v_ref, lb_ref, *rest, reverse, n_chunks):
    st_ref = rest[-1]

    @pl.when(pl.program_id(2) == 0)
    def _():
        st_ref[...] = jnp.zeros_like(st_ref)

    lb = lb_ref[...]
    f = lb + (1.0 - lb) * jax.nn.sigmoid(z_ref[...])
    lf = jnp.log(f)
    sub_total = jnp.sum(lf.reshape(lf.shape[0] // SUB, SUB, HG_D), axis=1)
    factorable = jnp.min(sub_total) > -DIAG_LOG_RANGE
    args = (q_ref, f, lf, v_ref) + rest

    @pl.when(factorable)
    def _():
        _hgrn_block(*args, reverse=reverse, n_chunks=n_chunks, pairwise=False)

    @pl.when(jnp.logical_not(factorable))
    def _():
        _hgrn_block(*args, reverse=reverse, n_chunks=n_chunks, pairwise=True)


def _hgrn_block(q_ref, f_blk, lf_blk, v_ref, *rest, reverse, n_chunks, pairwise):
    if reverse:
        of_ref, og_ref, nw_ref, o_ref, st_ref = rest
    else:
        o_ref, st_ref = rest
    c = CHUNK
    row = lax.broadcasted_iota(jnp.int32, (c, c), 0)
    col = lax.broadcasted_iota(jnp.int32, (c, c), 1)
    causal = (col >= row) if reverse else (col <= row)
    tri = jnp.where(causal, 1.0, 0.0).astype(F32)
    diag_mask = causal & (row // SUB == col // SUB)
    srow = lax.broadcasted_iota(jnp.int32, (SUB, 1), 0)

    splits = []
    size = c
    while size > SUB:
        for lo in range(0, c, size):
            splits.append((lo, lo + size // 2, lo + size))
        size //= 2

    order = list(range(n_chunks - 1, -1, -1) if reverse else range(n_chunks))
    chunks = [dict(rows=slice(ci * c, (ci + 1) * c)) for ci in order]
    for ch in chunks:
        rows = ch["rows"]
        ch.update(lf=lf_blk[rows], kk=1.0 - f_blk[rows], q=q_ref[rows, :], v=v_ref[rows, :])
        ch["b"] = _dot_f32(tri, ch["lf"])
    for ch in chunks:
        q, kk, v, b, lf = ch["q"], ch["kk"], ch["v"], ch["b"], ch["lf"]
        b_end = b[0:1, :] if reverse else b[c - 1:c, :]
        ch["qdec"] = (q * jnp.exp(b)).astype(BF16)
        ch["dec"] = jnp.exp(b_end)
        ch["kv"] = _dot_tn(v, kk * jnp.exp(b_end - b))
        scores = []
        for lo, mid, hi in splits:
            if reverse:
                anc = b[mid:mid + 1, :]
                qs, ks = slice(lo, mid), slice(mid, hi)
            else:
                anc = b[mid - 1:mid, :]
                qs, ks = slice(mid, hi), slice(lo, mid)
            scores.append((qs, ks, _dot_nt(q[qs] * jnp.exp(b[qs] - anc), kk[ks] * jnp.exp(anc - b[ks]))))
        ch["scores"] = scores
        if not pairwise:
            edge = b - lf
            anc = jnp.concatenate(
                [jnp.broadcast_to(edge[d0 + SUB - 1:d0 + SUB] if reverse else edge[d0:d0 + 1], (SUB, HG_D))
                 for d0 in range(0, c, SUB)], axis=0)
            ch["diag"] = jnp.where(diag_mask, _dot_nt(q * jnp.exp(b - anc), kk * jnp.exp(anc - b)), 0.0)
    for ch in chunks:
        v = ch["v"]
        parts = [None] * (c // SUB)
        for qs, ks, s in ch["scores"]:
            contrib = _dot(s, v[ks])
            for j in range(qs.start // SUB, qs.stop // SUB):
                piece = contrib[j * SUB - qs.start:(j + 1) * SUB - qs.start]
                parts[j] = piece if parts[j] is None else parts[j] + piece
        intra = jnp.concatenate([jnp.zeros((SUB, HG_D), F32) if p is None else p for p in parts], axis=0)
        if pairwise:
            accs = []
            for d0 in range(0, c, SUB):
                qb, bb, kb, vb = (t[d0:d0 + SUB] for t in (ch["q"], ch["b"], ch["kk"], v))
                acc = jnp.zeros((SUB, HG_D), F32)
                for s in range(SUB):
                    mask = (srow <= s) if reverse else (srow >= s)
                    e = jnp.where(mask, jnp.exp(jnp.minimum(bb - bb[s:s + 1, :], 0.0)), 0.0)
                    w = jnp.sum(qb * kb[s:s + 1, :] * e, axis=-1, keepdims=True)
                    acc = acc + w * vb[s:s + 1, :]
                accs.append(acc)
            ch["intra"] = intra + jnp.concatenate(accs, axis=0)
        else:
            ch["intra"] = intra + _dot(ch["diag"], v)
    st = st_ref[...]
    for ch in chunks:
        ch["st"] = st
        st = st * ch["dec"] + ch["kv"]
    st_ref[...] = st
    for ch in chunks:
        rows = ch["rows"]
        o = ch["intra"] + _dot_nt(ch["qdec"], ch["st"])
        if reverse:
            o = o + of_ref[rows, :]
            o = o * lax.rsqrt(jnp.mean(o * o, axis=-1, keepdims=True) + NORM_EPS) * nw_ref[...]
            o = o * jax.nn.silu(og_ref[rows, :])
        o_ref[rows, :] = o


def _hgrn_scan(p_hg, lb, *, bsz, seq, o_fwd=None, norm_w=None, tb=1024):
    reverse = o_fwd is not None
    t = bsz * seq
    tb = min(tb, seq)
    nblk = seq // tb
    h = HG_HEADS
    zoff = 2 * h if reverse else h

    def col(off):
        return pl.BlockSpec((tb, HG_D),
                            lambda b, hh, i: (b * nblk + (nblk - 1 - i if reverse else i), off + hh))

    vec = pl.BlockSpec((1, HG_D), lambda b, hh, i: (0, 0))
    in_specs = [col(0), col(zoff), col(3 * h), pl.BlockSpec((1, HG_D), lambda b, hh, i: (0, hh))]
    args = [p_hg, p_hg, p_hg, lb.reshape(1, -1)]
    if reverse:
        in_specs += [col(0), col(4 * h), vec]
        args += [o_fwd, p_hg, norm_w.reshape(1, -1)]
    return pl.pallas_call(
        functools.partial(_hgrn_kernel, reverse=reverse, n_chunks=tb // CHUNK),
        grid=(bsz, h, nblk),
        in_specs=in_specs,
        out_specs=col(0),
        out_shape=jax.ShapeDtypeStruct((t, h * HG_D), F32),
        scratch_shapes=[pltpu.VMEM((HG_D, HG_D), F32)],
        compiler_params=pltpu.CompilerParams(
            dimension_semantics=("arbitrary", "arbitrary", "arbitrary"),
            vmem_limit_bytes=VMEM_LIMIT),
        name="hgrn_bwd" if reverse else "hgrn_fwd",
    )(*args)


class _RwkvMasks:
    def __init__(self, reverse):
        c, n = CHUNK, RW_LANES
        m = (n // RW_HEAD) * c
        row = lax.broadcasted_iota(jnp.int32, (c, c), 0)
        col = lax.broadcasted_iota(jnp.int32, (c, c), 1)
        self.tri = jnp.where((col >= row) if reverse else (col <= row), 1.0, 0.0).astype(F32)
        self.stack = (lax.broadcasted_iota(jnp.int32, (m, n), 0) // c
                      == lax.broadcasted_iota(jnp.int32, (m, n), 1) // RW_HEAD)
        tr = lax.broadcasted_iota(jnp.int32, (c, m), 0)
        tc = lax.broadcasted_iota(jnp.int32, (c, m), 1) % c
        self.strict = (tc > tr) if reverse else (tc < tr)
        self.incl = (tc >= tr) if reverse else (tc <= tr)
        self.eye = jnp.where(tc == tr, 1.0, 0.0).astype(F32)
        self.diag = (lax.broadcasted_iota(jnp.int32, (n, n), 0) // RW_HEAD
                     == lax.broadcasted_iota(jnp.int32, (n, n), 1) // RW_HEAD)


def _rwkv_stack(t, masks):
    return jnp.where(masks.stack, jnp.concatenate([t] * (RW_LANES // RW_HEAD), axis=0), 0.0).astype(BF16)


def _lockstep(gens):
    gens = list(gens)
    while gens:
        alive = []
        for gen in gens:
            try:
                next(gen)
                alive.append(gen)
            except StopIteration:
                pass
        gens = alive


def _rwkv_prepare(p, refs, rows, lanes, masks, reverse):
    c = CHUNK
    m = (RW_LANES // RW_HEAD) * c
    stack = functools.partial(_rwkv_stack, masks=masks)
    r, k, v, kk, a, lw = (ref[rows, lanes] for ref in refs)
    lp = _dot_f32(masks.tri, lw)
    yield
    lp_end = lp[0:1, :] if reverse else lp[c - 1:c, :]
    pinv = jnp.exp(-lp)
    dec = jnp.exp(lp_end - lp)
    kb = kk * a
    ar = jnp.concatenate([-kk * jnp.exp(lp - lw), r * jnp.exp(lp)], axis=0).astype(BF16)
    bk = jnp.concatenate([stack(kb * pinv), stack(k * pinv)], axis=0)
    v_s = stack(v)
    sc = _dot_nt(ar, bk)
    yield
    ab = jnp.where(masks.strict, sc[:c, :m], 0.0)
    akrk = jnp.concatenate([jnp.where(masks.strict, sc[:c, m:], 0.0),
                            jnp.where(masks.incl, sc[c:, m:], 0.0)], axis=0)
    p.update(ar=ar, v=v.astype(BF16), akrk_v=_dot(akrk, v_s),
             rb=jnp.where(masks.incl, sc[c:, :m], 0.0).astype(BF16),
             pw=_dot(ab, stack(ab)), tinv=masks.eye + ab,
             bkp=jnp.concatenate([kb * dec, k * dec], axis=0).astype(BF16),
             gdec=jnp.exp(lp_end))


def _rwkv_double(p, masks, last):
    c = CHUNK
    pw_s = _rwkv_stack(p["pw"], masks)
    if last:
        p["tinv"] = p["tinv"] + _dot(p["tinv"], pw_s)
    else:
        z = _dot(jnp.concatenate([p["pw"], p["tinv"]], axis=0), pw_s)
        p["pw"], p["tinv"] = z[:c], p["tinv"] + z[c:]


def _rwkv_chain(preps, g_ref, y_ref, rows_seq, lanes, masks):
    c = CHUNK
    for p, rows in zip(preps, rows_seq):
        g = g_ref[...]
        arg = _dot_nt(p["ar"], g)
        yield
        u = _dot(p["tinv"], _rwkv_stack(arg[:c] + p["akrk_v"][:c], masks))
        yield
        y_ref[rows, lanes] = arg[c:] + p["akrk_v"][c:] + _dot(p["rb"], _rwkv_stack(u, masks))
        uv = jnp.concatenate([u.astype(BF16), p["v"]], axis=0)
        g_ref[...] = g * p["gdec"] + jnp.where(masks.diag, _dot_tn(uv, p["bkp"]), 0.0)
        yield


def _rwkv_kernel(*refs, n_chunks):
    in_f, in_b = refs[0:6], refs[6:12]
    y_f, y_b, g_f, g_b = refs[12:16]

    @pl.when(pl.program_id(2) == 0)
    def _():
        g_f[...] = jnp.zeros_like(g_f)
        g_b[...] = jnp.zeros_like(g_b)

    c = CHUNK
    rows = [slice(ci * c, (ci + 1) * c) for ci in range(n_chunks)]
    masks = (_RwkvMasks(False), _RwkvMasks(True))
    plan = []
    for gi in range(RW_GROUPS):
        lanes = slice(gi * RW_LANES, (gi + 1) * RW_LANES)
        plan.append((in_f, y_f, g_f.at[gi], masks[0], False, rows, lanes))
        plan.append((in_b, y_b, g_b.at[gi], masks[1], True, rows[::-1], lanes))
    preps = [[{} for _ in rows] for _ in plan]
    _lockstep(_rwkv_prepare(p, ins, rw, lanes, mk, rev)
              for (ins, _, _, mk, rev, rws, lanes), plist in zip(plan, preps)
              for p, rw in zip(plist, rws))
    n_double = c.bit_length() - 2
    for it in range(n_double):
        for chain, plist in zip(plan, preps):
            for p in plist:
                _rwkv_double(p, chain[3], last=it == n_double - 1)
    _lockstep(_rwkv_chain(plist, g_ref, y_ref, rws, lanes, mk)
              for (_, y_ref, g_ref, mk, _, rws, lanes), plist in zip(plan, preps))


def _rwkv_scan(r, k, v, kk, a, lw_f, lw_b, *, bsz, seq, tb=256):
    t = bsz * seq
    tb = min(tb, seq)
    nblk = seq // tb
    n = RW_LANES * RW_GROUPS
    spec_f = pl.BlockSpec((tb, n), lambda b, hh, i: (b * nblk + i, hh))
    spec_b = pl.BlockSpec((tb, n), lambda b, hh, i: (b * nblk + nblk - 1 - i, hh))
    out = jax.ShapeDtypeStruct((t, RW_WIDTH), F32)
    state = pltpu.VMEM((RW_GROUPS, RW_LANES, RW_LANES), F32)
    return pl.pallas_call(
        functools.partial(_rwkv_kernel, n_chunks=tb // CHUNK),
        grid=(bsz, RW_WIDTH // n, nblk),
        in_specs=[spec_f] * 6 + [spec_b] * 6,
        out_specs=[spec_f, spec_b],
        out_shape=[out, out],
        scratch_shapes=[state, state],
        compiler_params=pltpu.CompilerParams(
            dimension_semantics=("arbitrary", "arbitrary", "arbitrary"),
            vmem_limit_bytes=VMEM_LIMIT),
        name="rwkv",
    )(r, k, v, kk, a, lw_f, r, k, v, kk, a, lw_b)


LORA_COLS = 384


def _rwkv_prep_kernel(ps_ref, vf_ref, wup_ref, aup_ref, gup_ref, vup_ref, vec_ref, seg_ref,
                      lwf_ref, lwb_ref, a_ref, k2_ref, kk_ref, v2_ref, g_ref, *, mix):
    w = RW_WIDTH
    k = ps_ref[:, w:2 * w]
    v = ps_ref[:, 2 * w:3 * w]
    lora = ps_ref[:, 3 * w:3 * w + LORA_COLS]
    vec = vec_ref[...]
    for d, out_ref in ((0, lwf_ref), (1, lwb_ref)):
        z = vec[d:d + 1] + _dot(jnp.tanh(lora[:, d * DECAY_LORA:(d + 1) * DECAY_LORA]), wup_ref[d])
        softplus_neg = jnp.maximum(-z, 0.0) + jnp.log(1.0 + jnp.exp(-jnp.abs(z)))
        out_ref[...] = -jnp.exp(-softplus_neg - 0.5)
    off = 2 * DECAY_LORA
    a = jax.nn.sigmoid(vec[2:3] + _dot(lora[:, off:off + AAA_LORA], aup_ref[...]))
    off += AAA_LORA
    g_ref[...] = _dot(jax.nn.sigmoid(lora[:, off:off + GATE_LORA]), gup_ref[...])
    off += GATE_LORA
    if mix:
        v = v + (vf_ref[...] - v) * jax.nn.sigmoid(vec[3:4] + _dot(lora[:, off:LORA_COLS], vup_ref[...]))
    v2_ref[...] = v
    kx = k * vec[4:5]
    sq = kx * kx
    sq_hi = sq.astype(BF16)
    sq_lo = (sq - sq_hi.astype(F32)).astype(BF16)
    norm2 = _dot(sq_hi, seg_ref[...]) + _dot(sq_lo, seg_ref[...])
    kk_ref[...] = kx / jnp.maximum(jnp.sqrt(norm2), 1e-12)
    k2_ref[...] = k * (1.0 + (a - 1.0) * vec[5:6])
    a_ref[...] = a


def _rwkv_prep(ps, v_first, w_up, a_up, g_up, v_up, vec, *, tm=256):
    t = ps.shape[0]
    w = RW_WIDTH
    mix = v_first is not None
    if not mix:
        v_first = ps
        v_up = jnp.zeros((LORA_COLS - 2 * DECAY_LORA - AAA_LORA - GATE_LORA, w), F32)
    head = jnp.arange(w, dtype=jnp.int32) // RW_HEAD
    seg = (head[:, None] == head[None, :]).astype(BF16)
    row = lambda i: (i, 0)
    fixed2 = lambda i: (0, 0)
    full = lambda arr: pl.BlockSpec(arr.shape, (lambda i: (0, 0, 0)) if arr.ndim == 3 else fixed2)
    out = jax.ShapeDtypeStruct((t, w), F32)
    return pl.pallas_call(
        functools.partial(_rwkv_prep_kernel, mix=mix),
        grid=(t // tm,),
        in_specs=[pl.BlockSpec((tm, ps.shape[1]), row), pl.BlockSpec((tm, w), row),
                  full(w_up), full(a_up), full(g_up), full(v_up), full(vec), full(seg)],
        out_specs=[pl.BlockSpec((tm, w), row)] * 7,
        out_shape=[out] * 7,
        compiler_params=pltpu.CompilerParams(
            dimension_semantics=("arbitrary",), vmem_limit_bytes=VMEM_LIMIT),
        name="rwkv_prep",
    )(ps, v_first, w_up, a_up, g_up, v_up, vec, seg)


PAIR = 256


def _moe_kernel(be_ref, xs_ref, w1_ref, b1_ref, w2_ref, b2_ref, o_ref, w1p_ref, w2b_ref):
    i = pl.program_id(0)
    f2 = w1_ref.shape[-1]
    half = PAIR // 2

    @pl.when((i == 0) | (be_ref[i] != be_ref[jnp.maximum(i - 1, 0)]))
    def _():
        src = lax.broadcasted_iota(jnp.int32, (PAIR, PAIR), 0)
        dst = lax.broadcasted_iota(jnp.int32, (PAIR, PAIR), 1)
        perm = jnp.where(src == jnp.where(dst < half, 2 * dst, 2 * (dst - half) + 1), 1.0, 0.0)
        for j in range(0, f2, PAIR):
            w1p_ref[:, j:j + PAIR] = _dot(w1_ref[:, j:j + PAIR], perm).astype(BF16)
        w2b_ref[...] = w2_ref[...].astype(BF16)

    h = _dot(xs_ref[...], w1p_ref[...]) + b1_ref[...]
    acts = []
    for j in range(0, f2, PAIR):
        glu = jnp.minimum(h[:, j:j + half], SWIGLU_LIMIT)
        lin = jnp.clip(h[:, j + half:j + PAIR], -SWIGLU_LIMIT, SWIGLU_LIMIT)
        acts.append((glu * jax.nn.sigmoid(SWIGLU_ALPHA * glu) * (lin + 1.0)).astype(BF16))
    o_ref[...] = _dot(jnp.concatenate(acts, axis=1), w2b_ref[...]) + b2_ref[...]


def _moe_experts(block_exp, xs, w1, b1, w2, b2, *, layer):
    n_slots, d = xs.shape
    f2 = w1.shape[-1]
    f = f2 // 2
    blk = MOE_BLOCK
    n_blocks = n_slots // blk
    b1p = b1.reshape(DEPTH, N_EXPERTS, f2 // PAIR, PAIR // 2, 2).swapaxes(-1, -2)
    b1p = b1p.reshape(DEPTH, N_EXPERTS, 1, f2)
    wspec = lambda shape: pl.BlockSpec((None, None) + shape, lambda i, be: (layer, be[i], 0, 0))
    return pl.pallas_call(
        _moe_kernel,
        grid_spec=pltpu.PrefetchScalarGridSpec(
            num_scalar_prefetch=1,
            grid=(n_blocks,),
            in_specs=[pl.BlockSpec((blk, d), lambda i, be: (i, 0)),
                      wspec((d, f2)), wspec((1, f2)), wspec((f, d)), wspec((1, d))],
            out_specs=pl.BlockSpec((blk, d), lambda i, be: (i, 0)),
            scratch_shapes=[pltpu.VMEM((d, f2), BF16), pltpu.VMEM((f, d), BF16)]),
        out_shape=jax.ShapeDtypeStruct((n_slots, d), F32),
        compiler_params=pltpu.CompilerParams(
            dimension_semantics=("arbitrary",), vmem_limit_bytes=VMEM_LIMIT),
        name="moe_experts",
    )(block_exp, xs, w1, b1p, w2, b2.reshape(DEPTH, N_EXPERTS, 1, d))


def _ln_rows(y, g, b):
    yc = y - jnp.mean(y, axis=-1, keepdims=True)
    var = jnp.mean(yc * yc, axis=-1, keepdims=True)
    return yc * lax.rsqrt(var + NORM_EPS) * g + b


def _merge_ln_kernel(oa_ref, ob_ref, gate_ref, x_ref, pa_ref, pb_ref, wo_ref, g_ref, b_ref, o_ref):
    d = x_ref.shape[-1]
    merged = (jax.nn.sigmoid(gate_ref[:, :d]) * _dot(oa_ref[...], pa_ref[...])
              + jax.nn.sigmoid(gate_ref[:, d:]) * _dot(ob_ref[...], pb_ref[...]))
    o_ref[...] = _ln_rows(DN_ALPHA * x_ref[...] + _dot(merged, wo_ref[...]), g_ref[...], b_ref[...])


def _merge_ln(o_a, o_b, p_gate, x, proj_a, proj_b, w_out, g, b, *, tm=512):
    t, d = x.shape
    row = lambda i: (i, 0)
    fixed = lambda i: (0, 0)
    wspec = pl.BlockSpec((d, d), fixed)
    return pl.pallas_call(
        _merge_ln_kernel,
        grid=(t // tm,),
        in_specs=[pl.BlockSpec((tm, d), row), pl.BlockSpec((tm, d), row), pl.BlockSpec((tm, 2 * d), row),
                  pl.BlockSpec((tm, d), row), wspec, wspec, wspec,
                  pl.BlockSpec((1, d), fixed), pl.BlockSpec((1, d), fixed)],
        out_specs=pl.BlockSpec((tm, d), row),
        out_shape=jax.ShapeDtypeStruct((t, d), F32),
        compiler_params=pltpu.CompilerParams(
            dimension_semantics=("arbitrary",), vmem_limit_bytes=VMEM_LIMIT),
        name="merge_ln",
    )(o_a, o_b, p_gate, x, proj_a.astype(BF16), proj_b.astype(BF16), w_out.astype(BF16),
      g.reshape(1, d), b.reshape(1, d))


def _combine_ln_kernel(x_ref, yg_ref, gate_ref, g_ref, b_ref, o_ref):
    gate = gate_ref[...]
    moe = gate[:, 0:1] * yg_ref[0]
    for kk in range(1, TOP_K):
        moe = moe + gate[:, kk:kk + 1] * yg_ref[kk]
    o_ref[...] = _ln_rows(DN_ALPHA * x_ref[...] + moe, g_ref[...], b_ref[...])


def _combine_ln(x, yg, gates, g, b, *, tm=512):
    t, d = x.shape
    row = lambda i: (i, 0)
    fixed = lambda i: (0, 0)
    return pl.pallas_call(
        _combine_ln_kernel,
        grid=(t // tm,),
        in_specs=[pl.BlockSpec((tm, d), row), pl.BlockSpec((TOP_K, tm, d), lambda i: (0, i, 0)),
                  pl.BlockSpec((tm, TOP_K), row),
                  pl.BlockSpec((1, d), fixed), pl.BlockSpec((1, d), fixed)],
        out_specs=pl.BlockSpec((tm, d), row),
        out_shape=jax.ShapeDtypeStruct((t, d), F32),
        compiler_params=pltpu.CompilerParams(
            dimension_semantics=("arbitrary",), vmem_limit_bytes=VMEM_LIMIT),
        name="combine_ln",
    )(x, yg, gates, g.reshape(1, d), b.reshape(1, d))


def _rw_heads(t):
    return t.reshape(t.shape[0], RW_HEADS, RW_HEAD)


def _moe(x2, router_w, router_b, w1, b1, w2, b2, *, layer):
    t, d = x2.shape
    n_assign = t * TOP_K
    n_blocks = -(-n_assign // MOE_BLOCK) + N_EXPERTS
    n_slots = n_blocks * MOE_BLOCK
    top_idx, gates = _router(x2, router_w, router_b)
    e_flat = top_idx.reshape(-1)
    order = jnp.argsort(e_flat).astype(jnp.int32)
    rank = jnp.argsort(order).astype(jnp.int32)
    experts = jnp.arange(N_EXPERTS, dtype=jnp.int32)
    counts = jnp.bincount(e_flat, length=N_EXPERTS).astype(jnp.int32)
    starts = jnp.cumsum(counts) - counts
    padded = (counts + MOE_BLOCK - 1) // MOE_BLOCK * MOE_BLOCK
    pad_ends = jnp.cumsum(padded)
    pad_starts = pad_ends - padded
    shift = pad_starts - starts
    slot_of = rank + jnp.sum(jnp.where(e_flat[:, None] == experts, shift, 0), axis=1)
    block_start = jnp.arange(n_blocks, dtype=jnp.int32) * MOE_BLOCK
    block_exp = jnp.minimum(jnp.sum(pad_ends[None, :] <= block_start[:, None], axis=1),
                            N_EXPERTS - 1).astype(jnp.int32)
    blk_hot = block_exp[:, None] == experts
    per_slot = lambda tab: jnp.repeat(jnp.sum(jnp.where(blk_hot, tab, 0), axis=1), MOE_BLOCK)
    slot = jnp.arange(n_slots, dtype=jnp.int32)
    valid = slot - per_slot(pad_starts) < per_slot(counts)
    src = jnp.clip(slot - per_slot(shift), 0, n_assign - 1)
    slot_tok = jnp.where(valid, order[src] // TOP_K, 0)
    y = _moe_experts(block_exp, x2[slot_tok], w1, b1, w2, b2, layer=layer)
    return y[slot_of.reshape(t, TOP_K).T.reshape(-1)].reshape(TOP_K, t, d), gates


def kernel(x, w_in, hg_lb_logits, hg_norm_w, rw_mu, rw_w0, rw_w_up, rw_a0, rw_a_up, rw_g_up,
           rw_k_k, rw_k_a, rw_r_k, rw_lnx_w, rw_lnx_b, rw_v_down, rw_v_up, rw_v0, proj_a, proj_b,
           w_out, ln1_g, ln1_b, router_w, router_b, moe_w1, moe_b1, moe_w2, moe_b2, ln2_g, ln2_b):
    bsz, s, d = x.shape
    t = bsz * s
    scan = dict(bsz=bsz, seq=s)
    lb_all = jnp.cumsum(jax.nn.softmax(hg_lb_logits.astype(F32), axis=0), axis=0)
    lb_all = lb_all - lb_all[0:1]
    hg_cols = 5 * HG_WIDTH
    rw_cols = 3 * RW_WIDTH + 2 * DECAY_LORA + AAA_LORA + GATE_LORA
    x2 = x.reshape(t, d)
    v_first = None
    for l in range(DEPTH):
        w_l = w_in[l].astype(BF16)
        p_hg = _matmul(x2, w_l[:, :hg_cols], tm=1024, tn=1024)
        p_gate = _matmul(x2, w_l[:, hg_cols + rw_cols:], tm=1024, tn=1024)
        w_rw = w_in[l][:, hg_cols:hg_cols + rw_cols]
        w_self, w_nb = w_rw * (1.0 - rw_mu[l]), w_rw * rw_mu[l]
        if l > 0:
            w_self = jnp.concatenate([w_self, rw_v_down[l - 1]], axis=1)
        pad_cols = lambda w: jnp.pad(w, ((0, 0), (0, 3456 - w.shape[1]))).astype(BF16)
        x3 = jnp.pad(x2.reshape(bsz, s, d), ((0, 0), (1, 1), (0, 0)))
        x_nb = (0.5 * (x3[:, :-2] + x3[:, 2:])).reshape(t, d)
        ps = _matmul2(x2, x_nb, pad_cols(w_self), pad_cols(w_nb))

        o_a = _hgrn_scan(p_hg, lb_all[l], o_fwd=_hgrn_scan(p_hg, lb_all[l], **scan),
                         norm_w=hg_norm_w[l], **scan)

        r = ps[:, :RW_WIDTH]
        zeros = jnp.zeros((RW_WIDTH,), F32)
        if l == 0:
            v_first = ps[:, 2 * RW_WIDTH:3 * RW_WIDTH]
            mixing = dict(v_first=None, v_up=None)
            v0 = zeros
        else:
            v_up = rw_v_up[l - 1]
            mixing = dict(v_first=v_first, v_up=jnp.pad(v_up, ((0, 3456 - rw_cols - v_up.shape[0]), (0, 0))))
            v0 = rw_v0[l - 1]
        vec = jnp.stack([rw_w0[l, 0], rw_w0[l, 1], rw_a0[l], v0, rw_k_k[l], rw_k_a[l], zeros, zeros])
        lw_f, lw_b, a, k, kk, v, g = _rwkv_prep(ps, w_up=rw_w_up[l], a_up=rw_a_up[l], g_up=rw_g_up[l],
                                                vec=vec, **mixing)
        y_f, y_b = _rwkv_scan(ps, k, v, kk, a, lw_f, lw_b, **scan)
        y = _rw_heads(y_f + y_b)
        yc = y - jnp.mean(y, axis=-1, keepdims=True)
        yn = yc * lax.rsqrt(jnp.mean(yc * yc, axis=-1, keepdims=True) + RW_LN_EPS)
        yn = yn * _rw_heads(rw_lnx_w[l][None])[0] + _rw_heads(rw_lnx_b[l][None])[0]
        bonus = jnp.sum(_rw_heads(r) * _rw_heads(k) * rw_r_k[l], axis=-1, keepdims=True) * _rw_heads(v)
        o_b = (yn + bonus).reshape(t, RW_WIDTH) * g

        x2 = _merge_ln(o_a, o_b, p_gate, x2, proj_a[l], proj_b[l], w_out[l], ln1_g[l], ln1_b[l])
        yg, gates = _moe(x2, router_w[l], router_b[l], moe_w1, moe_b1, moe_w2, moe_b2, layer=l)
        x2 = _combine_ln(x2, yg, gates, ln2_g[l], ln2_b[l])
    return x2.reshape(bsz, s, d)
```

```python
import functools

import jax
import jax.numpy as jnp
from jax import lax
from jax.experimental import pallas as pl
from jax.experimental.pallas import tpu as pltpu

F32 = jnp.float32
BF16 = jnp.bfloat16

D_MODEL = 1024
DEPTH = 4
HG_HEADS = 8
HG_D = 128
HG_WIDTH = HG_HEADS * HG_D
RW_HEAD = 64
RW_HEADS = D_MODEL // RW_HEAD
RW_WIDTH = D_MODEL
DECAY_LORA = 64
AAA_LORA = 64
GATE_LORA = 128
N_EXPERTS = 32
TOP_K = 4
MOE_BLOCK = 512
SWIGLU_ALPHA = 1.702
SWIGLU_LIMIT = 7.0
NORM_EPS = 1e-5
RW_LN_EPS = 64e-5
DN_ALPHA = (2 * DEPTH) ** 0.25

CHUNK = 64
SUB = 16
DIAG_LOG_RANGE = 60.0
RW_LANES = 256
RW_GROUPS = 2
VMEM_LIMIT = 56 * 1024 * 1024

_NT = (((1,), (1,)), ((), ()))
_TN = (((0,), (0,)), ((), ()))


def _dot(a, b):
    return jnp.dot(a.astype(BF16), b.astype(BF16), preferred_element_type=F32)


def _dot_nt(a, b):
    return lax.dot_general(a.astype(BF16), b.astype(BF16), _NT, preferred_element_type=F32)


def _dot_tn(a, b):
    return lax.dot_general(a.astype(BF16), b.astype(BF16), _TN, preferred_element_type=F32)


def _dot_f32(a, b):
    return jnp.dot(a, b, preferred_element_type=F32, precision=lax.Precision.HIGHEST)


def _mm_kernel(x_ref, w_ref, o_ref):
    o_ref[...] = _dot(x_ref[...], w_ref[...]).astype(o_ref.dtype)


def _matmul(x, w, *, tm=512, tn=512, out_dtype=F32):
    m, k = x.shape
    n = w.shape[1]
    tm = min(tm, m)
    tn = min(tn, n)
    assert m % tm == 0 and n % tn == 0
    return pl.pallas_call(
        _mm_kernel,
        grid=(n // tn, m // tm),
        in_specs=[pl.BlockSpec((tm, k), lambda j, i: (i, 0)),
                  pl.BlockSpec((k, tn), lambda j, i: (0, j))],
        out_specs=pl.BlockSpec((tm, tn), lambda j, i: (i, j)),
        out_shape=jax.ShapeDtypeStruct((m, n), out_dtype),
        compiler_params=pltpu.CompilerParams(
            dimension_semantics=("arbitrary", "arbitrary"), vmem_limit_bytes=VMEM_LIMIT),
        name="matmul",
    )(x, w)


def _mm2_kernel(x_ref, y_ref, wx_ref, wy_ref, o_ref):
    o_ref[...] = _dot(x_ref[...], wx_ref[...]) + _dot(y_ref[...], wy_ref[...])


def _matmul2(x, y, wx, wy, *, tm=1024, tn=1152):
    m, k = x.shape
    n = wx.shape[1]
    tm = min(tm, m)
    assert m % tm == 0 and n % tn == 0
    xspec = pl.BlockSpec((tm, k), lambda j, i: (i, 0))
    wspec = pl.BlockSpec((k, tn), lambda j, i: (0, j))
    return pl.pallas_call(
        _mm2_kernel,
        grid=(n // tn, m // tm),
        in_specs=[xspec, xspec, wspec, wspec],
        out_specs=pl.BlockSpec((tm, tn), lambda j, i: (i, j)),
        out_shape=jax.ShapeDtypeStruct((m, n), F32),
        compiler_params=pltpu.CompilerParams(
            dimension_semantics=("arbitrary", "arbitrary"), vmem_limit_bytes=VMEM_LIMIT),
        name="matmul2",
    )(x, y, wx, wy)


LANES = 128
NEG_BIG = -3.0e38


def _router_kernel(x_ref, w_ref, b_ref, idx_ref, gate_ref, rank_ref, count_ref):
    @pl.when(pl.program_id(0) == 0)
    def _():
        count_ref[...] = jnp.zeros_like(count_ref)

    logits = _dot_f32(x_ref[...], w_ref[...]) + b_ref[...]
    tm = logits.shape[0]
    lane = lax.broadcasted_iota(jnp.int32, logits.shape, 1).astype(F32)
    cur = logits
    vals, idxs = [], []
    for _ in range(TOP_K):
        top = jnp.max(cur, axis=-1, keepdims=True)
        idx = jnp.min(jnp.where(cur == top, lane, float(LANES)), axis=-1, keepdims=True)
        vals.append(top)
        idxs.append(idx)
        cur = jnp.where(lane == idx, NEG_BIG, cur)
    exps = [jnp.exp(v - vals[0]) for v in vals]
    total = exps[0]
    for e in exps[1:]:
        total = total + e
    chosen = jnp.zeros_like(logits)
    for idx in idxs:
        chosen = jnp.where(lane == idx, 1.0, chosen)
    earlier = (lax.broadcasted_iota(jnp.int32, (tm, tm), 1)
               < lax.broadcasted_iota(jnp.int32, (tm, tm), 0))
    before = count_ref[...] + _dot(jnp.where(earlier, 1.0, 0.0), chosen)
    count_ref[...] = count_ref[...] + jnp.sum(chosen, axis=0, keepdims=True)
    idx_out = jnp.zeros_like(logits)
    gate_out = jnp.zeros_like(logits)
    rank_out = jnp.zeros_like(logits)
    for j in range(TOP_K):
        rank = jnp.sum(jnp.where(lane == idxs[j], before, 0.0), axis=-1, keepdims=True)
        idx_out = jnp.where(lane == float(j), idxs[j], idx_out)
        gate_out = jnp.where(lane == float(j), exps[j] / total, gate_out)
        rank_out = jnp.where(lane == float(j), rank, rank_out)
    idx_ref[...] = idx_out.astype(jnp.int32)
    gate_ref[...] = gate_out
    rank_ref[...] = rank_out.astype(jnp.int32)


def _router(x, w, b, *, tm=512):
    m, k = x.shape
    pad = LANES - N_EXPERTS
    w = jnp.pad(w, ((0, 0), (0, pad)))
    b = jnp.pad(b, (0, pad), constant_values=NEG_BIG).reshape(1, LANES)
    row = pl.BlockSpec((tm, LANES), lambda i: (i, 0))
    fixed = pl.BlockSpec((1, LANES), lambda i: (0, 0))
    per_token = lambda dtype: jax.ShapeDtypeStruct((m, LANES), dtype)
    idx, gates, rank, counts = pl.pallas_call(
        _router_kernel,
        grid=(m // tm,),
        in_specs=[pl.BlockSpec((tm, k), lambda i: (i, 0)), pl.BlockSpec((k, LANES), lambda i: (0, 0)), fixed],
        out_specs=[row, row, row, fixed],
        out_shape=[per_token(jnp.int32), per_token(F32), per_token(jnp.int32),
                   jax.ShapeDtypeStruct((1, LANES), F32)],
        compiler_params=pltpu.CompilerParams(
            dimension_semantics=("arbitrary",), vmem_limit_bytes=VMEM_LIMIT),
        name="router",
    )(x, w, b)
    return (idx[:, :TOP_K], gates[:, :TOP_K], rank[:, :TOP_K],
            counts[0, :N_EXPERTS].astype(jnp.int32))


def _hgrn_kernel(q_ref, z_ref, v_ref, lb_ref, *rest, reverse, n_chunks):
    st_ref = rest[-1]

    @pl.when(pl.program_id(2) == 0)
    def _():
        st_ref[...] = jnp.zeros_like(st_ref)

    lb = lb_ref[...]
    f = lb + (1.0 - lb) * jax.nn.sigmoid(z_ref[...])
    lf = jnp.log(f)
    sub_total = jnp.sum(lf.reshape(lf.shape[0] // SUB, SUB, HG_D), axis=1)
    factorable = jnp.min(sub_total) > -DIAG_LOG_RANGE
    args = (q_ref, f, lf, v_ref) + rest

    @pl.when(factorable)
    def _():
        _hgrn_block(*args, reverse=reverse, n_chunks=n_chunks, pairwise=False)

    @pl.when(jnp.logical_not(factorable))
    def _():
        _hgrn_block(*args, reverse=reverse, n_chunks=n_chunks, pairwise=True)


def _hgrn_block(q_ref, f_blk, lf_blk, v_ref, *rest, reverse, n_chunks, pairwise):
    if reverse:
        of_ref, og_ref, nw_ref, o_ref, st_ref = rest
    else:
        o_ref, st_ref = rest
    c = CHUNK
    row = lax.broadcasted_iota(jnp.int32, (c, c), 0)
    col = lax.broadcasted_iota(jnp.int32, (c, c), 1)
    causal = (col >= row) if reverse else (col <= row)
    tri = jnp.where(causal, 1.0, 0.0).astype(F32)
    diag_mask = causal & (row // SUB == col // SUB)
    srow = lax.broadcasted_iota(jnp.int32, (SUB, 1), 0)

    splits = []
    size = c
    while size > SUB:
        for lo in range(0, c, size):
            splits.append((lo, lo + size // 2, lo + size))
        size //= 2

    order = list(range(n_chunks - 1, -1, -1) if reverse else range(n_chunks))
    chunks = [dict(rows=slice(ci * c, (ci + 1) * c)) for ci in order]
    for ch in chunks:
        rows = ch["rows"]
        ch.update(lf=lf_blk[rows], kk=1.0 - f_blk[rows], q=q_ref[rows, :], v=v_ref[rows, :])
        ch["b"] = _dot_f32(tri, ch["lf"])
    for ch in chunks:
        q, kk, v, b, lf = ch["q"], ch["kk"], ch["v"], ch["b"], ch["lf"]
        b_end = b[0:1, :] if reverse else b[c - 1:c, :]
        ch["qdec"] = (q * jnp.exp(b)).astype(BF16)
        ch["dec"] = jnp.exp(b_end)
        ch["kv"] = _dot_tn(v, kk * jnp.exp(b_end - b))
        scores = []
        for lo, mid, hi in splits:
            if reverse:
                anc = b[mid:mid + 1, :]
                qs, ks = slice(lo, mid), slice(mid, hi)
            else:
                anc = b[mid - 1:mid, :]
                qs, ks = slice(mid, hi), slice(lo, mid)
            scores.append((qs, ks, _dot_nt(q[qs] * jnp.exp(b[qs] - anc), kk[ks] * jnp.exp(anc - b[ks]))))
        ch["scores"] = scores
        if not pairwise:
            edge = b - lf
            anc = jnp.concatenate(
                [jnp.broadcast_to(edge[d0 + SUB - 1:d0 + SUB] if reverse else edge[d0:d0 + 1], (SUB, HG_D))
                 for d0 in range(0, c, SUB)], axis=0)
            ch["diag"] = jnp.where(diag_mask, _dot_nt(q * jnp.exp(b - anc), kk * jnp.exp(anc - b)), 0.0)
    for ch in chunks:
        v = ch["v"]
        parts = [None] * (c // SUB)
        for qs, ks, s in ch["scores"]:
            contrib = _dot(s, v[ks])
            for j in range(qs.start // SUB, qs.stop // SUB):
                piece = contrib[j * SUB - qs.start:(j + 1) * SUB - qs.start]
                parts[j] = piece if parts[j] is None else parts[j] + piece
        intra = jnp.concatenate([jnp.zeros((SUB, HG_D), F32) if p is None else p for p in parts], axis=0)
        if pairwise:
            accs = []
            for d0 in range(0, c, SUB):
                qb, bb, kb, vb = (t[d0:d0 + SUB] for t in (ch["q"], ch["b"], ch["kk"], v))
                acc = jnp.zeros((SUB, HG_D), F32)
                for s in range(SUB):
                    mask = (srow <= s) if reverse else (srow >= s)
                    e = jnp.where(mask, jnp.exp(jnp.minimum(bb - bb[s:s + 1, :], 0.0)), 0.0)
                    w = jnp.sum(qb * kb[s:s + 1, :] * e, axis=-1, keepdims=True)
                    acc = acc + w * vb[s:s + 1, :]
                accs.append(acc)
            ch["intra"] = intra + jnp.concatenate(accs, axis=0)
        else:
            ch["intra"] = intra + _dot(ch["diag"], v)
    st = st_ref[...]
    for ch in chunks:
        ch["st"] = st
        st = st * ch["dec"] + ch["kv"]
    st_ref[...] = st
    for ch in chunks:
        rows = ch["rows"]
        o = ch["intra"] + _dot_nt(ch["qdec"], ch["st"])
        if reverse:
            o = o + of_ref[rows, :]
            o = o * lax.rsqrt(jnp.mean(o * o, axis=-1, keepdims=True) + NORM_EPS) * nw_ref[...]
            o = o * jax.nn.silu(og_ref[rows, :])
        o_ref[rows, :] = o


def _hgrn_scan(p_hg, lb, *, bsz, seq, o_fwd=None, norm_w=None, tb=1024):
    reverse = o_fwd is not None
    t = bsz * seq
    tb = min(tb, seq)
    nblk = seq // tb
    h = HG_HEADS
    zoff = 2 * h if reverse else h

    def col(off):
        return pl.BlockSpec((tb, HG_D),
                            lambda b, hh, i: (b * nblk + (nblk - 1 - i if reverse else i), off + hh))

    vec = pl.BlockSpec((1, HG_D), lambda b, hh, i: (0, 0))
    in_specs = [col(0), col(zoff), col(3 * h), pl.BlockSpec((1, HG_D), lambda b, hh, i: (0, hh))]
    args = [p_hg, p_hg, p_hg, lb.reshape(1, -1)]
    if reverse:
        in_specs += [col(0), col(4 * h), vec]
        args += [o_fwd, p_hg, norm_w.reshape(1, -1)]
    return pl.pallas_call(
        functools.partial(_hgrn_kernel, reverse=reverse, n_chunks=tb // CHUNK),
        grid=(bsz, h, nblk),
        in_specs=in_specs,
        out_specs=col(0),
        out_shape=jax.ShapeDtypeStruct((t, h * HG_D), F32),
        scratch_shapes=[pltpu.VMEM((HG_D, HG_D), F32)],
        compiler_params=pltpu.CompilerParams(
            dimension_semantics=("arbitrary", "arbitrary", "arbitrary"),
            vmem_limit_bytes=VMEM_LIMIT),
        name="hgrn_bwd" if reverse else "hgrn_fwd",
    )(*args)


class _RwkvMasks:
    def __init__(self, reverse):
        c, n = CHUNK, RW_LANES
        m = (n // RW_HEAD) * c
        row = lax.broadcasted_iota(jnp.int32, (c, c), 0)
        col = lax.broadcasted_iota(jnp.int32, (c, c), 1)
        self.tri = jnp.where((col >= row) if reverse else (col <= row), 1.0, 0.0).astype(F32)
        self.stack = (lax.broadcasted_iota(jnp.int32, (m, n), 0) // c
                      == lax.broadcasted_iota(jnp.int32, (m, n), 1) // RW_HEAD)
        tr = lax.broadcasted_iota(jnp.int32, (c, m), 0)
        tc = lax.broadcasted_iota(jnp.int32, (c, m), 1) % c
        self.strict = (tc > tr) if reverse else (tc < tr)
        self.incl = (tc >= tr) if reverse else (tc <= tr)
        self.eye = jnp.where(tc == tr, 1.0, 0.0).astype(F32)
        self.diag = (lax.broadcasted_iota(jnp.int32, (n, n), 0) // RW_HEAD
                     == lax.broadcasted_iota(jnp.int32, (n, n), 1) // RW_HEAD)


def _rwkv_stack(t, masks):
    return jnp.where(masks.stack, jnp.concatenate([t] * (RW_LANES // RW_HEAD), axis=0), 0.0).astype(BF16)


def _lockstep(gens):
    gens = list(gens)
    while gens:
        alive = []
        for gen in gens:
            try:
                next(gen)
                alive.append(gen)
            except StopIteration:
                pass
        gens = alive


def _rwkv_prepare(p, refs, rows, lanes, masks, reverse):
    c = CHUNK
    m = (RW_LANES // RW_HEAD) * c
    stack = functools.partial(_rwkv_stack, masks=masks)
    r, k, v, kk, a, lw = (ref[rows, lanes] for ref in refs)
    lp = _dot_f32(masks.tri, lw)
    yield
    lp_end = lp[0:1, :] if reverse else lp[c - 1:c, :]
    pinv = jnp.exp(-lp)
    dec = jnp.exp(lp_end - lp)
    kb = kk * a
    ar = jnp.concatenate([-kk * jnp.exp(lp - lw), r * jnp.exp(lp)], axis=0).astype(BF16)
    bk = jnp.concatenate([stack(kb * pinv), stack(k * pinv)], axis=0)
    v_s = stack(v)
    sc = _dot_nt(ar, bk)
    yield
    ab = jnp.where(masks.strict, sc[:c, :m], 0.0)
    akrk = jnp.concatenate([jnp.where(masks.strict, sc[:c, m:], 0.0),
                            jnp.where(masks.incl, sc[c:, m:], 0.0)], axis=0)
    p.update(ar=ar, v=v.astype(BF16), akrk_v=_dot(akrk, v_s),
             rb=jnp.where(masks.incl, sc[c:, :m], 0.0).astype(BF16),
             pw=_dot(ab, stack(ab)), tinv=masks.eye + ab,
             bkp=jnp.concatenate([kb * dec, k * dec], axis=0).astype(BF16),
             gdec=jnp.exp(lp_end))


def _rwkv_double(p, masks, last):
    c = CHUNK
    pw_s = _rwkv_stack(p["pw"], masks)
    if last:
        p["tinv"] = p["tinv"] + _dot(p["tinv"], pw_s)
    else:
        z = _dot(jnp.concatenate([p["pw"], p["tinv"]], axis=0), pw_s)
        p["pw"], p["tinv"] = z[:c], p["tinv"] + z[c:]


def _rwkv_chain(preps, g_ref, y_ref, rows_seq, lanes, masks):
    c = CHUNK
    for p, rows in zip(preps, rows_seq):
        g = g_ref[...]
        arg = _dot_nt(p["ar"], g)
        yield
        u = _dot(p["tinv"], _rwkv_stack(arg[:c] + p["akrk_v"][:c], masks))
        yield
        y_ref[rows, lanes] = arg[c:] + p["akrk_v"][c:] + _dot(p["rb"], _rwkv_stack(u, masks))
        uv = jnp.concatenate([u.astype(BF16), p["v"]], axis=0)
        g_ref[...] = g * p["gdec"] + jnp.where(masks.diag, _dot_tn(uv, p["bkp"]), 0.0)
        yield


def _rwkv_kernel(*refs, n_chunks):
    in_f, in_b = refs[0:6], refs[6:12]
    y_f, y_b, g_f, g_b = refs[12:16]

    @pl.when(pl.program_id(2) == 0)
    def _():
        g_f[...] = jnp.zeros_like(g_f)
        g_b[...] = jnp.zeros_like(g_b)

    c = CHUNK
    rows = [slice(ci * c, (ci + 1) * c) for ci in range(n_chunks)]
    masks = (_RwkvMasks(False), _RwkvMasks(True))
    plan = []
    for gi in range(RW_GROUPS):
        lanes = slice(gi * RW_LANES, (gi + 1) * RW_LANES)
        plan.append((in_f, y_f, g_f.at[gi], masks[0], False, rows, lanes))
        plan.append((in_b, y_b, g_b.at[gi], masks[1], True, rows[::-1], lanes))
    preps = [[{} for _ in rows] for _ in plan]
    _lockstep(_rwkv_prepare(p, ins, rw, lanes, mk, rev)
              for (ins, _, _, mk, rev, rws, lanes), plist in zip(plan, preps)
              for p, rw in zip(plist, rws))
    n_double = c.bit_length() - 2
    for it in range(n_double):
        for chain, plist in zip(plan, preps):
            for p in plist:
                _rwkv_double(p, chain[3], last=it == n_double - 1)
    _lockstep(_rwkv_chain(plist, g_ref, y_ref, rws, lanes, mk)
              for (_, y_ref, g_ref, mk, _, rws, lanes), plist in zip(plan, preps))


def _rwkv_scan(r, k, v, kk, a, lw_f, lw_b, *, bsz, seq, tb=256):
    t = bsz * seq
    tb = min(tb, seq)
    nblk = seq // tb
    n = RW_LANES * RW_GROUPS
    spec_f = pl.BlockSpec((tb, n), lambda b, hh, i: (b * nblk + i, hh))
    spec_b = pl.BlockSpec((tb, n), lambda b, hh, i: (b * nblk + nblk - 1 - i, hh))
    out = jax.ShapeDtypeStruct((t, RW_WIDTH), F32)
    state = pltpu.VMEM((RW_GROUPS, RW_LANES, RW_LANES), F32)
    return pl.pallas_call(
        functools.partial(_rwkv_kernel, n_chunks=tb // CHUNK),
        grid=(bsz, RW_WIDTH // n, nblk),
        in_specs=[spec_f] * 6 + [spec_b] * 6,
        out_specs=[spec_f, spec_b],
        out_shape=[out, out],
        scratch_shapes=[state, state],
        compiler_params=pltpu.CompilerParams(
            dimension_semantics=("arbitrary", "arbitrary", "arbitrary"),
            vmem_limit_bytes=VMEM_LIMIT),
        name="rwkv",
    )(r, k, v, kk, a, lw_f, r, k, v, kk, a, lw_b)


LORA_COLS = 384


def _rwkv_prep_kernel(ps_ref, vf_ref, wup_ref, aup_ref, gup_ref, vup_ref, vec_ref, seg_ref,
                      lwf_ref, lwb_ref, a_ref, k2_ref, kk_ref, v2_ref, g_ref, *, mix):
    w = RW_WIDTH
    k = ps_ref[:, w:2 * w]
    v = ps_ref[:, 2 * w:3 * w]
    lora = ps_ref[:, 3 * w:3 * w + LORA_COLS]
    vec = vec_ref[...]
    for d, out_ref in ((0, lwf_ref), (1, lwb_ref)):
        z = vec[d:d + 1] + _dot(jnp.tanh(lora[:, d * DECAY_LORA:(d + 1) * DECAY_LORA]), wup_ref[d])
        softplus_neg = jnp.maximum(-z, 0.0) + jnp.log(1.0 + jnp.exp(-jnp.abs(z)))
        out_ref[...] = -jnp.exp(-softplus_neg - 0.5)
    off = 2 * DECAY_LORA
    a = jax.nn.sigmoid(vec[2:3] + _dot(lora[:, off:off + AAA_LORA], aup_ref[...]))
    off += AAA_LORA
    g_ref[...] = _dot(jax.nn.sigmoid(lora[:, off:off + GATE_LORA]), gup_ref[...])
    off += GATE_LORA
    if mix:
        v = v + (vf_ref[...] - v) * jax.nn.sigmoid(vec[3:4] + _dot(lora[:, off:LORA_COLS], vup_ref[...]))
    v2_ref[...] = v
    kx = k * vec[4:5]
    sq = kx * kx
    sq_hi = sq.astype(BF16)
    sq_lo = (sq - sq_hi.astype(F32)).astype(BF16)
    norm2 = _dot(sq_hi, seg_ref[...]) + _dot(sq_lo, seg_ref[...])
    kk_ref[...] = kx / jnp.maximum(jnp.sqrt(norm2), 1e-12)
    k2_ref[...] = k * (1.0 + (a - 1.0) * vec[5:6])
    a_ref[...] = a


def _rwkv_prep(ps, v_first, w_up, a_up, g_up, v_up, vec, *, tm=256):
    t = ps.shape[0]
    w = RW_WIDTH
    mix = v_first is not None
    if not mix:
        v_first = ps
        v_up = jnp.zeros((LORA_COLS - 2 * DECAY_LORA - AAA_LORA - GATE_LORA, w), F32)
    head = jnp.arange(w, dtype=jnp.int32) // RW_HEAD
    seg = (head[:, None] == head[None, :]).astype(BF16)
    row = lambda i: (i, 0)
    fixed2 = lambda i: (0, 0)
    full = lambda arr: pl.BlockSpec(arr.shape, (lambda i: (0, 0, 0)) if arr.ndim == 3 else fixed2)
    out = jax.ShapeDtypeStruct((t, w), F32)
    return pl.pallas_call(
        functools.partial(_rwkv_prep_kernel, mix=mix),
        grid=(t // tm,),
        in_specs=[pl.BlockSpec((tm, ps.shape[1]), row), pl.BlockSpec((tm, w), row),
                  full(w_up), full(a_up), full(g_up), full(v_up), full(vec), full(seg)],
        out_specs=[pl.BlockSpec((tm, w), row)] * 7,
        out_shape=[out] * 7,
        compiler_params=pltpu.CompilerParams(
            dimension_semantics=("arbitrary",), vmem_limit_bytes=VMEM_LIMIT),
        name="rwkv_prep",
    )(ps, v_first, w_up, a_up, g_up, v_up, vec, seg)


PAIR = 256


def _moe_kernel(be_ref, xs_ref, w1_ref, b1_ref, w2_ref, b2_ref, o_ref, w1p_ref, w2b_ref):
    i = pl.program_id(0)
    f2 = w1_ref.shape[-1]
    half = PAIR // 2

    @pl.when((i == 0) | (be_ref[i] != be_ref[jnp.maximum(i - 1, 0)]))
    def _():
        src = lax.broadcasted_iota(jnp.int32, (PAIR, PAIR), 0)
        dst = lax.broadcasted_iota(jnp.int32, (PAIR, PAIR), 1)
        perm = jnp.where(src == jnp.where(dst < half, 2 * dst, 2 * (dst - half) + 1), 1.0, 0.0)
        for j in range(0, f2, PAIR):
            w1p_ref[:, j:j + PAIR] = _dot(w1_ref[:, j:j + PAIR], perm).astype(BF16)
        w2b_ref[...] = w2_ref[...].astype(BF16)

    h = _dot(xs_ref[...], w1p_ref[...]) + b1_ref[...]
    acts = []
    for j in range(0, f2, PAIR):
        glu = jnp.minimum(h[:, j:j + half], SWIGLU_LIMIT)
        lin = jnp.clip(h[:, j + half:j + PAIR], -SWIGLU_LIMIT, SWIGLU_LIMIT)
        acts.append((glu * jax.nn.sigmoid(SWIGLU_ALPHA * glu) * (lin + 1.0)).astype(BF16))
    o_ref[...] = _dot(jnp.concatenate(acts, axis=1), w2b_ref[...]) + b2_ref[...]


def _moe_experts(block_exp, xs, w1, b1, w2, b2, *, layer):
    n_slots, d = xs.shape
    f2 = w1.shape[-1]
    f = f2 // 2
    blk = MOE_BLOCK
    n_blocks = n_slots // blk
    b1p = b1.reshape(DEPTH, N_EXPERTS, f2 // PAIR, PAIR // 2, 2).swapaxes(-1, -2)
    b1p = b1p.reshape(DEPTH, N_EXPERTS, 1, f2)
    wspec = lambda shape: pl.BlockSpec((None, None) + shape, lambda i, be: (layer, be[i], 0, 0))
    return pl.pallas_call(
        _moe_kernel,
        grid_spec=pltpu.PrefetchScalarGridSpec(
            num_scalar_prefetch=1,
            grid=(n_blocks,),
            in_specs=[pl.BlockSpec((blk, d), lambda i, be: (i, 0)),
                      wspec((d, f2)), wspec((1, f2)), wspec((f, d)), wspec((1, d))],
            out_specs=pl.BlockSpec((blk, d), lambda i, be: (i, 0)),
            scratch_shapes=[pltpu.VMEM((d, f2), BF16), pltpu.VMEM((f, d), BF16)]),
        out_shape=jax.ShapeDtypeStruct((n_slots, d), F32),
        compiler_params=pltpu.CompilerParams(
            dimension_semantics=("arbitrary",), vmem_limit_bytes=VMEM_LIMIT),
        name="moe_experts",
    )(block_exp, xs, w1, b1p, w2, b2.reshape(DEPTH, N_EXPERTS, 1, d))


def _ln_rows(y, g, b):
    yc = y - jnp.mean(y, axis=-1, keepdims=True)
    var = jnp.mean(yc * yc, axis=-1, keepdims=True)
    return yc * lax.rsqrt(var + NORM_EPS) * g + b


def _merge_ln_kernel(oa_ref, ob_ref, gate_ref, x_ref, pa_ref, pb_ref, wo_ref, g_ref, b_ref, o_ref):
    d = x_ref.shape[-1]
    merged = (jax.nn.sigmoid(gate_ref[:, :d]) * _dot(oa_ref[...], pa_ref[...])
              + jax.nn.sigmoid(gate_ref[:, d:]) * _dot(ob_ref[...], pb_ref[...]))
    o_ref[...] = _ln_rows(DN_ALPHA * x_ref[...] + _dot(merged, wo_ref[...]), g_ref[...], b_ref[...])


def _merge_ln(o_a, o_b, p_gate, x, proj_a, proj_b, w_out, g, b, *, tm=512):
    t, d = x.shape
    row = lambda i: (i, 0)
    fixed = lambda i: (0, 0)
    wspec = pl.BlockSpec((d, d), fixed)
    return pl.pallas_call(
        _merge_ln_kernel,
        grid=(t // tm,),
        in_specs=[pl.BlockSpec((tm, d), row), pl.BlockSpec((tm, d), row), pl.BlockSpec((tm, 2 * d), row),
                  pl.BlockSpec((tm, d), row), wspec, wspec, wspec,
                  pl.BlockSpec((1, d), fixed), pl.BlockSpec((1, d), fixed)],
        out_specs=pl.BlockSpec((tm, d), row),
        out_shape=jax.ShapeDtypeStruct((t, d), F32),
        compiler_params=pltpu.CompilerParams(
            dimension_semantics=("arbitrary",), vmem_limit_bytes=VMEM_LIMIT),
        name="merge_ln",
    )(o_a, o_b, p_gate, x, proj_a.astype(BF16), proj_b.astype(BF16), w_out.astype(BF16),
      g.reshape(1, d), b.reshape(1, d))


def _combine_ln_kernel(x_ref, yg_ref, gate_ref, g_ref, b_ref, o_ref, ob_ref):
    gate = gate_ref[...]
    moe = gate[:, 0:1] * yg_ref[0]
    for kk in range(1, TOP_K):
        moe = moe + gate[:, kk:kk + 1] * yg_ref[kk]
    out = _ln_rows(DN_ALPHA * x_ref[...] + moe, g_ref[...], b_ref[...])
    o_ref[...] = out
    ob_ref[...] = out.astype(BF16)


def _combine_ln(x, yg, gates, g, b, *, tm=512):
    t, d = x.shape
    row = lambda i: (i, 0)
    fixed = lambda i: (0, 0)
    return pl.pallas_call(
        _combine_ln_kernel,
        grid=(t // tm,),
        in_specs=[pl.BlockSpec((tm, d), row), pl.BlockSpec((TOP_K, tm, d), lambda i: (0, i, 0)),
                  pl.BlockSpec((tm, TOP_K), row),
                  pl.BlockSpec((1, d), fixed), pl.BlockSpec((1, d), fixed)],
        out_specs=[pl.BlockSpec((tm, d), row), pl.BlockSpec((tm, d), row)],
        out_shape=[jax.ShapeDtypeStruct((t, d), F32), jax.ShapeDtypeStruct((t, d), BF16)],
        compiler_params=pltpu.CompilerParams(
            dimension_semantics=("arbitrary",), vmem_limit_bytes=VMEM_LIMIT),
        name="combine_ln",
    )(x, yg, gates, g.reshape(1, d), b.reshape(1, d))


def _rw_heads(t):
    return t.reshape(t.shape[0], RW_HEADS, RW_HEAD)


def _moe(x2, router_w, router_b, w1, b1, w2, b2, *, layer):
    t, d = x2.shape
    n_assign = t * TOP_K
    n_blocks = -(-n_assign // MOE_BLOCK) + N_EXPERTS
    n_slots = n_blocks * MOE_BLOCK
    top_idx, gates, rank, counts = _router(x2, router_w, router_b)
    e_flat = top_idx.reshape(-1)
    order = jnp.argsort(e_flat).astype(jnp.int32)
    experts = jnp.arange(N_EXPERTS, dtype=jnp.int32)
    starts = jnp.cumsum(counts) - counts
    padded = (counts + MOE_BLOCK - 1) // MOE_BLOCK * MOE_BLOCK
    pad_ends = jnp.cumsum(padded)
    pad_starts = pad_ends - padded
    shift = pad_starts - starts
    slot_of = rank.reshape(-1) + jnp.sum(jnp.where(e_flat[:, None] == experts, pad_starts, 0), axis=1)
    block_start = jnp.arange(n_blocks, dtype=jnp.int32) * MOE_BLOCK
    block_exp = jnp.minimum(jnp.sum(pad_ends[None, :] <= block_start[:, None], axis=1),
                            N_EXPERTS - 1).astype(jnp.int32)
    blk_hot = block_exp[:, None] == experts
    per_slot = lambda tab: jnp.repeat(jnp.sum(jnp.where(blk_hot, tab, 0), axis=1), MOE_BLOCK)
    slot = jnp.arange(n_slots, dtype=jnp.int32)
    valid = slot - per_slot(pad_starts) < per_slot(counts)
    src = jnp.clip(slot - per_slot(shift), 0, n_assign - 1)
    slot_tok = jnp.where(valid, order[src] // TOP_K, 0)
    y = _moe_experts(block_exp, x2[slot_tok], w1, b1, w2, b2, layer=layer)
    return y[slot_of.reshape(t, TOP_K).T.reshape(-1)].reshape(TOP_K, t, d), gates


def kernel(x, w_in, hg_lb_logits, hg_norm_w, rw_mu, rw_w0, rw_w_up, rw_a0, rw_a_up, rw_g_up,
           rw_k_k, rw_k_a, rw_r_k, rw_lnx_w, rw_lnx_b, rw_v_down, rw_v_up, rw_v0, proj_a, proj_b,
           w_out, ln1_g, ln1_b, router_w, router_b, moe_w1, moe_b1, moe_w2, moe_b2, ln2_g, ln2_b):
    bsz, s, d = x.shape
    t = bsz * s
    scan = dict(bsz=bsz, seq=s)
    lb_all = jnp.cumsum(jax.nn.softmax(hg_lb_logits.astype(F32), axis=0), axis=0)
    lb_all = lb_all - lb_all[0:1]
    hg_cols = 5 * HG_WIDTH
    rw_cols = 3 * RW_WIDTH + 2 * DECAY_LORA + AAA_LORA + GATE_LORA
    x2 = x.reshape(t, d)
    xb = x2.astype(BF16)
    v_first = None
    for l in range(DEPTH):
        w_l = w_in[l].astype(BF16)
        p_hg = _matmul(xb, w_l[:, :hg_cols], tm=1024, tn=2560)
        p_gate = _matmul(xb, w_l[:, hg_cols + rw_cols:], tm=1024, tn=2048)
        w_rw = w_in[l][:, hg_cols:hg_cols + rw_cols]
        w_self, w_nb = w_rw * (1.0 - rw_mu[l]), w_rw * rw_mu[l]
        if l > 0:
            w_self = jnp.concatenate([w_self, rw_v_down[l - 1]], axis=1)
        pad_cols = lambda w: jnp.pad(w, ((0, 0), (0, 3456 - w.shape[1]))).astype(BF16)
        x3 = jnp.pad(x2.reshape(bsz, s, d), ((0, 0), (1, 1), (0, 0)))
        x_nb = (0.5 * (x3[:, :-2] + x3[:, 2:])).reshape(t, d).astype(BF16)
        ps = _matmul2(xb, x_nb, pad_cols(w_self), pad_cols(w_nb), tm=2048)

        o_a = _hgrn_scan(p_hg, lb_all[l], o_fwd=_hgrn_scan(p_hg, lb_all[l], **scan),
                         norm_w=hg_norm_w[l], **scan)

        r = ps[:, :RW_WIDTH]
        zeros = jnp.zeros((RW_WIDTH,), F32)
        if l == 0:
            v_first = ps[:, 2 * RW_WIDTH:3 * RW_WIDTH]
            mixing = dict(v_first=None, v_up=None)
            v0 = zeros
        else:
            v_up = rw_v_up[l - 1]
            mixing = dict(v_first=v_first, v_up=jnp.pad(v_up, ((0, 3456 - rw_cols - v_up.shape[0]), (0, 0))))
            v0 = rw_v0[l - 1]
        vec = jnp.stack([rw_w0[l, 0], rw_w0[l, 1], rw_a0[l], v0, rw_k_k[l], rw_k_a[l], zeros, zeros])
        lw_f, lw_b, a, k, kk, v, g = _rwkv_prep(ps, w_up=rw_w_up[l], a_up=rw_a_up[l], g_up=rw_g_up[l],
                                                vec=vec, **mixing)
        y_f, y_b = _rwkv_scan(ps, k, v, kk, a, lw_f, lw_b, **scan)
        y = _rw_heads(y_f + y_b)
        yc = y - jnp.mean(y, axis=-1, keepdims=True)
        yn = yc * lax.rsqrt(jnp.mean(yc * yc, axis=-1, keepdims=True) + RW_LN_EPS)
        yn = yn * _rw_heads(rw_lnx_w[l][None])[0] + _rw_heads(rw_lnx_b[l][None])[0]
        bonus = jnp.sum(_rw_heads(r) * _rw_heads(k) * rw_r_k[l], axis=-1, keepdims=True) * _rw_heads(v)
        o_b = (yn + bonus).reshape(t, RW_WIDTH) * g

        x2 = _merge_ln(o_a, o_b, p_gate, x2, proj_a[l], proj_b[l], w_out[l], ln1_g[l], ln1_b[l])
        yg, gates = _moe(x2, router_w[l], router_b[l], moe_w1, moe_b1, moe_w2, moe_b2, layer=l)
        x2, xb = _combine_ln(x2, yg, gates, ln2_g[l], ln2_b[l])
    return x2.reshape(bsz, s, d)
```

```python
import functools

import jax
import jax.numpy as jnp
from jax import lax
from jax.experimental import pallas as pl
from jax.experimental.pallas import tpu as pltpu

F32 = jnp.float32
BF16 = jnp.bfloat16

D_MODEL = 1024
DEPTH = 4
HG_HEADS = 8
HG_D = 128
HG_WIDTH = HG_HEADS * HG_D
RW_HEAD = 64
RW_HEADS = D_MODEL // RW_HEAD
RW_WIDTH = D_MODEL
DECAY_LORA = 64
AAA_LORA = 64
GATE_LORA = 128
N_EXPERTS = 32
TOP_K = 4
MOE_BLOCK = 512
SWIGLU_ALPHA = 1.702
SWIGLU_LIMIT = 7.0
NORM_EPS = 1e-5
RW_LN_EPS = 64e-5
DN_ALPHA = (2 * DEPTH) ** 0.25

CHUNK = 64
SUB = 16
DIAG_LOG_RANGE = 60.0
RW_LANES = 256
RW_GROUPS = 2
VMEM_LIMIT = 56 * 1024 * 1024

_NT = (((1,), (1,)), ((), ()))
_TN = (((0,), (0,)), ((), ()))


def _dot(a, b):
    return jnp.dot(a.astype(BF16), b.astype(BF16), preferred_element_type=F32)


def _dot_nt(a, b):
    return lax.dot_general(a.astype(BF16), b.astype(BF16), _NT, preferred_element_type=F32)


def _dot_tn(a, b):
    return lax.dot_general(a.astype(BF16), b.astype(BF16), _TN, preferred_element_type=F32)


def _dot_f32(a, b):
    return jnp.dot(a, b, preferred_element_type=F32, precision=lax.Precision.HIGHEST)


def _mm_kernel(x_ref, w_ref, o_ref):
    o_ref[...] = _dot(x_ref[...], w_ref[...]).astype(o_ref.dtype)


def _matmul(x, w, *, tm=512, tn=512, out_dtype=F32):
    m, k = x.shape
    n = w.shape[1]
    tm = min(tm, m)
    tn = min(tn, n)
    assert m % tm == 0 and n % tn == 0
    return pl.pallas_call(
        _mm_kernel,
        grid=(n // tn, m // tm),
        in_specs=[pl.BlockSpec((tm, k), lambda j, i: (i, 0)),
                  pl.BlockSpec((k, tn), lambda j, i: (0, j))],
        out_specs=pl.BlockSpec((tm, tn), lambda j, i: (i, j)),
        out_shape=jax.ShapeDtypeStruct((m, n), out_dtype),
        compiler_params=pltpu.CompilerParams(
            dimension_semantics=("arbitrary", "arbitrary"), vmem_limit_bytes=VMEM_LIMIT),
        name="matmul",
    )(x, w)


def _mm2_kernel(x_ref, y_ref, wx_ref, wy_ref, o_ref):
    o_ref[...] = _dot(x_ref[...], wx_ref[...]) + _dot(y_ref[...], wy_ref[...])


def _matmul2(x, y, wx, wy, *, tm=1024, tn=1152):
    m, k = x.shape
    n = wx.shape[1]
    tm = min(tm, m)
    assert m % tm == 0 and n % tn == 0
    xspec = pl.BlockSpec((tm, k), lambda j, i: (i, 0))
    wspec = pl.BlockSpec((k, tn), lambda j, i: (0, j))
    return pl.pallas_call(
        _mm2_kernel,
        grid=(n // tn, m // tm),
        in_specs=[xspec, xspec, wspec, wspec],
        out_specs=pl.BlockSpec((tm, tn), lambda j, i: (i, j)),
        out_shape=jax.ShapeDtypeStruct((m, n), F32),
        compiler_params=pltpu.CompilerParams(
            dimension_semantics=("arbitrary", "arbitrary"), vmem_limit_bytes=VMEM_LIMIT),
        name="matmul2",
    )(x, y, wx, wy)


LANES = 128
NEG_BIG = -3.0e38


def _router_kernel(x_ref, w_ref, b_ref, idx_ref, gate_ref, rank_ref, count_ref):
    @pl.when(pl.program_id(0) == 0)
    def _():
        count_ref[...] = jnp.zeros_like(count_ref)

    logits = _dot_f32(x_ref[...], w_ref[...]) + b_ref[...]
    tm = logits.shape[0]
    lane = lax.broadcasted_iota(jnp.int32, logits.shape, 1).astype(F32)
    cur = logits
    vals, idxs = [], []
    for _ in range(TOP_K):
        top = jnp.max(cur, axis=-1, keepdims=True)
        idx = jnp.min(jnp.where(cur == top, lane, float(LANES)), axis=-1, keepdims=True)
        vals.append(top)
        idxs.append(idx)
        cur = jnp.where(lane == idx, NEG_BIG, cur)
    exps = [jnp.exp(v - vals[0]) for v in vals]
    total = exps[0]
    for e in exps[1:]:
        total = total + e
    chosen = jnp.zeros_like(logits)
    for idx in idxs:
        chosen = jnp.where(lane == idx, 1.0, chosen)
    earlier = (lax.broadcasted_iota(jnp.int32, (tm, tm), 1)
               < lax.broadcasted_iota(jnp.int32, (tm, tm), 0))
    before = count_ref[...] + _dot(jnp.where(earlier, 1.0, 0.0), chosen)
    count_ref[...] = count_ref[...] + jnp.sum(chosen, axis=0, keepdims=True)
    idx_out = jnp.zeros_like(logits)
    gate_out = jnp.zeros_like(logits)
    rank_out = jnp.zeros_like(logits)
    for j in range(TOP_K):
        rank = jnp.sum(jnp.where(lane == idxs[j], before, 0.0), axis=-1, keepdims=True)
        idx_out = jnp.where(lane == float(j), idxs[j], idx_out)
        gate_out = jnp.where(lane == float(j), exps[j] / total, gate_out)
        rank_out = jnp.where(lane == float(j), rank, rank_out)
    idx_ref[...] = idx_out.astype(jnp.int32)
    gate_ref[...] = gate_out
    rank_ref[...] = rank_out.astype(jnp.int32)


def _router(x, w, b, *, tm=512):
    m, k = x.shape
    pad = LANES - N_EXPERTS
    w = jnp.pad(w, ((0, 0), (0, pad)))
    b = jnp.pad(b, (0, pad), constant_values=NEG_BIG).reshape(1, LANES)
    row = pl.BlockSpec((tm, LANES), lambda i: (i, 0))
    fixed = pl.BlockSpec((1, LANES), lambda i: (0, 0))
    per_token = lambda dtype: jax.ShapeDtypeStruct((m, LANES), dtype)
    idx, gates, rank, counts = pl.pallas_call(
        _router_kernel,
        grid=(m // tm,),
        in_specs=[pl.BlockSpec((tm, k), lambda i: (i, 0)), pl.BlockSpec((k, LANES), lambda i: (0, 0)), fixed],
        out_specs=[row, row, row, fixed],
        out_shape=[per_token(jnp.int32), per_token(F32), per_token(jnp.int32),
                   jax.ShapeDtypeStruct((1, LANES), F32)],
        compiler_params=pltpu.CompilerParams(
            dimension_semantics=("arbitrary",), vmem_limit_bytes=VMEM_LIMIT),
        name="router",
    )(x, w, b)
    return (idx[:, :TOP_K], gates[:, :TOP_K], rank[:, :TOP_K],
            counts[0, :N_EXPERTS].astype(jnp.int32))


def _hgrn_kernel(q_ref, z_ref, v_ref, lb_ref, *rest, reverse, n_chunks):
    st_ref = rest[-1]

    @pl.when(pl.program_id(2) == 0)
    def _():
        st_ref[...] = jnp.zeros_like(st_ref)

    lb = lb_ref[...]
    f = lb + (1.0 - lb) * jax.nn.sigmoid(z_ref[...])
    lf = jnp.log(f)
    sub_total = jnp.sum(lf.reshape(lf.shape[0] // SUB, SUB, HG_D), axis=1)
    factorable = jnp.min(sub_total) > -DIAG_LOG_RANGE
    args = (q_ref, f, lf, v_ref) + rest

    @pl.when(factorable)
    def _():
        _hgrn_block(*args, reverse=reverse, n_chunks=n_chunks, pairwise=False)

    @pl.when(jnp.logical_not(factorable))
    def _():
        _hgrn_block(*args, reverse=reverse, n_chunks=n_chunks, pairwise=True)


def _hgrn_block(q_ref, f_blk, lf_blk, v_ref, *rest, reverse, n_chunks, pairwise):
    if reverse:
        of_ref, og_ref, nw_ref, o_ref, st_ref = rest
    else:
        o_ref, st_ref = rest
    c = CHUNK
    row = lax.broadcasted_iota(jnp.int32, (c, c), 0)
    col = lax.broadcasted_iota(jnp.int32, (c, c), 1)
    causal = (col >= row) if reverse else (col <= row)
    tri = jnp.where(causal, 1.0, 0.0).astype(F32)
    diag_mask = causal & (row // SUB == col // SUB)
    srow = lax.broadcasted_iota(jnp.int32, (SUB, 1), 0)

    splits = []
    size = c
    while size > SUB:
        for lo in range(0, c, size):
            splits.append((lo, lo + size // 2, lo + size))
        size //= 2

    order = list(range(n_chunks - 1, -1, -1) if reverse else range(n_chunks))
    chunks = [dict(rows=slice(ci * c, (ci + 1) * c)) for ci in order]
    for ch in chunks:
        rows = ch["rows"]
        ch.update(lf=lf_blk[rows], kk=1.0 - f_blk[rows], q=q_ref[rows, :], v=v_ref[rows, :])
        ch["b"] = _dot_f32(tri, ch["lf"])
    for ch in chunks:
        q, kk, v, b, lf = ch["q"], ch["kk"], ch["v"], ch["b"], ch["lf"]
        b_end = b[0:1, :] if reverse else b[c - 1:c, :]
        ch["qdec"] = (q * jnp.exp(b)).astype(BF16)
        ch["dec"] = jnp.exp(b_end)
        ch["kv"] = _dot_tn(v, kk * jnp.exp(b_end - b))
        scores = []
        for lo, mid, hi in splits:
            if reverse:
                anc = b[mid:mid + 1, :]
                qs, ks = slice(lo, mid), slice(mid, hi)
            else:
                anc = b[mid - 1:mid, :]
                qs, ks = slice(mid, hi), slice(lo, mid)
            scores.append((qs, ks, _dot_nt(q[qs] * jnp.exp(b[qs] - anc), kk[ks] * jnp.exp(anc - b[ks]))))
        ch["scores"] = scores
        if not pairwise:
            edge = b - lf
            anc = jnp.concatenate(
                [jnp.broadcast_to(edge[d0 + SUB - 1:d0 + SUB] if reverse else edge[d0:d0 + 1], (SUB, HG_D))
                 for d0 in range(0, c, SUB)], axis=0)
            ch["diag"] = jnp.where(diag_mask, _dot_nt(q * jnp.exp(b - anc), kk * jnp.exp(anc - b)), 0.0)
    for ch in chunks:
        v = ch["v"]
        parts = [None] * (c // SUB)
        for qs, ks, s in ch["scores"]:
            contrib = _dot(s, v[ks])
            for j in range(qs.start // SUB, qs.stop // SUB):
                piece = contrib[j * SUB - qs.start:(j + 1) * SUB - qs.start]
                parts[j] = piece if parts[j] is None else parts[j] + piece
        intra = jnp.concatenate([jnp.zeros((SUB, HG_D), F32) if p is None else p for p in parts], axis=0)
        if pairwise:
            accs = []
            for d0 in range(0, c, SUB):
                qb, bb, kb, vb = (t[d0:d0 + SUB] for t in (ch["q"], ch["b"], ch["kk"], v))
                acc = jnp.zeros((SUB, HG_D), F32)
                for s in range(SUB):
                    mask = (srow <= s) if reverse else (srow >= s)
                    e = jnp.where(mask, jnp.exp(jnp.minimum(bb - bb[s:s + 1, :], 0.0)), 0.0)
                    w = jnp.sum(qb * kb[s:s + 1, :] * e, axis=-1, keepdims=True)
                    acc = acc + w * vb[s:s + 1, :]
                accs.append(acc)
            ch["intra"] = intra + jnp.concatenate(accs, axis=0)
        else:
            ch["intra"] = intra + _dot(ch["diag"], v)
    st = st_ref[...]
    for ch in chunks:
        ch["st"] = st
        st = st * ch["dec"] + ch["kv"]
    st_ref[...] = st
    for ch in chunks:
        rows = ch["rows"]
        o = ch["intra"] + _dot_nt(ch["qdec"], ch["st"])
        if reverse:
            o = o + of_ref[rows, :]
            o = o * lax.rsqrt(jnp.mean(o * o, axis=-1, keepdims=True) + NORM_EPS) * nw_ref[...]
            o = o * jax.nn.silu(og_ref[rows, :])
        o_ref[rows, :] = o


def _hgrn_scan(p_hg, lb, *, bsz, seq, o_fwd=None, norm_w=None, tb=1024):
    reverse = o_fwd is not None
    t = bsz * seq
    tb = min(tb, seq)
    nblk = seq // tb
    h = HG_HEADS
    zoff = 2 * h if reverse else h

    def col(off):
        return pl.BlockSpec((tb, HG_D),
                            lambda b, hh, i: (b * nblk + (nblk - 1 - i if reverse else i), off + hh))

    vec = pl.BlockSpec((1, HG_D), lambda b, hh, i: (0, 0))
    in_specs = [col(0), col(zoff), col(3 * h), pl.BlockSpec((1, HG_D), lambda b, hh, i: (0, hh))]
    args = [p_hg, p_hg, p_hg, lb.reshape(1, -1)]
    if reverse:
        in_specs += [col(0), col(4 * h), vec]
        args += [o_fwd, p_hg, norm_w.reshape(1, -1)]
    return pl.pallas_call(
        functools.partial(_hgrn_kernel, reverse=reverse, n_chunks=tb // CHUNK),
        grid=(bsz, h, nblk),
        in_specs=in_specs,
        out_specs=col(0),
        out_shape=jax.ShapeDtypeStruct((t, h * HG_D), F32),
        scratch_shapes=[pltpu.VMEM((HG_D, HG_D), F32)],
        compiler_params=pltpu.CompilerParams(
            dimension_semantics=("arbitrary", "arbitrary", "arbitrary"),
            vmem_limit_bytes=VMEM_LIMIT),
        name="hgrn_bwd" if reverse else "hgrn_fwd",
    )(*args)


class _RwkvMasks:
    def __init__(self, reverse):
        c, n = CHUNK, RW_LANES
        m = (n // RW_HEAD) * c
        row = lax.broadcasted_iota(jnp.int32, (c, c), 0)
        col = lax.broadcasted_iota(jnp.int32, (c, c), 1)
        self.tri = jnp.where((col >= row) if reverse else (col <= row), 1.0, 0.0).astype(F32)
        self.stack = (lax.broadcasted_iota(jnp.int32, (m, n), 0) // c
                      == lax.broadcasted_iota(jnp.int32, (m, n), 1) // RW_HEAD)
        tr = lax.broadcasted_iota(jnp.int32, (c, m), 0)
        tc = lax.broadcasted_iota(jnp.int32, (c, m), 1) % c
        self.strict = (tc > tr) if reverse else (tc < tr)
        self.incl = (tc >= tr) if reverse else (tc <= tr)
        self.eye = jnp.where(tc == tr, 1.0, 0.0).astype(F32)
        self.diag = (lax.broadcasted_iota(jnp.int32, (n, n), 0) // RW_HEAD
                     == lax.broadcasted_iota(jnp.int32, (n, n), 1) // RW_HEAD)


def _rwkv_stack(t, masks):
    return jnp.where(masks.stack, jnp.concatenate([t] * (RW_LANES // RW_HEAD), axis=0), 0.0).astype(BF16)


def _lockstep(gens):
    gens = list(gens)
    while gens:
        alive = []
        for gen in gens:
            try:
                next(gen)
                alive.append(gen)
            except StopIteration:
                pass
        gens = alive


def _rwkv_prepare(p, refs, rows, lanes, masks, reverse):
    c = CHUNK
    m = (RW_LANES // RW_HEAD) * c
    stack = functools.partial(_rwkv_stack, masks=masks)
    r, k, v, kk, a, lw = (ref[rows, lanes] for ref in refs)
    lp = _dot_f32(masks.tri, lw)
    yield
    lp_end = lp[0:1, :] if reverse else lp[c - 1:c, :]
    pinv = jnp.exp(-lp)
    dec = jnp.exp(lp_end - lp)
    kb = kk * a
    ar = jnp.concatenate([-kk * jnp.exp(lp - lw), r * jnp.exp(lp)], axis=0).astype(BF16)
    bk = jnp.concatenate([stack(kb * pinv), stack(k * pinv)], axis=0)
    v_s = stack(v)
    sc = _dot_nt(ar, bk)
    yield
    ab = jnp.where(masks.strict, sc[:c, :m], 0.0)
    akrk = jnp.concatenate([jnp.where(masks.strict, sc[:c, m:], 0.0),
                            jnp.where(masks.incl, sc[c:, m:], 0.0)], axis=0)
    p.update(ar=ar, v=v.astype(BF16), akrk_v=_dot(akrk, v_s),
             rb=jnp.where(masks.incl, sc[c:, :m], 0.0).astype(BF16),
             pw=_dot(ab, stack(ab)), tinv=masks.eye + ab,
             bkp=jnp.concatenate([kb * dec, k * dec], axis=0).astype(BF16),
             gdec=jnp.exp(lp_end))


def _rwkv_double(p, masks, last):
    c = CHUNK
    pw_s = _rwkv_stack(p["pw"], masks)
    if last:
        p["tinv"] = p["tinv"] + _dot(p["tinv"], pw_s)
    else:
        z = _dot(jnp.concatenate([p["pw"], p["tinv"]], axis=0), pw_s)
        p["pw"], p["tinv"] = z[:c], p["tinv"] + z[c:]


def _rwkv_chain(preps, g_ref, y_ref, rows_seq, lanes, masks):
    c = CHUNK
    for p, rows in zip(preps, rows_seq):
        g = g_ref[...]
        arg = _dot_nt(p["ar"], g)
        yield
        u = _dot(p["tinv"], _rwkv_stack(arg[:c] + p["akrk_v"][:c], masks))
        yield
        y_ref[rows, lanes] = arg[c:] + p["akrk_v"][c:] + _dot(p["rb"], _rwkv_stack(u, masks))
        uv = jnp.concatenate([u.astype(BF16), p["v"]], axis=0)
        g_ref[...] = g * p["gdec"] + jnp.where(masks.diag, _dot_tn(uv, p["bkp"]), 0.0)
        yield


def _rwkv_kernel(*refs, n_chunks):
    in_f, in_b = refs[0:6], refs[6:12]
    y_f, y_b, g_f, g_b = refs[12:16]

    @pl.when(pl.program_id(2) == 0)
    def _():
        g_f[...] = jnp.zeros_like(g_f)
        g_b[...] = jnp.zeros_like(g_b)

    c = CHUNK
    rows = [slice(ci * c, (ci + 1) * c) for ci in range(n_chunks)]
    masks = (_RwkvMasks(False), _RwkvMasks(True))
    plan = []
    for gi in range(RW_GROUPS):
        lanes = slice(gi * RW_LANES, (gi + 1) * RW_LANES)
        plan.append((in_f, y_f, g_f.at[gi], masks[0], False, rows, lanes))
        plan.append((in_b, y_b, g_b.at[gi], masks[1], True, rows[::-1], lanes))
    preps = [[{} for _ in rows] for _ in plan]
    _lockstep(_rwkv_prepare(p, ins, rw, lanes, mk, rev)
              for (ins, _, _, mk, rev, rws, lanes), plist in zip(plan, preps)
              for p, rw in zip(plist, rws))
    n_double = c.bit_length() - 2
    for it in range(n_double):
        for chain, plist in zip(plan, preps):
            for p in plist:
                _rwkv_double(p, chain[3], last=it == n_double - 1)
    _lockstep(_rwkv_chain(plist, g_ref, y_ref, rws, lanes, mk)
              for (_, y_ref, g_ref, mk, _, rws, lanes), plist in zip(plan, preps))


def _rwkv_scan(r, k, v, kk, a, lw_f, lw_b, *, bsz, seq, tb=256):
    t = bsz * seq
    tb = min(tb, seq)
    nblk = seq // tb
    n = RW_LANES * RW_GROUPS
    spec_f = pl.BlockSpec((tb, n), lambda b, hh, i: (b * nblk + i, hh))
    spec_b = pl.BlockSpec((tb, n), lambda b, hh, i: (b * nblk + nblk - 1 - i, hh))
    out = jax.ShapeDtypeStruct((t, RW_WIDTH), F32)
    state = pltpu.VMEM((RW_GROUPS, RW_LANES, RW_LANES), F32)
    return pl.pallas_call(
        functools.partial(_rwkv_kernel, n_chunks=tb // CHUNK),
        grid=(bsz, RW_WIDTH // n, nblk),
        in_specs=[spec_f] * 6 + [spec_b] * 6,
        out_specs=[spec_f, spec_b],
        out_shape=[out, out],
        scratch_shapes=[state, state],
        compiler_params=pltpu.CompilerParams(
            dimension_semantics=("arbitrary", "arbitrary", "arbitrary"),
            vmem_limit_bytes=VMEM_LIMIT),
        name="rwkv",
    )(r, k, v, kk, a, lw_f, r, k, v, kk, a, lw_b)


LORA_COLS = 384


def _rwkv_prep_kernel(ps_ref, vf_ref, wup_ref, aup_ref, gup_ref, vup_ref, vec_ref, seg_ref,
                      lwf_ref, lwb_ref, a_ref, k2_ref, kk_ref, v2_ref, g_ref, *, mix):
    w = RW_WIDTH
    k = ps_ref[:, w:2 * w]
    v = ps_ref[:, 2 * w:3 * w]
    lora = ps_ref[:, 3 * w:3 * w + LORA_COLS]
    vec = vec_ref[...]
    for d, out_ref in ((0, lwf_ref), (1, lwb_ref)):
        z = vec[d:d + 1] + _dot(jnp.tanh(lora[:, d * DECAY_LORA:(d + 1) * DECAY_LORA]), wup_ref[d])
        softplus_neg = jnp.maximum(-z, 0.0) + jnp.log(1.0 + jnp.exp(-jnp.abs(z)))
        out_ref[...] = -jnp.exp(-softplus_neg - 0.5)
    off = 2 * DECAY_LORA
    a = jax.nn.sigmoid(vec[2:3] + _dot(lora[:, off:off + AAA_LORA], aup_ref[...]))
    off += AAA_LORA
    g_ref[...] = _dot(jax.nn.sigmoid(lora[:, off:off + GATE_LORA]), gup_ref[...])
    off += GATE_LORA
    if mix:
        v = v + (vf_ref[...] - v) * jax.nn.sigmoid(vec[3:4] + _dot(lora[:, off:LORA_COLS], vup_ref[...]))
    v2_ref[...] = v
    kx = k * vec[4:5]
    sq = kx * kx
    sq_hi = sq.astype(BF16)
    sq_lo = (sq - sq_hi.astype(F32)).astype(BF16)
    norm2 = _dot(sq_hi, seg_ref[...]) + _dot(sq_lo, seg_ref[...])
    kk_ref[...] = kx / jnp.maximum(jnp.sqrt(norm2), 1e-12)
    k2_ref[...] = k * (1.0 + (a - 1.0) * vec[5:6])
    a_ref[...] = a


def _rwkv_prep(ps, v_first, w_up, a_up, g_up, v_up, vec, *, tm=256):
    t = ps.shape[0]
    w = RW_WIDTH
    mix = v_first is not None
    if not mix:
        v_first = ps
        v_up = jnp.zeros((LORA_COLS - 2 * DECAY_LORA - AAA_LORA - GATE_LORA, w), F32)
    head = jnp.arange(w, dtype=jnp.int32) // RW_HEAD
    seg = (head[:, None] == head[None, :]).astype(BF16)
    row = lambda i: (i, 0)
    fixed2 = lambda i: (0, 0)
    full = lambda arr: pl.BlockSpec(arr.shape, (lambda i: (0, 0, 0)) if arr.ndim == 3 else fixed2)
    out = jax.ShapeDtypeStruct((t, w), F32)
    return pl.pallas_call(
        functools.partial(_rwkv_prep_kernel, mix=mix),
        grid=(t // tm,),
        in_specs=[pl.BlockSpec((tm, ps.shape[1]), row), pl.BlockSpec((tm, w), row),
                  full(w_up), full(a_up), full(g_up), full(v_up), full(vec), full(seg)],
        out_specs=[pl.BlockSpec((tm, w), row)] * 7,
        out_shape=[out] * 7,
        compiler_params=pltpu.CompilerParams(
            dimension_semantics=("arbitrary",), vmem_limit_bytes=VMEM_LIMIT),
        name="rwkv_prep",
    )(ps, v_first, w_up, a_up, g_up, v_up, vec, seg)


PAIR = 256


def _moe_kernel(be_ref, xs_ref, w1_ref, b1_ref, w2_ref, b2_ref, o_ref, w1p_ref, w2b_ref):
    i = pl.program_id(0)
    f2 = w1_ref.shape[-1]
    half = PAIR // 2

    @pl.when((i == 0) | (be_ref[i] != be_ref[jnp.maximum(i - 1, 0)]))
    def _():
        src = lax.broadcasted_iota(jnp.int32, (PAIR, PAIR), 0)
        dst = lax.broadcasted_iota(jnp.int32, (PAIR, PAIR), 1)
        perm = jnp.where(src == jnp.where(dst < half, 2 * dst, 2 * (dst - half) + 1), 1.0, 0.0)
        for j in range(0, f2, PAIR):
            w1p_ref[:, j:j + PAIR] = _dot(w1_ref[:, j:j + PAIR], perm).astype(BF16)
        w2b_ref[...] = w2_ref[...].astype(BF16)

    h = _dot(_unpack_bf16_halves(xs_ref[...]), w1p_ref[...]) + b1_ref[...]
    acts = []
    for j in range(0, f2, PAIR):
        glu = jnp.minimum(h[:, j:j + half], SWIGLU_LIMIT)
        lin = jnp.clip(h[:, j + half:j + PAIR], -SWIGLU_LIMIT, SWIGLU_LIMIT)
        acts.append((glu * jax.nn.sigmoid(SWIGLU_ALPHA * glu) * (lin + 1.0)).astype(BF16))
    o_ref[...] = _dot(jnp.concatenate(acts, axis=1), w2b_ref[...]) + b2_ref[...]


def _moe_experts(block_exp, xs, w1, b1, w2, b2, *, layer):
    n_slots = xs.shape[0]
    d, f2 = w1.shape[-2:]
    f = f2 // 2
    blk = MOE_BLOCK
    n_blocks = n_slots // blk
    b1p = b1.reshape(DEPTH, N_EXPERTS, f2 // PAIR, PAIR // 2, 2).swapaxes(-1, -2)
    b1p = b1p.reshape(DEPTH, N_EXPERTS, 1, f2)
    wspec = lambda shape: pl.BlockSpec((None, None) + shape, lambda i, be: (layer, be[i], 0, 0))
    return pl.pallas_call(
        _moe_kernel,
        grid_spec=pltpu.PrefetchScalarGridSpec(
            num_scalar_prefetch=1,
            grid=(n_blocks,),
            in_specs=[pl.BlockSpec((blk, d // 2), lambda i, be: (i, 0)),
                      wspec((d, f2)), wspec((1, f2)), wspec((f, d)), wspec((1, d))],
            out_specs=pl.BlockSpec((blk, d), lambda i, be: (i, 0)),
            scratch_shapes=[pltpu.VMEM((d, f2), BF16), pltpu.VMEM((f, d), BF16)]),
        out_shape=jax.ShapeDtypeStruct((n_slots, d), F32),
        compiler_params=pltpu.CompilerParams(
            dimension_semantics=("arbitrary",), vmem_limit_bytes=VMEM_LIMIT),
        name="moe_experts",
    )(block_exp, xs, w1, b1p, w2, b2.reshape(DEPTH, N_EXPERTS, 1, d))


def _ln_rows(y, g, b):
    yc = y - jnp.mean(y, axis=-1, keepdims=True)
    var = jnp.mean(yc * yc, axis=-1, keepdims=True)
    return yc * lax.rsqrt(var + NORM_EPS) * g + b


def _pack_bf16_halves(x):
    n = x.shape[-1] // 2
    bits = pltpu.bitcast(x.astype(BF16).astype(F32), jnp.uint32)
    return (bits[:, :n] >> 16) | (bits[:, n:] & jnp.uint32(0xFFFF0000))


def _unpack_bf16_halves(w):
    lo = pltpu.bitcast(w << 16, F32)
    hi = pltpu.bitcast(w & jnp.uint32(0xFFFF0000), F32)
    return jnp.concatenate([lo, hi], axis=1)


def _merge_ln_kernel(oa_ref, ob_ref, xb_ref, x_ref, wg_ref, pa_ref, pb_ref, wo_ref, g_ref, b_ref,
                     o_ref, op_ref):
    d = x_ref.shape[-1]
    gate = _dot(xb_ref[...], wg_ref[...])
    merged = (jax.nn.sigmoid(gate[:, :d]) * _dot(oa_ref[...], pa_ref[...])
              + jax.nn.sigmoid(gate[:, d:]) * _dot(ob_ref[...], pb_ref[...]))
    out = _ln_rows(DN_ALPHA * x_ref[...] + _dot(merged, wo_ref[...]), g_ref[...], b_ref[...])
    o_ref[...] = out
    op_ref[...] = _pack_bf16_halves(out)


def _merge_ln(o_a, o_b, xb, x, w_gate, proj_a, proj_b, w_out, g, b, *, tm=512):
    t, d = x.shape
    row = lambda i: (i, 0)
    fixed = lambda i: (0, 0)
    wspec = pl.BlockSpec((d, d), fixed)
    act = pl.BlockSpec((tm, d), row)
    return pl.pallas_call(
        _merge_ln_kernel,
        grid=(t // tm,),
        in_specs=[act, act, act, act, pl.BlockSpec((d, 2 * d), fixed), wspec, wspec, wspec,
                  pl.BlockSpec((1, d), fixed), pl.BlockSpec((1, d), fixed)],
        out_specs=[act, pl.BlockSpec((tm, d // 2), row)],
        out_shape=[jax.ShapeDtypeStruct((t, d), F32), jax.ShapeDtypeStruct((t, d // 2), jnp.uint32)],
        compiler_params=pltpu.CompilerParams(
            dimension_semantics=("arbitrary",), vmem_limit_bytes=VMEM_LIMIT),
        name="merge_ln",
    )(o_a, o_b, xb, x, w_gate, proj_a.astype(BF16), proj_b.astype(BF16), w_out.astype(BF16),
      g.reshape(1, d), b.reshape(1, d))


def _combine_ln_kernel(x_ref, yg_ref, gate_ref, g_ref, b_ref, o_ref, ob_ref):
    gate = gate_ref[...]
    moe = gate[:, 0:1] * yg_ref[0]
    for kk in range(1, TOP_K):
        moe = moe + gate[:, kk:kk + 1] * yg_ref[kk]
    out = _ln_rows(DN_ALPHA * x_ref[...] + moe, g_ref[...], b_ref[...])
    o_ref[...] = out
    ob_ref[...] = out.astype(BF16)


def _combine_ln(x, yg, gates, g, b, *, tm=512):
    t, d = x.shape
    row = lambda i: (i, 0)
    fixed = lambda i: (0, 0)
    return pl.pallas_call(
        _combine_ln_kernel,
        grid=(t // tm,),
        in_specs=[pl.BlockSpec((tm, d), row), pl.BlockSpec((TOP_K, tm, d), lambda i: (0, i, 0)),
                  pl.BlockSpec((tm, TOP_K), row),
                  pl.BlockSpec((1, d), fixed), pl.BlockSpec((1, d), fixed)],
        out_specs=[pl.BlockSpec((tm, d), row), pl.BlockSpec((tm, d), row)],
        out_shape=[jax.ShapeDtypeStruct((t, d), F32), jax.ShapeDtypeStruct((t, d), BF16)],
        compiler_params=pltpu.CompilerParams(
            dimension_semantics=("arbitrary",), vmem_limit_bytes=VMEM_LIMIT),
        name="combine_ln",
    )(x, yg, gates, g.reshape(1, d), b.reshape(1, d))


def _rw_heads(t):
    return t.reshape(t.shape[0], RW_HEADS, RW_HEAD)


def _moe(x2, x_packed, router_w, router_b, w1, b1, w2, b2, *, layer):
    t, d = x2.shape
    n_assign = t * TOP_K
    n_blocks = -(-n_assign // MOE_BLOCK) + N_EXPERTS
    n_slots = n_blocks * MOE_BLOCK
    top_idx, gates, rank, counts = _router(x2, router_w, router_b)
    e_flat = top_idx.reshape(-1)
    order = jnp.argsort(e_flat).astype(jnp.int32)
    experts = jnp.arange(N_EXPERTS, dtype=jnp.int32)
    starts = jnp.cumsum(counts) - counts
    padded = (counts + MOE_BLOCK - 1) // MOE_BLOCK * MOE_BLOCK
    pad_ends = jnp.cumsum(padded)
    pad_starts = pad_ends - padded
    shift = pad_starts - starts
    slot_of = rank.reshape(-1) + jnp.sum(jnp.where(e_flat[:, None] == experts, pad_starts, 0), axis=1)
    block_start = jnp.arange(n_blocks, dtype=jnp.int32) * MOE_BLOCK
    block_exp = jnp.minimum(jnp.sum(pad_ends[None, :] <= block_start[:, None], axis=1),
                            N_EXPERTS - 1).astype(jnp.int32)
    blk_hot = block_exp[:, None] == experts
    per_slot = lambda tab: jnp.repeat(jnp.sum(jnp.where(blk_hot, tab, 0), axis=1), MOE_BLOCK)
    slot = jnp.arange(n_slots, dtype=jnp.int32)
    valid = slot - per_slot(pad_starts) < per_slot(counts)
    src = jnp.clip(slot - per_slot(shift), 0, n_assign - 1)
    slot_tok = jnp.where(valid, order[src] // TOP_K, 0)
    y = _moe_experts(block_exp, x_packed[slot_tok], w1, b1, w2, b2, layer=layer)
    return y[slot_of.reshape(t, TOP_K).T.reshape(-1)].reshape(TOP_K, t, d), gates


def kernel(x, w_in, hg_lb_logits, hg_norm_w, rw_mu, rw_w0, rw_w_up, rw_a0, rw_a_up, rw_g_up,
           rw_k_k, rw_k_a, rw_r_k, rw_lnx_w, rw_lnx_b, rw_v_down, rw_v_up, rw_v0, proj_a, proj_b,
           w_out, ln1_g, ln1_b, router_w, router_b, moe_w1, moe_b1, moe_w2, moe_b2, ln2_g, ln2_b):
    bsz, s, d = x.shape
    t = bsz * s
    scan = dict(bsz=bsz, seq=s)
    lb_all = jnp.cumsum(jax.nn.softmax(hg_lb_logits.astype(F32), axis=0), axis=0)
    lb_all = lb_all - lb_all[0:1]
    hg_cols = 5 * HG_WIDTH
    rw_cols = 3 * RW_WIDTH + 2 * DECAY_LORA + AAA_LORA + GATE_LORA
    x2 = x.reshape(t, d)
    xb = x2.astype(BF16)
    v_first = None
    for l in range(DEPTH):
        w_l = w_in[l].astype(BF16)
        p_hg = _matmul(xb, w_l[:, :hg_cols], tm=1024, tn=2560)
        w_rw = w_in[l][:, hg_cols:hg_cols + rw_cols]
        w_self, w_nb = w_rw * (1.0 - rw_mu[l]), w_rw * rw_mu[l]
        if l > 0:
            w_self = jnp.concatenate([w_self, rw_v_down[l - 1]], axis=1)
        pad_cols = lambda w: jnp.pad(w, ((0, 0), (0, 3456 - w.shape[1]))).astype(BF16)
        x3 = jnp.pad(x2.reshape(bsz, s, d), ((0, 0), (1, 1), (0, 0)))
        x_nb = (0.5 * (x3[:, :-2] + x3[:, 2:])).reshape(t, d).astype(BF16)
        ps = _matmul2(xb, x_nb, pad_cols(w_self), pad_cols(w_nb), tm=2048)

        o_a = _hgrn_scan(p_hg, lb_all[l], o_fwd=_hgrn_scan(p_hg, lb_all[l], **scan),
                         norm_w=hg_norm_w[l], **scan)

        r = ps[:, :RW_WIDTH]
        zeros = jnp.zeros((RW_WIDTH,), F32)
        if l == 0:
            v_first = ps[:, 2 * RW_WIDTH:3 * RW_WIDTH]
            mixing = dict(v_first=None, v_up=None)
            v0 = zeros
        else:
            v_up = rw_v_up[l - 1]
            mixing = dict(v_first=v_first, v_up=jnp.pad(v_up, ((0, 3456 - rw_cols - v_up.shape[0]), (0, 0))))
            v0 = rw_v0[l - 1]
        vec = jnp.stack([rw_w0[l, 0], rw_w0[l, 1], rw_a0[l], v0, rw_k_k[l], rw_k_a[l], zeros, zeros])
        lw_f, lw_b, a, k, kk, v, g = _rwkv_prep(ps, w_up=rw_w_up[l], a_up=rw_a_up[l], g_up=rw_g_up[l],
                                                vec=vec, **mixing)
        y_f, y_b = _rwkv_scan(ps, k, v, kk, a, lw_f, lw_b, **scan)
        y = _rw_heads(y_f + y_b)
        yc = y - jnp.mean(y, axis=-1, keepdims=True)
        yn = yc * lax.rsqrt(jnp.mean(yc * yc, axis=-1, keepdims=True) + RW_LN_EPS)
        yn = yn * _rw_heads(rw_lnx_w[l][None])[0] + _rw_heads(rw_lnx_b[l][None])[0]
        bonus = jnp.sum(_rw_heads(r) * _rw_heads(k) * rw_r_k[l], axis=-1, keepdims=True) * _rw_heads(v)
        o_b = (yn + bonus).reshape(t, RW_WIDTH) * g

        x2, x_packed = _merge_ln(o_a, o_b, xb, x2, w_l[:, hg_cols + rw_cols:], proj_a[l], proj_b[l],
                                 w_out[l], ln1_g[l], ln1_b[l])
        yg, gates = _moe(x2, x_packed, router_w[l], router_b[l], moe_w1, moe_b1, moe_w2, moe_b2, layer=l)
        x2, xb = _combine_ln(x2, yg, gates, ln2_g[l], ln2_b[l])
    return x2.reshape(bsz, s, d)
```

```python
import functools

import jax
import jax.numpy as jnp
from jax import lax
from jax.experimental import pallas as pl
from jax.experimental.pallas import tpu as pltpu

F32 = jnp.float32
BF16 = jnp.bfloat16

D_MODEL = 1024
DEPTH = 4
HG_HEADS = 8
HG_D = 128
HG_WIDTH = HG_HEADS * HG_D
RW_HEAD = 64
RW_HEADS = D_MODEL // RW_HEAD
RW_WIDTH = D_MODEL
DECAY_LORA = 64
AAA_LORA = 64
GATE_LORA = 128
N_EXPERTS = 32
TOP_K = 4
MOE_BLOCK = 512
SWIGLU_ALPHA = 1.702
SWIGLU_LIMIT = 7.0
NORM_EPS = 1e-5
RW_LN_EPS = 64e-5
DN_ALPHA = (2 * DEPTH) ** 0.25

CHUNK = 64
SUB = 16
DIAG_LOG_RANGE = 60.0
RW_LANES = 256
RW_GROUPS = 2
LORA_COLS = 384
RW_PROJ_COLS = 3 * RW_WIDTH + LORA_COLS
PAIR = 256
LANES = 128
NEG_BIG = -3.0e38
VMEM_LIMIT = 56 * 1024 * 1024
assert CHUNK == RW_HEAD

_NT = (((1,), (1,)), ((), ()))
_TN = (((0,), (0,)), ((), ()))


def _dot(a, b):
    return jnp.dot(a.astype(BF16), b.astype(BF16), preferred_element_type=F32)


def _dot_nt(a, b):
    return lax.dot_general(a.astype(BF16), b.astype(BF16), _NT, preferred_element_type=F32)


def _dot_tn(a, b):
    return lax.dot_general(a.astype(BF16), b.astype(BF16), _TN, preferred_element_type=F32)


def _dot_f32(a, b):
    return jnp.dot(a, b, preferred_element_type=F32, precision=lax.Precision.HIGHEST)


def _mm_kernel(x_ref, w_ref, o_ref):
    o_ref[...] = _dot(x_ref[...], w_ref[...])


def _matmul(x, w, *, tm=512, tn=512):
    m, k = x.shape
    n = w.shape[1]
    tm = min(tm, m)
    tn = min(tn, n)
    assert m % tm == 0 and n % tn == 0
    return pl.pallas_call(
        _mm_kernel,
        grid=(n // tn, m // tm),
        in_specs=[pl.BlockSpec((tm, k), lambda j, i: (i, 0)),
                  pl.BlockSpec((k, tn), lambda j, i: (0, j))],
        out_specs=pl.BlockSpec((tm, tn), lambda j, i: (i, j)),
        out_shape=jax.ShapeDtypeStruct((m, n), F32),
        compiler_params=pltpu.CompilerParams(
            dimension_semantics=("arbitrary", "arbitrary"), vmem_limit_bytes=VMEM_LIMIT),
        name="matmul",
    )(x, w)


def _mm2_kernel(x_ref, y_ref, wx_ref, wy_ref, o_ref):
    o_ref[...] = _dot(x_ref[...], wx_ref[...]) + _dot(y_ref[...], wy_ref[...])


def _matmul2(x, y, wx, wy, *, tm=1024, tn=1152):
    m, k = x.shape
    n = wx.shape[1]
    tm = min(tm, m)
    assert m % tm == 0 and n % tn == 0
    xspec = pl.BlockSpec((tm, k), lambda j, i: (i, 0))
    wspec = pl.BlockSpec((k, tn), lambda j, i: (0, j))
    return pl.pallas_call(
        _mm2_kernel,
        grid=(n // tn, m // tm),
        in_specs=[xspec, xspec, wspec, wspec],
        out_specs=pl.BlockSpec((tm, tn), lambda j, i: (i, j)),
        out_shape=jax.ShapeDtypeStruct((m, n), F32),
        compiler_params=pltpu.CompilerParams(
            dimension_semantics=("arbitrary", "arbitrary"), vmem_limit_bytes=VMEM_LIMIT),
        name="matmul2",
    )(x, y, wx, wy)


def _router_kernel(x_ref, w_ref, b_ref, idx_ref, gate_ref, rank_ref, count_ref):
    @pl.when(pl.program_id(0) == 0)
    def _():
        count_ref[...] = jnp.zeros_like(count_ref)

    logits = _dot_f32(x_ref[...], w_ref[...]) + b_ref[...]
    tm = logits.shape[0]
    lane = lax.broadcasted_iota(jnp.int32, logits.shape, 1).astype(F32)
    cur = logits
    vals, idxs = [], []
    for _ in range(TOP_K):
        top = jnp.max(cur, axis=-1, keepdims=True)
        idx = jnp.min(jnp.where(cur == top, lane, float(LANES)), axis=-1, keepdims=True)
        vals.append(top)
        idxs.append(idx)
        cur = jnp.where(lane == idx, NEG_BIG, cur)
    exps = [jnp.exp(v - vals[0]) for v in vals]
    total = exps[0]
    for e in exps[1:]:
        total = total + e
    chosen = jnp.zeros_like(logits)
    for idx in idxs:
        chosen = jnp.where(lane == idx, 1.0, chosen)
    earlier = (lax.broadcasted_iota(jnp.int32, (tm, tm), 1)
               < lax.broadcasted_iota(jnp.int32, (tm, tm), 0))
    before = count_ref[...] + _dot(jnp.where(earlier, 1.0, 0.0), chosen)
    count_ref[...] = count_ref[...] + jnp.sum(chosen, axis=0, keepdims=True)
    idx_out = jnp.zeros_like(logits)
    gate_out = jnp.zeros_like(logits)
    rank_out = jnp.zeros_like(logits)
    for j in range(TOP_K):
        rank = jnp.sum(jnp.where(lane == idxs[j], before, 0.0), axis=-1, keepdims=True)
        idx_out = jnp.where(lane == float(j), idxs[j], idx_out)
        gate_out = jnp.where(lane == float(j), exps[j] / total, gate_out)
        rank_out = jnp.where(lane == float(j), rank, rank_out)
    idx_ref[...] = idx_out.astype(jnp.int32)
    gate_ref[...] = gate_out
    rank_ref[...] = rank_out.astype(jnp.int32)


def _router(x, w, b, *, tm=512):
    m, k = x.shape
    pad = LANES - N_EXPERTS
    w = jnp.pad(w, ((0, 0), (0, pad)))
    b = jnp.pad(b, (0, pad), constant_values=NEG_BIG).reshape(1, LANES)
    row = pl.BlockSpec((tm, LANES), lambda i: (i, 0))
    fixed = pl.BlockSpec((1, LANES), lambda i: (0, 0))
    per_token = lambda dtype: jax.ShapeDtypeStruct((m, LANES), dtype)
    idx, gates, rank, counts = pl.pallas_call(
        _router_kernel,
        grid=(m // tm,),
        in_specs=[pl.BlockSpec((tm, k), lambda i: (i, 0)), pl.BlockSpec((k, LANES), lambda i: (0, 0)), fixed],
        out_specs=[row, row, row, fixed],
        out_shape=[per_token(jnp.int32), per_token(F32), per_token(jnp.int32),
                   jax.ShapeDtypeStruct((1, LANES), F32)],
        compiler_params=pltpu.CompilerParams(
            dimension_semantics=("arbitrary",), vmem_limit_bytes=VMEM_LIMIT),
        name="router",
    )(x, w, b)
    return (idx[:, :TOP_K], gates[:, :TOP_K], rank[:, :TOP_K],
            counts[0, :N_EXPERTS].astype(jnp.int32))


def _hgrn_kernel(q_ref, z_ref, v_ref, lb_ref, *rest, reverse, n_chunks):
    st_ref = rest[-1]

    @pl.when(pl.program_id(2) == 0)
    def _():
        st_ref[...] = jnp.zeros_like(st_ref)

    lb = lb_ref[...]
    f = lb + (1.0 - lb) * jax.nn.sigmoid(z_ref[...])
    lf = jnp.log(f)
    sub_total = jnp.sum(lf.reshape(lf.shape[0] // SUB, SUB, HG_D), axis=1)
    factorable = jnp.min(sub_total) > -DIAG_LOG_RANGE
    args = (q_ref, f, lf, v_ref) + rest

    @pl.when(factorable)
    def _():
        _hgrn_block(*args, reverse=reverse, n_chunks=n_chunks, pairwise=False)

    @pl.when(jnp.logical_not(factorable))
    def _():
        _hgrn_block(*args, reverse=reverse, n_chunks=n_chunks, pairwise=True)


def _hgrn_block(q_ref, f_blk, lf_blk, v_ref, *rest, reverse, n_chunks, pairwise):
    if reverse:
        of_ref, og_ref, nw_ref, o_ref, st_ref = rest
    else:
        o_ref, st_ref = rest
    c = CHUNK
    row = lax.broadcasted_iota(jnp.int32, (c, c), 0)
    col = lax.broadcasted_iota(jnp.int32, (c, c), 1)
    causal = (col >= row) if reverse else (col <= row)
    tri = jnp.where(causal, 1.0, 0.0).astype(F32)
    diag_mask = causal & (row // SUB == col // SUB)
    srow = lax.broadcasted_iota(jnp.int32, (SUB, 1), 0)

    splits = []
    size = c
    while size > SUB:
        for lo in range(0, c, size):
            splits.append((lo, lo + size // 2, lo + size))
        size //= 2

    order = list(range(n_chunks - 1, -1, -1) if reverse else range(n_chunks))
    chunks = [dict(rows=slice(ci * c, (ci + 1) * c)) for ci in order]
    for ch in chunks:
        rows = ch["rows"]
        ch.update(lf=lf_blk[rows], kk=1.0 - f_blk[rows], q=q_ref[rows, :], v=v_ref[rows, :])
        ch["b"] = _dot_f32(tri, ch["lf"])
    for ch in chunks:
        q, kk, v, b, lf = ch["q"], ch["kk"], ch["v"], ch["b"], ch["lf"]
        b_end = b[0:1, :] if reverse else b[c - 1:c, :]
        ch["qdec"] = (q * jnp.exp(b)).astype(BF16)
        ch["dec"] = jnp.exp(b_end)
        ch["kv"] = _dot_tn(v, kk * jnp.exp(b_end - b))
        scores = []
        for lo, mid, hi in splits:
            if reverse:
                anc = b[mid:mid + 1, :]
                qs, ks = slice(lo, mid), slice(mid, hi)
            else:
                anc = b[mid - 1:mid, :]
                qs, ks = slice(mid, hi), slice(lo, mid)
            scores.append((qs, ks, _dot_nt(q[qs] * jnp.exp(b[qs] - anc), kk[ks] * jnp.exp(anc - b[ks]))))
        ch["scores"] = scores
        if not pairwise:
            edge = b - lf
            anc = jnp.concatenate(
                [jnp.broadcast_to(edge[d0 + SUB - 1:d0 + SUB] if reverse else edge[d0:d0 + 1], (SUB, HG_D))
                 for d0 in range(0, c, SUB)], axis=0)
            ch["diag"] = jnp.where(diag_mask, _dot_nt(q * jnp.exp(b - anc), kk * jnp.exp(anc - b)), 0.0)
    for ch in chunks:
        v = ch["v"]
        parts = [None] * (c // SUB)
        for qs, ks, s in ch["scores"]:
            contrib = _dot(s, v[ks])
            for j in range(qs.start // SUB, qs.stop // SUB):
                piece = contrib[j * SUB - qs.start:(j + 1) * SUB - qs.start]
                parts[j] = piece if parts[j] is None else parts[j] + piece
        intra = jnp.concatenate([jnp.zeros((SUB, HG_D), F32) if p is None else p for p in parts], axis=0)
        if pairwise:
            accs = []
            for d0 in range(0, c, SUB):
                qb, bb, kb, vb = (t[d0:d0 + SUB] for t in (ch["q"], ch["b"], ch["kk"], v))
                acc = jnp.zeros((SUB, HG_D), F32)
                for s in range(SUB):
                    mask = (srow <= s) if reverse else (srow >= s)
                    e = jnp.where(mask, jnp.exp(jnp.minimum(bb - bb[s:s + 1, :], 0.0)), 0.0)
                    w = jnp.sum(qb * kb[s:s + 1, :] * e, axis=-1, keepdims=True)
                    acc = acc + w * vb[s:s + 1, :]
                accs.append(acc)
            ch["intra"] = intra + jnp.concatenate(accs, axis=0)
        else:
            ch["intra"] = intra + _dot(ch["diag"], v)
    st = st_ref[...]
    for ch in chunks:
        ch["st"] = st
        st = st * ch["dec"] + ch["kv"]
    st_ref[...] = st
    for ch in chunks:
        rows = ch["rows"]
        o = ch["intra"] + _dot_nt(ch["qdec"], ch["st"])
        if reverse:
            o = o + of_ref[rows, :]
            o = o * lax.rsqrt(jnp.mean(o * o, axis=-1, keepdims=True) + NORM_EPS) * nw_ref[...]
            o = o * jax.nn.silu(og_ref[rows, :])
        o_ref[rows, :] = o


def _hgrn_scan(p_hg, lb, *, bsz, seq, o_fwd=None, norm_w=None, tb=1024):
    reverse = o_fwd is not None
    t = bsz * seq
    tb = min(tb, seq)
    nblk = seq // tb
    h = HG_HEADS
    zoff = 2 * h if reverse else h

    def col(off):
        return pl.BlockSpec((tb, HG_D),
                            lambda b, hh, i: (b * nblk + (nblk - 1 - i if reverse else i), off + hh))

    vec = pl.BlockSpec((1, HG_D), lambda b, hh, i: (0, 0))
    in_specs = [col(0), col(zoff), col(3 * h), pl.BlockSpec((1, HG_D), lambda b, hh, i: (0, hh))]
    args = [p_hg, p_hg, p_hg, lb.reshape(1, -1)]
    if reverse:
        in_specs += [col(0), col(4 * h), vec]
        args += [o_fwd, p_hg, norm_w.reshape(1, -1)]
    return pl.pallas_call(
        functools.partial(_hgrn_kernel, reverse=reverse, n_chunks=tb // CHUNK),
        grid=(bsz, h, nblk),
        in_specs=in_specs,
        out_specs=col(0),
        out_shape=jax.ShapeDtypeStruct((t, h * HG_D), F32),
        scratch_shapes=[pltpu.VMEM((HG_D, HG_D), F32)],
        compiler_params=pltpu.CompilerParams(
            dimension_semantics=("arbitrary", "arbitrary", "arbitrary"),
            vmem_limit_bytes=VMEM_LIMIT),
        name="hgrn_bwd" if reverse else "hgrn_fwd",
    )(*args)


class _RwkvMasks:
    def __init__(self, reverse):
        c, n = CHUNK, RW_LANES
        m = (n // RW_HEAD) * c
        row = lax.broadcasted_iota(jnp.int32, (c, c), 0)
        col = lax.broadcasted_iota(jnp.int32, (c, c), 1)
        self.tri = jnp.where((col >= row) if reverse else (col <= row), 1.0, 0.0).astype(F32)
        self.stack = (lax.broadcasted_iota(jnp.int32, (m, n), 0) // c
                      == lax.broadcasted_iota(jnp.int32, (m, n), 1) // RW_HEAD)
        tr = lax.broadcasted_iota(jnp.int32, (c, m), 0)
        tc = lax.broadcasted_iota(jnp.int32, (c, m), 1) % c
        self.strict = (tc > tr) if reverse else (tc < tr)
        self.incl = (tc >= tr) if reverse else (tc <= tr)
        self.eye = jnp.where(tc == tr, 1.0, 0.0).astype(F32)
        self.diag = (lax.broadcasted_iota(jnp.int32, (n, n), 0) // RW_HEAD
                     == lax.broadcasted_iota(jnp.int32, (n, n), 1) // RW_HEAD)


def _rwkv_stack(t, masks):
    return jnp.where(masks.stack, jnp.concatenate([t] * (RW_LANES // RW_HEAD), axis=0), 0.0).astype(BF16)


def _lockstep(gens):
    gens = list(gens)
    while gens:
        alive = []
        for gen in gens:
            try:
                next(gen)
                alive.append(gen)
            except StopIteration:
                pass
        gens = alive


def _rwkv_prepare(p, refs, rows, lanes, masks, reverse):
    c = CHUNK
    m = (RW_LANES // RW_HEAD) * c
    stack = functools.partial(_rwkv_stack, masks=masks)
    r, k, v, kk, a, lw = (ref[rows, lanes] for ref in refs)
    lp = _dot_f32(masks.tri, lw)
    yield
    lp_end = lp[0:1, :] if reverse else lp[c - 1:c, :]
    pinv = jnp.exp(-lp)
    dec = jnp.exp(lp_end - lp)
    kb = kk * a
    ar = jnp.concatenate([-kk * jnp.exp(lp - lw), r * jnp.exp(lp)], axis=0).astype(BF16)
    bk = jnp.concatenate([stack(kb * pinv), stack(k * pinv)], axis=0)
    v_s = stack(v)
    sc = _dot_nt(ar, bk)
    yield
    ab = jnp.where(masks.strict, sc[:c, :m], 0.0)
    akrk = jnp.concatenate([jnp.where(masks.strict, sc[:c, m:], 0.0),
                            jnp.where(masks.incl, sc[c:, m:], 0.0)], axis=0)
    p.update(ar=ar, v=v.astype(BF16), akrk_v=_dot(akrk, v_s),
             rb=jnp.where(masks.incl, sc[c:, :m], 0.0).astype(BF16),
             pw=_dot(ab, stack(ab)), tinv=masks.eye + ab,
             bkp=jnp.concatenate([kb * dec, k * dec], axis=0).astype(BF16),
             gdec=jnp.exp(lp_end))


def _rwkv_double(p, masks, last):
    c = CHUNK
    pw_s = _rwkv_stack(p["pw"], masks)
    if last:
        p["tinv"] = p["tinv"] + _dot(p["tinv"], pw_s)
    else:
        z = _dot(jnp.concatenate([p["pw"], p["tinv"]], axis=0), pw_s)
        p["pw"], p["tinv"] = z[:c], p["tinv"] + z[c:]


def _rwkv_chain(preps, g_ref, y_ref, rows_seq, lanes, masks):
    c = CHUNK
    for p, rows in zip(preps, rows_seq):
        g = g_ref[...]
        arg = _dot_nt(p["ar"], g)
        yield
        u = _dot(p["tinv"], _rwkv_stack(arg[:c] + p["akrk_v"][:c], masks))
        yield
        y_ref[rows, lanes] = arg[c:] + p["akrk_v"][c:] + _dot(p["rb"], _rwkv_stack(u, masks))
        uv = jnp.concatenate([u.astype(BF16), p["v"]], axis=0)
        g_ref[...] = g * p["gdec"] + jnp.where(masks.diag, _dot_tn(uv, p["bkp"]), 0.0)
        yield


def _rwkv_kernel(*refs, n_chunks):
    in_f, in_b = refs[0:6], refs[6:12]
    y_f, y_b, g_f, g_b = refs[12:16]

    @pl.when(pl.program_id(2) == 0)
    def _():
        g_f[...] = jnp.zeros_like(g_f)
        g_b[...] = jnp.zeros_like(g_b)

    c = CHUNK
    rows = [slice(ci * c, (ci + 1) * c) for ci in range(n_chunks)]
    masks = (_RwkvMasks(False), _RwkvMasks(True))
    plan = []
    for gi in range(RW_GROUPS):
        lanes = slice(gi * RW_LANES, (gi + 1) * RW_LANES)
        plan.append((in_f, y_f, g_f.at[gi], masks[0], False, rows, lanes))
        plan.append((in_b, y_b, g_b.at[gi], masks[1], True, rows[::-1], lanes))
    preps = [[{} for _ in rows] for _ in plan]
    _lockstep(_rwkv_prepare(p, ins, rw, lanes, mk, rev)
              for (ins, _, _, mk, rev, rws, lanes), plist in zip(plan, preps)
              for p, rw in zip(plist, rws))
    n_double = c.bit_length() - 2
    for it in range(n_double):
        for chain, plist in zip(plan, preps):
            for p in plist:
                _rwkv_double(p, chain[3], last=it == n_double - 1)
    _lockstep(_rwkv_chain(plist, g_ref, y_ref, rws, lanes, mk)
              for (_, y_ref, g_ref, mk, _, rws, lanes), plist in zip(plan, preps))


def _rwkv_scan(r, k, v, kk, a, lw_f, lw_b, *, bsz, seq, tb=256):
    t = bsz * seq
    tb = min(tb, seq)
    nblk = seq // tb
    n = RW_LANES * RW_GROUPS
    spec_f = pl.BlockSpec((tb, n), lambda b, hh, i: (b * nblk + i, hh))
    spec_b = pl.BlockSpec((tb, n), lambda b, hh, i: (b * nblk + nblk - 1 - i, hh))
    out = jax.ShapeDtypeStruct((t, RW_WIDTH), F32)
    state = pltpu.VMEM((RW_GROUPS, RW_LANES, RW_LANES), F32)
    return pl.pallas_call(
        functools.partial(_rwkv_kernel, n_chunks=tb // CHUNK),
        grid=(bsz, RW_WIDTH // n, nblk),
        in_specs=[spec_f] * 6 + [spec_b] * 6,
        out_specs=[spec_f, spec_b],
        out_shape=[out, out],
        scratch_shapes=[state, state],
        compiler_params=pltpu.CompilerParams(
            dimension_semantics=("arbitrary", "arbitrary", "arbitrary"),
            vmem_limit_bytes=VMEM_LIMIT),
        name="rwkv",
    )(r, k, v, kk, a, lw_f, r, k, v, kk, a, lw_b)


def _rwkv_prep_kernel(ps_ref, vf_ref, wup_ref, aup_ref, gup_ref, vup_ref, vec_ref, seg_ref,
                      lwf_ref, lwb_ref, a_ref, k2_ref, kk_ref, v2_ref, g_ref, *, mix):
    w = RW_WIDTH
    k = ps_ref[:, w:2 * w]
    v = ps_ref[:, 2 * w:3 * w]
    lora = ps_ref[:, 3 * w:3 * w + LORA_COLS]
    vec = vec_ref[...]
    for d, out_ref in ((0, lwf_ref), (1, lwb_ref)):
        z = vec[d:d + 1] + _dot(jnp.tanh(lora[:, d * DECAY_LORA:(d + 1) * DECAY_LORA]), wup_ref[d])
        softplus_neg = jnp.maximum(-z, 0.0) + jnp.log(1.0 + jnp.exp(-jnp.abs(z)))
        out_ref[...] = -jnp.exp(-softplus_neg - 0.5)
    off = 2 * DECAY_LORA
    a = jax.nn.sigmoid(vec[2:3] + _dot(lora[:, off:off + AAA_LORA], aup_ref[...]))
    off += AAA_LORA
    g_ref[...] = _dot(jax.nn.sigmoid(lora[:, off:off + GATE_LORA]), gup_ref[...])
    off += GATE_LORA
    if mix:
        v = v + (vf_ref[...] - v) * jax.nn.sigmoid(vec[3:4] + _dot(lora[:, off:LORA_COLS], vup_ref[...]))
    v2_ref[...] = v
    kx = k * vec[4:5]
    sq = kx * kx
    sq_hi = sq.astype(BF16)
    sq_lo = (sq - sq_hi.astype(F32)).astype(BF16)
    norm2 = _dot(sq_hi, seg_ref[...]) + _dot(sq_lo, seg_ref[...])
    kk_ref[...] = kx / jnp.maximum(jnp.sqrt(norm2), 1e-12)
    k2_ref[...] = k * (1.0 + (a - 1.0) * vec[5:6])
    a_ref[...] = a


def _rwkv_prep(ps, v_first, w_up, a_up, g_up, v_up, vec, *, tm=256):
    t = ps.shape[0]
    w = RW_WIDTH
    mix = v_first is not None
    if not mix:
        v_first = ps
        v_up = jnp.zeros((LORA_COLS - 2 * DECAY_LORA - AAA_LORA - GATE_LORA, w), F32)
    head = jnp.arange(w, dtype=jnp.int32) // RW_HEAD
    seg = (head[:, None] == head[None, :]).astype(BF16)
    row = lambda i: (i, 0)
    fixed2 = lambda i: (0, 0)
    full = lambda arr: pl.BlockSpec(arr.shape, (lambda i: (0, 0, 0)) if arr.ndim == 3 else fixed2)
    out = jax.ShapeDtypeStruct((t, w), F32)
    return pl.pallas_call(
        functools.partial(_rwkv_prep_kernel, mix=mix),
        grid=(t // tm,),
        in_specs=[pl.BlockSpec((tm, ps.shape[1]), row), pl.BlockSpec((tm, w), row),
                  full(w_up), full(a_up), full(g_up), full(v_up), full(vec), full(seg)],
        out_specs=[pl.BlockSpec((tm, w), row)] * 7,
        out_shape=[out] * 7,
        compiler_params=pltpu.CompilerParams(
            dimension_semantics=("arbitrary",), vmem_limit_bytes=VMEM_LIMIT),
        name="rwkv_prep",
    )(ps, v_first, w_up, a_up, g_up, v_up, vec, seg)


def _moe_kernel(be_ref, xs_ref, w1_ref, b1_ref, w2_ref, b2_ref, o_ref, w1p_ref, w2b_ref):
    i = pl.program_id(0)
    f2 = w1_ref.shape[-1]
    half = PAIR // 2

    @pl.when((i == 0) | (be_ref[i] != be_ref[jnp.maximum(i - 1, 0)]))
    def _():
        src = lax.broadcasted_iota(jnp.int32, (PAIR, PAIR), 0)
        dst = lax.broadcasted_iota(jnp.int32, (PAIR, PAIR), 1)
        perm = jnp.where(src == jnp.where(dst < half, 2 * dst, 2 * (dst - half) + 1), 1.0, 0.0)
        for j in range(0, f2, PAIR):
            w1p_ref[:, j:j + PAIR] = _dot(w1_ref[:, j:j + PAIR], perm).astype(BF16)
        w2b_ref[...] = w2_ref[...].astype(BF16)

    h = _dot(xs_ref[...], w1p_ref[...]) + b1_ref[...]
    acts = []
    for j in range(0, f2, PAIR):
        glu = jnp.minimum(h[:, j:j + half], SWIGLU_LIMIT)
        lin = jnp.clip(h[:, j + half:j + PAIR], -SWIGLU_LIMIT, SWIGLU_LIMIT)
        acts.append((glu * jax.nn.sigmoid(SWIGLU_ALPHA * glu) * (lin + 1.0)).astype(BF16))
    o_ref[...] = _dot(jnp.concatenate(acts, axis=1), w2b_ref[...]) + b2_ref[...]


def _moe_experts(block_exp, xs, w1, b1, w2, b2, *, layer):
    n_slots, d = xs.shape
    f2 = w1.shape[-1]
    f = f2 // 2
    blk = MOE_BLOCK
    n_blocks = n_slots // blk
    b1p = b1.reshape(DEPTH, N_EXPERTS, f2 // PAIR, PAIR // 2, 2).swapaxes(-1, -2)
    b1p = b1p.reshape(DEPTH, N_EXPERTS, 1, f2)
    wspec = lambda shape: pl.BlockSpec((None, None) + shape, lambda i, be: (layer, be[i], 0, 0))
    return pl.pallas_call(
        _moe_kernel,
        grid_spec=pltpu.PrefetchScalarGridSpec(
            num_scalar_prefetch=1,
            grid=(n_blocks,),
            in_specs=[pl.BlockSpec((blk, d), lambda i, be: (i, 0)),
                      wspec((d, f2)), wspec((1, f2)), wspec((f, d)), wspec((1, d))],
            out_specs=pl.BlockSpec((blk, d), lambda i, be: (i, 0)),
            scratch_shapes=[pltpu.VMEM((d, f2), BF16), pltpu.VMEM((f, d), BF16)]),
        out_shape=jax.ShapeDtypeStruct((n_slots, d), F32),
        compiler_params=pltpu.CompilerParams(
            dimension_semantics=("arbitrary",), vmem_limit_bytes=VMEM_LIMIT),
        name="moe_experts",
    )(block_exp, xs, w1, b1p, w2, b2.reshape(DEPTH, N_EXPERTS, 1, d))


def _ln_rows(y, g, b):
    yc = y - jnp.mean(y, axis=-1, keepdims=True)
    var = jnp.mean(yc * yc, axis=-1, keepdims=True)
    return yc * lax.rsqrt(var + NORM_EPS) * g + b


def _merge_ln_kernel(oa_ref, ob_ref, gate_ref, x_ref, pa_ref, pb_ref, wo_ref, g_ref, b_ref, o_ref):
    d = x_ref.shape[-1]
    merged = (jax.nn.sigmoid(gate_ref[:, :d]) * _dot(oa_ref[...], pa_ref[...])
              + jax.nn.sigmoid(gate_ref[:, d:]) * _dot(ob_ref[...], pb_ref[...]))
    o_ref[...] = _ln_rows(DN_ALPHA * x_ref[...] + _dot(merged, wo_ref[...]), g_ref[...], b_ref[...])


def _merge_ln(o_a, o_b, p_gate, x, proj_a, proj_b, w_out, g, b, *, tm=512):
    t, d = x.shape
    row = lambda i: (i, 0)
    fixed = lambda i: (0, 0)
    wspec = pl.BlockSpec((d, d), fixed)
    return pl.pallas_call(
        _merge_ln_kernel,
        grid=(t // tm,),
        in_specs=[pl.BlockSpec((tm, d), row), pl.BlockSpec((tm, d), row), pl.BlockSpec((tm, 2 * d), row),
                  pl.BlockSpec((tm, d), row), wspec, wspec, wspec,
                  pl.BlockSpec((1, d), fixed), pl.BlockSpec((1, d), fixed)],
        out_specs=pl.BlockSpec((tm, d), row),
        out_shape=jax.ShapeDtypeStruct((t, d), F32),
        compiler_params=pltpu.CompilerParams(
            dimension_semantics=("arbitrary",), vmem_limit_bytes=VMEM_LIMIT),
        name="merge_ln",
    )(o_a, o_b, p_gate, x, proj_a.astype(BF16), proj_b.astype(BF16), w_out.astype(BF16),
      g.reshape(1, d), b.reshape(1, d))


def _combine_ln_kernel(x_ref, yg_ref, gate_ref, g_ref, b_ref, o_ref, ob_ref):
    gate = gate_ref[...]
    moe = gate[:, 0:1] * yg_ref[0]
    for kk in range(1, TOP_K):
        moe = moe + gate[:, kk:kk + 1] * yg_ref[kk]
    out = _ln_rows(DN_ALPHA * x_ref[...] + moe, g_ref[...], b_ref[...])
    o_ref[...] = out
    ob_ref[...] = out.astype(BF16)


def _combine_ln(x, yg, gates, g, b, *, tm=512):
    t, d = x.shape
    row = lambda i: (i, 0)
    fixed = lambda i: (0, 0)
    return pl.pallas_call(
        _combine_ln_kernel,
        grid=(t // tm,),
        in_specs=[pl.BlockSpec((tm, d), row), pl.BlockSpec((TOP_K, tm, d), lambda i: (0, i, 0)),
                  pl.BlockSpec((tm, TOP_K), row),
                  pl.BlockSpec((1, d), fixed), pl.BlockSpec((1, d), fixed)],
        out_specs=[pl.BlockSpec((tm, d), row), pl.BlockSpec((tm, d), row)],
        out_shape=[jax.ShapeDtypeStruct((t, d), F32), jax.ShapeDtypeStruct((t, d), BF16)],
        compiler_params=pltpu.CompilerParams(
            dimension_semantics=("arbitrary",), vmem_limit_bytes=VMEM_LIMIT),
        name="combine_ln",
    )(x, yg, gates, g.reshape(1, d), b.reshape(1, d))


def _rw_heads(t):
    return t.reshape(t.shape[0], RW_HEADS, RW_HEAD)


def _moe(x2, router_w, router_b, w1, b1, w2, b2, *, layer):
    t, d = x2.shape
    n_assign = t * TOP_K
    n_blocks = -(-n_assign // MOE_BLOCK) + N_EXPERTS
    n_slots = n_blocks * MOE_BLOCK
    top_idx, gates, rank, counts = _router(x2, router_w, router_b)
    e_flat = top_idx.reshape(-1)
    order = jnp.argsort(e_flat).astype(jnp.int32)
    experts = jnp.arange(N_EXPERTS, dtype=jnp.int32)
    starts = jnp.cumsum(counts) - counts
    padded = (counts + MOE_BLOCK - 1) // MOE_BLOCK * MOE_BLOCK
    pad_ends = jnp.cumsum(padded)
    pad_starts = pad_ends - padded
    shift = pad_starts - starts
    slot_of = rank.reshape(-1) + jnp.sum(jnp.where(e_flat[:, None] == experts, pad_starts, 0), axis=1)
    block_start = jnp.arange(n_blocks, dtype=jnp.int32) * MOE_BLOCK
    block_exp = jnp.minimum(jnp.sum(pad_ends[None, :] <= block_start[:, None], axis=1),
                            N_EXPERTS - 1).astype(jnp.int32)
    blk_hot = block_exp[:, None] == experts
    per_slot = lambda tab: jnp.repeat(jnp.sum(jnp.where(blk_hot, tab, 0), axis=1), MOE_BLOCK)
    slot = jnp.arange(n_slots, dtype=jnp.int32)
    valid = slot - per_slot(pad_starts) < per_slot(counts)
    src = jnp.clip(slot - per_slot(shift), 0, n_assign - 1)
    slot_tok = jnp.where(valid, order[src] // TOP_K, 0)
    y = _moe_experts(block_exp, x2[slot_tok], w1, b1, w2, b2, layer=layer)
    return y[slot_of.reshape(t, TOP_K).T.reshape(-1)].reshape(TOP_K, t, d), gates


def kernel(x, w_in, hg_lb_logits, hg_norm_w, rw_mu, rw_w0, rw_w_up, rw_a0, rw_a_up, rw_g_up,
           rw_k_k, rw_k_a, rw_r_k, rw_lnx_w, rw_lnx_b, rw_v_down, rw_v_up, rw_v0, proj_a, proj_b,
           w_out, ln1_g, ln1_b, router_w, router_b, moe_w1, moe_b1, moe_w2, moe_b2, ln2_g, ln2_b):
    bsz, s, d = x.shape
    t = bsz * s
    scan = dict(bsz=bsz, seq=s)
    lb_all = jnp.cumsum(jax.nn.softmax(hg_lb_logits.astype(F32), axis=0), axis=0)
    lb_all = lb_all - lb_all[0:1]
    hg_cols = 5 * HG_WIDTH
    rw_cols = 3 * RW_WIDTH + 2 * DECAY_LORA + AAA_LORA + GATE_LORA
    x2 = x.reshape(t, d)
    xb = x2.astype(BF16)
    v_first = None
    for l in range(DEPTH):
        w_l = w_in[l].astype(BF16)
        p_hg = _matmul(xb, w_l[:, :hg_cols], tm=1024, tn=2560)
        p_gate = _matmul(xb, w_l[:, hg_cols + rw_cols:], tm=1024, tn=2048)
        w_rw = w_in[l][:, hg_cols:hg_cols + rw_cols]
        w_self, w_nb = w_rw * (1.0 - rw_mu[l]), w_rw * rw_mu[l]
        if l > 0:
            w_self = jnp.concatenate([w_self, rw_v_down[l - 1]], axis=1)
        pad_cols = lambda w: jnp.pad(w, ((0, 0), (0, RW_PROJ_COLS - w.shape[1]))).astype(BF16)
        x3 = jnp.pad(x2.reshape(bsz, s, d), ((0, 0), (1, 1), (0, 0)))
        x_nb = (0.5 * (x3[:, :-2] + x3[:, 2:])).reshape(t, d).astype(BF16)
        ps = _matmul2(xb, x_nb, pad_cols(w_self), pad_cols(w_nb), tm=2048)

        o_a = _hgrn_scan(p_hg, lb_all[l], o_fwd=_hgrn_scan(p_hg, lb_all[l], **scan),
                         norm_w=hg_norm_w[l], **scan)

        r = ps[:, :RW_WIDTH]
        zeros = jnp.zeros((RW_WIDTH,), F32)
        if l == 0:
            v_first = ps[:, 2 * RW_WIDTH:3 * RW_WIDTH]
            mixing = dict(v_first=None, v_up=None)
            v0 = zeros
        else:
            v_up = rw_v_up[l - 1]
            mixing = dict(v_first=v_first, v_up=jnp.pad(v_up, ((0, RW_PROJ_COLS - rw_cols - v_up.shape[0]), (0, 0))))
            v0 = rw_v0[l - 1]
        vec = jnp.stack([rw_w0[l, 0], rw_w0[l, 1], rw_a0[l], v0, rw_k_k[l], rw_k_a[l], zeros, zeros])
        lw_f, lw_b, a, k, kk, v, g = _rwkv_prep(ps, w_up=rw_w_up[l], a_up=rw_a_up[l], g_up=rw_g_up[l],
                                                vec=vec, **mixing)
        y_f, y_b = _rwkv_scan(ps, k, v, kk, a, lw_f, lw_b, **scan)
        y = _rw_heads(y_f + y_b)
        yc = y - jnp.mean(y, axis=-1, keepdims=True)
        yn = yc * lax.rsqrt(jnp.mean(yc * yc, axis=-1, keepdims=True) + RW_LN_EPS)
        yn = yn * _rw_heads(rw_lnx_w[l][None])[0] + _rw_heads(rw_lnx_b[l][None])[0]
        bonus = jnp.sum(_rw_heads(r) * _rw_heads(k) * rw_r_k[l], axis=-1, keepdims=True) * _rw_heads(v)
        o_b = (yn + bonus).reshape(t, RW_WIDTH) * g

        x2 = _merge_ln(o_a, o_b, p_gate, x2, proj_a[l], proj_b[l], w_out[l], ln1_g[l], ln1_b[l])
        yg, gates = _moe(x2, router_w[l], router_b[l], moe_w1, moe_b1, moe_w2, moe_b2, layer=l)
        x2, xb = _combine_ln(x2, yg, gates, ln2_g[l], ln2_b[l])
    return x2.reshape(bsz, s, d)
```

```python
import functools

import jax
import jax.numpy as jnp
from jax import lax
from jax.experimental import pallas as pl
from jax.experimental.pallas import tpu as pltpu

F32 = jnp.float32
BF16 = jnp.bfloat16

D_MODEL = 1024
DEPTH = 4
HG_HEADS = 8
HG_D = 128
HG_WIDTH = HG_HEADS * HG_D
RW_HEAD = 64
RW_HEADS = D_MODEL // RW_HEAD
RW_WIDTH = D_MODEL
DECAY_LORA = 64
AAA_LORA = 64
GATE_LORA = 128
N_EXPERTS = 32
TOP_K = 4
MOE_BLOCK = 512
SWIGLU_ALPHA = 1.702
SWIGLU_LIMIT = 7.0
NORM_EPS = 1e-5
RW_LN_EPS = 64e-5
DN_ALPHA = (2 * DEPTH) ** 0.25

CHUNK = 64
SUB = 16
DIAG_LOG_RANGE = 60.0
RW_LANES = 256
RW_GROUPS = 2
LORA_COLS = 384
RW_PROJ_COLS = 3 * RW_WIDTH + LORA_COLS
PAIR = 256
LANES = 128
NEG_BIG = -3.0e38
VMEM_LIMIT = 56 * 1024 * 1024
assert CHUNK == RW_HEAD

_NT = (((1,), (1,)), ((), ()))
_TN = (((0,), (0,)), ((), ()))


def _dot(a, b):
    return jnp.dot(a.astype(BF16), b.astype(BF16), preferred_element_type=F32)


def _dot_nt(a, b):
    return lax.dot_general(a.astype(BF16), b.astype(BF16), _NT, preferred_element_type=F32)


def _dot_tn(a, b):
    return lax.dot_general(a.astype(BF16), b.astype(BF16), _TN, preferred_element_type=F32)


def _dot_f32(a, b):
    return jnp.dot(a, b, preferred_element_type=F32, precision=lax.Precision.HIGHEST)


def _mm_kernel(x_ref, w_ref, o_ref):
    o_ref[...] = _dot(x_ref[...], w_ref[...])


def _matmul(x, w, *, tm=512, tn=512):
    m, k = x.shape
    n = w.shape[1]
    tm = min(tm, m)
    tn = min(tn, n)
    assert m % tm == 0 and n % tn == 0
    return pl.pallas_call(
        _mm_kernel,
        grid=(n // tn, m // tm),
        in_specs=[pl.BlockSpec((tm, k), lambda j, i: (i, 0)),
                  pl.BlockSpec((k, tn), lambda j, i: (0, j))],
        out_specs=pl.BlockSpec((tm, tn), lambda j, i: (i, j)),
        out_shape=jax.ShapeDtypeStruct((m, n), F32),
        compiler_params=pltpu.CompilerParams(
            dimension_semantics=("arbitrary", "arbitrary"), vmem_limit_bytes=VMEM_LIMIT),
        name="matmul",
    )(x, w)


def _mm2_kernel(x_ref, y_ref, wx_ref, wy_ref, o_ref):
    o_ref[...] = _dot(x_ref[...], wx_ref[...]) + _dot(y_ref[...], wy_ref[...])


def _matmul2(x, y, wx, wy, *, tm=1024, tn=1152):
    m, k = x.shape
    n = wx.shape[1]
    tm = min(tm, m)
    assert m % tm == 0 and n % tn == 0
    xspec = pl.BlockSpec((tm, k), lambda j, i: (i, 0))
    wspec = pl.BlockSpec((k, tn), lambda j, i: (0, j))
    return pl.pallas_call(
        _mm2_kernel,
        grid=(n // tn, m // tm),
        in_specs=[xspec, xspec, wspec, wspec],
        out_specs=pl.BlockSpec((tm, tn), lambda j, i: (i, j)),
        out_shape=jax.ShapeDtypeStruct((m, n), F32),
        compiler_params=pltpu.CompilerParams(
            dimension_semantics=("arbitrary", "arbitrary"), vmem_limit_bytes=VMEM_LIMIT),
        name="matmul2",
    )(x, y, wx, wy)


def _router_kernel(x_ref, w_ref, b_ref, idx_ref, gate_ref, rank_ref, count_ref):
    @pl.when(pl.program_id(0) == 0)
    def _():
        count_ref[...] = jnp.zeros_like(count_ref)

    logits = _dot_f32(x_ref[...], w_ref[...]) + b_ref[...]
    tm = logits.shape[0]
    lane = lax.broadcasted_iota(jnp.int32, logits.shape, 1).astype(F32)
    cur = logits
    vals, idxs = [], []
    for _ in range(TOP_K):
        top = jnp.max(cur, axis=-1, keepdims=True)
        idx = jnp.min(jnp.where(cur == top, lane, float(LANES)), axis=-1, keepdims=True)
        vals.append(top)
        idxs.append(idx)
        cur = jnp.where(lane == idx, NEG_BIG, cur)
    exps = [jnp.exp(v - vals[0]) for v in vals]
    total = exps[0]
    for e in exps[1:]:
        total = total + e
    chosen = jnp.zeros_like(logits)
    for idx in idxs:
        chosen = jnp.where(lane == idx, 1.0, chosen)
    earlier = (lax.broadcasted_iota(jnp.int32, (tm, tm), 1)
               < lax.broadcasted_iota(jnp.int32, (tm, tm), 0))
    before = count_ref[...] + _dot(jnp.where(earlier, 1.0, 0.0), chosen)
    count_ref[...] = count_ref[...] + jnp.sum(chosen, axis=0, keepdims=True)
    idx_out = jnp.zeros_like(logits)
    gate_out = jnp.zeros_like(logits)
    rank_out = jnp.zeros_like(logits)
    for j in range(TOP_K):
        rank = jnp.sum(jnp.where(lane == idxs[j], before, 0.0), axis=-1, keepdims=True)
        idx_out = jnp.where(lane == float(j), idxs[j], idx_out)
        gate_out = jnp.where(lane == float(j), exps[j] / total, gate_out)
        rank_out = jnp.where(lane == float(j), rank, rank_out)
    idx_ref[...] = idx_out.astype(jnp.int32)
    gate_ref[...] = gate_out
    rank_ref[...] = rank_out.astype(jnp.int32)


def _router(x, w, b, *, tm=512):
    m, k = x.shape
    pad = LANES - N_EXPERTS
    w = jnp.pad(w, ((0, 0), (0, pad)))
    b = jnp.pad(b, (0, pad), constant_values=NEG_BIG).reshape(1, LANES)
    row = pl.BlockSpec((tm, LANES), lambda i: (i, 0))
    fixed = pl.BlockSpec((1, LANES), lambda i: (0, 0))
    per_token = lambda dtype: jax.ShapeDtypeStruct((m, LANES), dtype)
    idx, gates, rank, counts = pl.pallas_call(
        _router_kernel,
        grid=(m // tm,),
        in_specs=[pl.BlockSpec((tm, k), lambda i: (i, 0)), pl.BlockSpec((k, LANES), lambda i: (0, 0)), fixed],
        out_specs=[row, row, row, fixed],
        out_shape=[per_token(jnp.int32), per_token(F32), per_token(jnp.int32),
                   jax.ShapeDtypeStruct((1, LANES), F32)],
        compiler_params=pltpu.CompilerParams(
            dimension_semantics=("arbitrary",), vmem_limit_bytes=VMEM_LIMIT),
        name="router",
    )(x, w, b)
    return (idx[:, :TOP_K], gates[:, :TOP_K], rank[:, :TOP_K],
            counts[0, :N_EXPERTS].astype(jnp.int32))


def _hgrn_kernel(q_ref, z_ref, v_ref, lb_ref, *rest, reverse, n_chunks):
    st_ref = rest[-1]

    @pl.when(pl.program_id(2) == 0)
    def _():
        st_ref[...] = jnp.zeros_like(st_ref)

    lb = lb_ref[...]
    f = lb + (1.0 - lb) * jax.nn.sigmoid(z_ref[...])
    lf = jnp.log(f)
    sub_total = jnp.sum(lf.reshape(lf.shape[0] // SUB, SUB, HG_D), axis=1)
    factorable = jnp.min(sub_total) > -DIAG_LOG_RANGE
    args = (q_ref, f, lf, v_ref) + rest

    @pl.when(factorable)
    def _():
        _hgrn_block(*args, reverse=reverse, n_chunks=n_chunks, pairwise=False)

    @pl.when(jnp.logical_not(factorable))
    def _():
        _hgrn_block(*args, reverse=reverse, n_chunks=n_chunks, pairwise=True)


def _hgrn_block(q_ref, f_blk, lf_blk, v_ref, *rest, reverse, n_chunks, pairwise):
    if reverse:
        of_ref, og_ref, nw_ref, o_ref, st_ref = rest
    else:
        o_ref, st_ref = rest
    c = CHUNK
    row = lax.broadcasted_iota(jnp.int32, (c, c), 0)
    col = lax.broadcasted_iota(jnp.int32, (c, c), 1)
    causal = (col >= row) if reverse else (col <= row)
    tri = jnp.where(causal, 1.0, 0.0).astype(F32)
    diag_mask = causal & (row // SUB == col // SUB)
    srow = lax.broadcasted_iota(jnp.int32, (SUB, 1), 0)

    splits = []
    size = c
    while size > SUB:
        for lo in range(0, c, size):
            splits.append((lo, lo + size // 2, lo + size))
        size //= 2

    order = list(range(n_chunks - 1, -1, -1) if reverse else range(n_chunks))
    chunks = [dict(rows=slice(ci * c, (ci + 1) * c)) for ci in order]
    for ch in chunks:
        rows = ch["rows"]
        ch.update(lf=lf_blk[rows], kk=1.0 - f_blk[rows], q=q_ref[rows, :], v=v_ref[rows, :])
        ch["b"] = _dot_f32(tri, ch["lf"])
    for ch in chunks:
        q, kk, v, b, lf = ch["q"], ch["kk"], ch["v"], ch["b"], ch["lf"]
        b_end = b[0:1, :] if reverse else b[c - 1:c, :]
        ch["qdec"] = (q * jnp.exp(b)).astype(BF16)
        ch["dec"] = jnp.exp(b_end)
        ch["kv"] = _dot_tn(v, kk * jnp.exp(b_end - b))
        scores = []
        for lo, mid, hi in splits:
            if reverse:
                anc = b[mid:mid + 1, :]
                qs, ks = slice(lo, mid), slice(mid, hi)
            else:
                anc = b[mid - 1:mid, :]
                qs, ks = slice(mid, hi), slice(lo, mid)
            scores.append((qs, ks, _dot_nt(q[qs] * jnp.exp(b[qs] - anc), kk[ks] * jnp.exp(anc - b[ks]))))
        ch["scores"] = scores
        if not pairwise:
            edge = b - lf
            anc = jnp.concatenate(
                [jnp.broadcast_to(edge[d0 + SUB - 1:d0 + SUB] if reverse else edge[d0:d0 + 1], (SUB, HG_D))
                 for d0 in range(0, c, SUB)], axis=0)
            ch["diag"] = jnp.where(diag_mask, _dot_nt(q * jnp.exp(b - anc), kk * jnp.exp(anc - b)), 0.0)
    for ch in chunks:
        v = ch["v"]
        parts = [None] * (c // SUB)
        for qs, ks, s in ch["scores"]:
            contrib = _dot(s, v[ks])
            for j in range(qs.start // SUB, qs.stop // SUB):
                piece = contrib[j * SUB - qs.start:(j + 1) * SUB - qs.start]
                parts[j] = piece if parts[j] is None else parts[j] + piece
        intra = jnp.concatenate([jnp.zeros((SUB, HG_D), F32) if p is None else p for p in parts], axis=0)
        if pairwise:
            accs = []
            for d0 in range(0, c, SUB):
                qb, bb, kb, vb = (t[d0:d0 + SUB] for t in (ch["q"], ch["b"], ch["kk"], v))
                acc = jnp.zeros((SUB, HG_D), F32)
                for s in range(SUB):
                    mask = (srow <= s) if reverse else (srow >= s)
                    e = jnp.where(mask, jnp.exp(jnp.minimum(bb - bb[s:s + 1, :], 0.0)), 0.0)
                    w = jnp.sum(qb * kb[s:s + 1, :] * e, axis=-1, keepdims=True)
                    acc = acc + w * vb[s:s + 1, :]
                accs.append(acc)
            ch["intra"] = intra + jnp.concatenate(accs, axis=0)
        else:
            ch["intra"] = intra + _dot(ch["diag"], v)
    st = st_ref[...]
    for ch in chunks:
        ch["st"] = st
        st = st * ch["dec"] + ch["kv"]
    st_ref[...] = st
    for ch in chunks:
        rows = ch["rows"]
        o = ch["intra"] + _dot_nt(ch["qdec"], ch["st"])
        if reverse:
            o = o + of_ref[rows, :]
            o = o * lax.rsqrt(jnp.mean(o * o, axis=-1, keepdims=True) + NORM_EPS) * nw_ref[...]
            o = o * jax.nn.silu(og_ref[rows, :])
        o_ref[rows, :] = o


def _hgrn_scan(p_hg, lb, *, bsz, seq, o_fwd=None, norm_w=None, tb=2048):
    reverse = o_fwd is not None
    t = bsz * seq
    tb = min(tb, seq)
    nblk = seq // tb
    h = HG_HEADS
    zoff = 2 * h if reverse else h

    def col(off):
        return pl.BlockSpec((tb, HG_D),
                            lambda b, hh, i: (b * nblk + (nblk - 1 - i if reverse else i), off + hh))

    vec = pl.BlockSpec((1, HG_D), lambda b, hh, i: (0, 0))
    in_specs = [col(0), col(zoff), col(3 * h), pl.BlockSpec((1, HG_D), lambda b, hh, i: (0, hh))]
    args = [p_hg, p_hg, p_hg, lb.reshape(1, -1)]
    if reverse:
        in_specs += [col(0), col(4 * h), vec]
        args += [o_fwd, p_hg, norm_w.reshape(1, -1)]
    return pl.pallas_call(
        functools.partial(_hgrn_kernel, reverse=reverse, n_chunks=tb // CHUNK),
        grid=(bsz, h, nblk),
        in_specs=in_specs,
        out_specs=col(0),
        out_shape=jax.ShapeDtypeStruct((t, h * HG_D), F32),
        scratch_shapes=[pltpu.VMEM((HG_D, HG_D), F32)],
        compiler_params=pltpu.CompilerParams(
            dimension_semantics=("arbitrary", "arbitrary", "arbitrary"),
            vmem_limit_bytes=VMEM_LIMIT),
        name="hgrn_bwd" if reverse else "hgrn_fwd",
    )(*args)


class _RwkvMasks:
    def __init__(self, reverse):
        c, n = CHUNK, RW_LANES
        m = (n // RW_HEAD) * c
        row = lax.broadcasted_iota(jnp.int32, (c, c), 0)
        col = lax.broadcasted_iota(jnp.int32, (c, c), 1)
        self.tri = jnp.where((col >= row) if reverse else (col <= row), 1.0, 0.0).astype(F32)
        self.stack = (lax.broadcasted_iota(jnp.int32, (m, n), 0) // c
                      == lax.broadcasted_iota(jnp.int32, (m, n), 1) // RW_HEAD)
        tr = lax.broadcasted_iota(jnp.int32, (c, m), 0)
        tc = lax.broadcasted_iota(jnp.int32, (c, m), 1) % c
        self.strict = (tc > tr) if reverse else (tc < tr)
        self.incl = (tc >= tr) if reverse else (tc <= tr)
        self.eye = jnp.where(tc == tr, 1.0, 0.0).astype(F32)
        self.diag = (lax.broadcasted_iota(jnp.int32, (n, n), 0) // RW_HEAD
                     == lax.broadcasted_iota(jnp.int32, (n, n), 1) // RW_HEAD)


def _rwkv_stack(t, masks):
    return jnp.where(masks.stack, jnp.concatenate([t] * (RW_LANES // RW_HEAD), axis=0), 0.0).astype(BF16)


def _lockstep(gens):
    gens = list(gens)
    while gens:
        alive = []
        for gen in gens:
            try:
                next(gen)
                alive.append(gen)
            except StopIteration:
                pass
        gens = alive


def _rwkv_prepare(p, refs, rows, lanes, masks, reverse):
    c = CHUNK
    m = (RW_LANES // RW_HEAD) * c
    stack = functools.partial(_rwkv_stack, masks=masks)
    r, k, v, kk, a, lw = (ref[rows, lanes] for ref in refs)
    lp = _dot_f32(masks.tri, lw)
    yield
    lp_end = lp[0:1, :] if reverse else lp[c - 1:c, :]
    pinv = jnp.exp(-lp)
    dec = jnp.exp(lp_end - lp)
    kb = kk * a
    ar = jnp.concatenate([-kk * jnp.exp(lp - lw), r * jnp.exp(lp)], axis=0).astype(BF16)
    bk = jnp.concatenate([stack(kb * pinv), stack(k * pinv)], axis=0)
    v_s = stack(v)
    sc = _dot_nt(ar, bk)
    yield
    ab = jnp.where(masks.strict, sc[:c, :m], 0.0)
    akrk = jnp.concatenate([jnp.where(masks.strict, sc[:c, m:], 0.0),
                            jnp.where(masks.incl, sc[c:, m:], 0.0)], axis=0)
    p.update(ar=ar, v=v.astype(BF16), akrk_v=_dot(akrk, v_s),
             rb=jnp.where(masks.incl, sc[c:, :m], 0.0).astype(BF16),
             pw=_dot(ab, stack(ab)), tinv=masks.eye + ab,
             bkp=jnp.concatenate([kb * dec, k * dec], axis=0).astype(BF16),
             gdec=jnp.exp(lp_end))


def _rwkv_double(p, masks, last):
    c = CHUNK
    pw_s = _rwkv_stack(p["pw"], masks)
    if last:
        p["tinv"] = p["tinv"] + _dot(p["tinv"], pw_s)
    else:
        z = _dot(jnp.concatenate([p["pw"], p["tinv"]], axis=0), pw_s)
        p["pw"], p["tinv"] = z[:c], p["tinv"] + z[c:]


def _rwkv_chain(preps, g_ref, y_ref, rows_seq, lanes, masks):
    c = CHUNK
    for p, rows in zip(preps, rows_seq):
        g = g_ref[...]
        arg = _dot_nt(p["ar"], g)
        yield
        u = _dot(p["tinv"], _rwkv_stack(arg[:c] + p["akrk_v"][:c], masks))
        yield
        y_ref[rows, lanes] = arg[c:] + p["akrk_v"][c:] + _dot(p["rb"], _rwkv_stack(u, masks))
        uv = jnp.concatenate([u.astype(BF16), p["v"]], axis=0)
        g_ref[...] = g * p["gdec"] + jnp.where(masks.diag, _dot_tn(uv, p["bkp"]), 0.0)
        yield


def _rwkv_kernel(*refs, n_chunks):
    in_f, in_b = refs[0:6], refs[6:12]
    y_f, y_b, g_f, g_b = refs[12:16]

    @pl.when(pl.program_id(2) == 0)
    def _():
        g_f[...] = jnp.zeros_like(g_f)
        g_b[...] = jnp.zeros_like(g_b)

    c = CHUNK
    rows = [slice(ci * c, (ci + 1) * c) for ci in range(n_chunks)]
    masks = (_RwkvMasks(False), _RwkvMasks(True))
    plan = []
    for gi in range(RW_GROUPS):
        lanes = slice(gi * RW_LANES, (gi + 1) * RW_LANES)
        plan.append((in_f, y_f, g_f.at[gi], masks[0], False, rows, lanes))
        plan.append((in_b, y_b, g_b.at[gi], masks[1], True, rows[::-1], lanes))
    preps = [[{} for _ in rows] for _ in plan]
    _lockstep(_rwkv_prepare(p, ins, rw, lanes, mk, rev)
              for (ins, _, _, mk, rev, rws, lanes), plist in zip(plan, preps)
              for p, rw in zip(plist, rws))
    n_double = c.bit_length() - 2
    for it in range(n_double):
        for chain, plist in zip(plan, preps):
            for p in plist:
                _rwkv_double(p, chain[3], last=it == n_double - 1)
    _lockstep(_rwkv_chain(plist, g_ref, y_ref, rws, lanes, mk)
              for (_, y_ref, g_ref, mk, _, rws, lanes), plist in zip(plan, preps))


def _rwkv_scan(r, k, v, kk, a, lw_f, lw_b, *, bsz, seq, tb=512):
    t = bsz * seq
    tb = min(tb, seq)
    nblk = seq // tb
    n = RW_LANES * RW_GROUPS
    spec_f = pl.BlockSpec((tb, n), lambda b, hh, i: (b * nblk + i, hh))
    spec_b = pl.BlockSpec((tb, n), lambda b, hh, i: (b * nblk + nblk - 1 - i, hh))
    out = jax.ShapeDtypeStruct((t, RW_WIDTH), F32)
    state = pltpu.VMEM((RW_GROUPS, RW_LANES, RW_LANES), F32)
    return pl.pallas_call(
        functools.partial(_rwkv_kernel, n_chunks=tb // CHUNK),
        grid=(bsz, RW_WIDTH // n, nblk),
        in_specs=[spec_f] * 6 + [spec_b] * 6,
        out_specs=[spec_f, spec_b],
        out_shape=[out, out],
        scratch_shapes=[state, state],
        compiler_params=pltpu.CompilerParams(
            dimension_semantics=("arbitrary", "arbitrary", "arbitrary"),
            vmem_limit_bytes=VMEM_LIMIT),
        name="rwkv",
    )(r, k, v, kk, a, lw_f, r, k, v, kk, a, lw_b)


def _rwkv_prep_kernel(ps_ref, vf_ref, wup_ref, aup_ref, gup_ref, vup_ref, vec_ref, seg_ref,
                      lwf_ref, lwb_ref, a_ref, k2_ref, kk_ref, v2_ref, g_ref, *, mix):
    w = RW_WIDTH
    k = ps_ref[:, w:2 * w]
    v = ps_ref[:, 2 * w:3 * w]
    lora = ps_ref[:, 3 * w:3 * w + LORA_COLS]
    vec = vec_ref[...]
    for d, out_ref in ((0, lwf_ref), (1, lwb_ref)):
        z = vec[d:d + 1] + _dot(jnp.tanh(lora[:, d * DECAY_LORA:(d + 1) * DECAY_LORA]), wup_ref[d])
        softplus_neg = jnp.maximum(-z, 0.0) + jnp.log(1.0 + jnp.exp(-jnp.abs(z)))
        out_ref[...] = -jnp.exp(-softplus_neg - 0.5)
    off = 2 * DECAY_LORA
    a = jax.nn.sigmoid(vec[2:3] + _dot(lora[:, off:off + AAA_LORA], aup_ref[...]))
    off += AAA_LORA
    g_ref[...] = _dot(jax.nn.sigmoid(lora[:, off:off + GATE_LORA]), gup_ref[...])
    off += GATE_LORA
    if mix:
        v = v + (vf_ref[...] - v) * jax.nn.sigmoid(vec[3:4] + _dot(lora[:, off:LORA_COLS], vup_ref[...]))
    v2_ref[...] = v
    kx = k * vec[4:5]
    sq = kx * kx
    sq_hi = sq.astype(BF16)
    sq_lo = (sq - sq_hi.astype(F32)).astype(BF16)
    norm2 = _dot(sq_hi, seg_ref[...]) + _dot(sq_lo, seg_ref[...])
    kk_ref[...] = kx / jnp.maximum(jnp.sqrt(norm2), 1e-12)
    k2_ref[...] = k * (1.0 + (a - 1.0) * vec[5:6])
    a_ref[...] = a


def _rwkv_prep(ps, v_first, w_up, a_up, g_up, v_up, vec, *, tm=256):
    t = ps.shape[0]
    w = RW_WIDTH
    mix = v_first is not None
    if not mix:
        v_first = ps
        v_up = jnp.zeros((LORA_COLS - 2 * DECAY_LORA - AAA_LORA - GATE_LORA, w), F32)
    head = jnp.arange(w, dtype=jnp.int32) // RW_HEAD
    seg = (head[:, None] == head[None, :]).astype(BF16)
    row = lambda i: (i, 0)
    fixed2 = lambda i: (0, 0)
    full = lambda arr: pl.BlockSpec(arr.shape, (lambda i: (0, 0, 0)) if arr.ndim == 3 else fixed2)
    out = jax.ShapeDtypeStruct((t, w), F32)
    return pl.pallas_call(
        functools.partial(_rwkv_prep_kernel, mix=mix),
        grid=(t // tm,),
        in_specs=[pl.BlockSpec((tm, ps.shape[1]), row), pl.BlockSpec((tm, w), row),
                  full(w_up), full(a_up), full(g_up), full(v_up), full(vec), full(seg)],
        out_specs=[pl.BlockSpec((tm, w), row)] * 7,
        out_shape=[out] * 7,
        compiler_params=pltpu.CompilerParams(
            dimension_semantics=("arbitrary",), vmem_limit_bytes=VMEM_LIMIT),
        name="rwkv_prep",
    )(ps, v_first, w_up, a_up, g_up, v_up, vec, seg)


def _moe_kernel(be_ref, xs_ref, w1_ref, b1_ref, w2_ref, b2_ref, o_ref, w1p_ref, w2b_ref):
    i = pl.program_id(0)
    f2 = w1_ref.shape[-1]
    half = PAIR // 2

    @pl.when((i == 0) | (be_ref[i] != be_ref[jnp.maximum(i - 1, 0)]))
    def _():
        src = lax.broadcasted_iota(jnp.int32, (PAIR, PAIR), 0)
        dst = lax.broadcasted_iota(jnp.int32, (PAIR, PAIR), 1)
        perm = jnp.where(src == jnp.where(dst < half, 2 * dst, 2 * (dst - half) + 1), 1.0, 0.0)
        for j in range(0, f2, PAIR):
            w1p_ref[:, j:j + PAIR] = _dot(w1_ref[:, j:j + PAIR], perm).astype(BF16)
        w2b_ref[...] = w2_ref[...].astype(BF16)

    h = _dot(xs_ref[...], w1p_ref[...]) + b1_ref[...]
    acts = []
    for j in range(0, f2, PAIR):
        glu = jnp.minimum(h[:, j:j + half], SWIGLU_LIMIT)
        lin = jnp.clip(h[:, j + half:j + PAIR], -SWIGLU_LIMIT, SWIGLU_LIMIT)
        acts.append((glu * jax.nn.sigmoid(SWIGLU_ALPHA * glu) * (lin + 1.0)).astype(BF16))
    o_ref[...] = _dot(jnp.concatenate(acts, axis=1), w2b_ref[...]) + b2_ref[...]


def _moe_experts(block_exp, xs, w1, b1, w2, b2, *, layer):
    n_slots, d = xs.shape
    f2 = w1.shape[-1]
    f = f2 // 2
    blk = MOE_BLOCK
    n_blocks = n_slots // blk
    b1p = b1.reshape(DEPTH, N_EXPERTS, f2 // PAIR, PAIR // 2, 2).swapaxes(-1, -2)
    b1p = b1p.reshape(DEPTH, N_EXPERTS, 1, f2)
    wspec = lambda shape: pl.BlockSpec((None, None) + shape, lambda i, be: (layer, be[i], 0, 0))
    return pl.pallas_call(
        _moe_kernel,
        grid_spec=pltpu.PrefetchScalarGridSpec(
            num_scalar_prefetch=1,
            grid=(n_blocks,),
            in_specs=[pl.BlockSpec((blk, d), lambda i, be: (i, 0)),
                      wspec((d, f2)), wspec((1, f2)), wspec((f, d)), wspec((1, d))],
            out_specs=pl.BlockSpec((blk, d), lambda i, be: (i, 0)),
            scratch_shapes=[pltpu.VMEM((d, f2), BF16), pltpu.VMEM((f, d), BF16)]),
        out_shape=jax.ShapeDtypeStruct((n_slots, d), F32),
        compiler_params=pltpu.CompilerParams(
            dimension_semantics=("arbitrary",), vmem_limit_bytes=VMEM_LIMIT),
        name="moe_experts",
    )(block_exp, xs, w1, b1p, w2, b2.reshape(DEPTH, N_EXPERTS, 1, d))


def _ln_rows(y, g, b):
    yc = y - jnp.mean(y, axis=-1, keepdims=True)
    var = jnp.mean(yc * yc, axis=-1, keepdims=True)
    return yc * lax.rsqrt(var + NORM_EPS) * g + b


def _merge_ln_kernel(oa_ref, ob_ref, gate_ref, x_ref, pa_ref, pb_ref, wo_ref, g_ref, b_ref, o_ref):
    d = x_ref.shape[-1]
    merged = (jax.nn.sigmoid(gate_ref[:, :d]) * _dot(oa_ref[...], pa_ref[...])
              + jax.nn.sigmoid(gate_ref[:, d:]) * _dot(ob_ref[...], pb_ref[...]))
    o_ref[...] = _ln_rows(DN_ALPHA * x_ref[...] + _dot(merged, wo_ref[...]), g_ref[...], b_ref[...])


def _merge_ln(o_a, o_b, p_gate, x, proj_a, proj_b, w_out, g, b, *, tm=512):
    t, d = x.shape
    row = lambda i: (i, 0)
    fixed = lambda i: (0, 0)
    wspec = pl.BlockSpec((d, d), fixed)
    return pl.pallas_call(
        _merge_ln_kernel,
        grid=(t // tm,),
        in_specs=[pl.BlockSpec((tm, d), row), pl.BlockSpec((tm, d), row), pl.BlockSpec((tm, 2 * d), row),
                  pl.BlockSpec((tm, d), row), wspec, wspec, wspec,
                  pl.BlockSpec((1, d), fixed), pl.BlockSpec((1, d), fixed)],
        out_specs=pl.BlockSpec((tm, d), row),
        out_shape=jax.ShapeDtypeStruct((t, d), F32),
        compiler_params=pltpu.CompilerParams(
            dimension_semantics=("arbitrary",), vmem_limit_bytes=VMEM_LIMIT),
        name="merge_ln",
    )(o_a, o_b, p_gate, x, proj_a.astype(BF16), proj_b.astype(BF16), w_out.astype(BF16),
      g.reshape(1, d), b.reshape(1, d))


def _combine_ln_kernel(x_ref, yg_ref, gate_ref, g_ref, b_ref, o_ref, ob_ref):
    gate = gate_ref[...]
    moe = gate[:, 0:1] * yg_ref[0]
    for kk in range(1, TOP_K):
        moe = moe + gate[:, kk:kk + 1] * yg_ref[kk]
    out = _ln_rows(DN_ALPHA * x_ref[...] + moe, g_ref[...], b_ref[...])
    o_ref[...] = out
    ob_ref[...] = out.astype(BF16)


def _combine_ln(x, yg, gates, g, b, *, tm=512):
    t, d = x.shape
    row = lambda i: (i, 0)
    fixed = lambda i: (0, 0)
    return pl.pallas_call(
        _combine_ln_kernel,
        grid=(t // tm,),
        in_specs=[pl.BlockSpec((tm, d), row), pl.BlockSpec((TOP_K, tm, d), lambda i: (0, i, 0)),
                  pl.BlockSpec((tm, TOP_K), row),
                  pl.BlockSpec((1, d), fixed), pl.BlockSpec((1, d), fixed)],
        out_specs=[pl.BlockSpec((tm, d), row), pl.BlockSpec((tm, d), row)],
        out_shape=[jax.ShapeDtypeStruct((t, d), F32), jax.ShapeDtypeStruct((t, d), BF16)],
        compiler_params=pltpu.CompilerParams(
            dimension_semantics=("arbitrary",), vmem_limit_bytes=VMEM_LIMIT),
        name="combine_ln",
    )(x, yg, gates, g.reshape(1, d), b.reshape(1, d))


def _rw_heads(t):
    return t.reshape(t.shape[0], RW_HEADS, RW_HEAD)


def _moe(x2, router_w, router_b, w1, b1, w2, b2, *, layer):
    t, d = x2.shape
    n_assign = t * TOP_K
    n_blocks = -(-n_assign // MOE_BLOCK) + N_EXPERTS
    n_slots = n_blocks * MOE_BLOCK
    top_idx, gates, rank, counts = _router(x2, router_w, router_b)
    e_flat = top_idx.reshape(-1)
    order = jnp.argsort(e_flat).astype(jnp.int32)
    experts = jnp.arange(N_EXPERTS, dtype=jnp.int32)
    starts = jnp.cumsum(counts) - counts
    padded = (counts + MOE_BLOCK - 1) // MOE_BLOCK * MOE_BLOCK
    pad_ends = jnp.cumsum(padded)
    pad_starts = pad_ends - padded
    shift = pad_starts - starts
    slot_of = rank.reshape(-1) + jnp.sum(jnp.where(e_flat[:, None] == experts, pad_starts, 0), axis=1)
    block_start = jnp.arange(n_blocks, dtype=jnp.int32) * MOE_BLOCK
    block_exp = jnp.minimum(jnp.sum(pad_ends[None, :] <= block_start[:, None], axis=1),
                            N_EXPERTS - 1).astype(jnp.int32)
    blk_hot = block_exp[:, None] == experts
    per_slot = lambda tab: jnp.repeat(jnp.sum(jnp.where(blk_hot, tab, 0), axis=1), MOE_BLOCK)
    slot = jnp.arange(n_slots, dtype=jnp.int32)
    valid = slot - per_slot(pad_starts) < per_slot(counts)
    src = jnp.clip(slot - per_slot(shift), 0, n_assign - 1)
    slot_tok = jnp.where(valid, order[src] // TOP_K, 0)
    y = _moe_experts(block_exp, x2[slot_tok], w1, b1, w2, b2, layer=layer)
    return y[slot_of.reshape(t, TOP_K).T.reshape(-1)].reshape(TOP_K, t, d), gates


def kernel(x, w_in, hg_lb_logits, hg_norm_w, rw_mu, rw_w0, rw_w_up, rw_a0, rw_a_up, rw_g_up,
           rw_k_k, rw_k_a, rw_r_k, rw_lnx_w, rw_lnx_b, rw_v_down, rw_v_up, rw_v0, proj_a, proj_b,
           w_out, ln1_g, ln1_b, router_w, router_b, moe_w1, moe_b1, moe_w2, moe_b2, ln2_g, ln2_b):
    bsz, s, d = x.shape
    t = bsz * s
    scan = dict(bsz=bsz, seq=s)
    lb_all = jnp.cumsum(jax.nn.softmax(hg_lb_logits.astype(F32), axis=0), axis=0)
    lb_all = lb_all - lb_all[0:1]
    hg_cols = 5 * HG_WIDTH
    rw_cols = 3 * RW_WIDTH + 2 * DECAY_LORA + AAA_LORA + GATE_LORA
    x2 = x.reshape(t, d)
    xb = x2.astype(BF16)
    v_first = None
    for l in range(DEPTH):
        w_l = w_in[l].astype(BF16)
        p_hg = _matmul(xb, w_l[:, :hg_cols], tm=1024, tn=2560)
        p_gate = _matmul(xb, w_l[:, hg_cols + rw_cols:], tm=1024, tn=2048)
        w_rw = w_in[l][:, hg_cols:hg_cols + rw_cols]
        w_self, w_nb = w_rw * (1.0 - rw_mu[l]), w_rw * rw_mu[l]
        if l > 0:
            w_self = jnp.concatenate([w_self, rw_v_down[l - 1]], axis=1)
        pad_cols = lambda w: jnp.pad(w, ((0, 0), (0, RW_PROJ_COLS - w.shape[1]))).astype(BF16)
        x3 = jnp.pad(x2.reshape(bsz, s, d), ((0, 0), (1, 1), (0, 0)))
        x_nb = (0.5 * (x3[:, :-2] + x3[:, 2:])).reshape(t, d).astype(BF16)
        ps = _matmul2(xb, x_nb, pad_cols(w_self), pad_cols(w_nb), tm=2048)

        o_a = _hgrn_scan(p_hg, lb_all[l], o_fwd=_hgrn_scan(p_hg, lb_all[l], **scan),
                         norm_w=hg_norm_w[l], **scan)

        r = ps[:, :RW_WIDTH]
        zeros = jnp.zeros((RW_WIDTH,), F32)
        if l == 0:
            v_first = ps[:, 2 * RW_WIDTH:3 * RW_WIDTH]
            mixing = dict(v_first=None, v_up=None)
            v0 = zeros
        else:
            v_up = rw_v_up[l - 1]
            mixing = dict(v_first=v_first, v_up=jnp.pad(v_up, ((0, RW_PROJ_COLS - rw_cols - v_up.shape[0]), (0, 0))))
            v0 = rw_v0[l - 1]
        vec = jnp.stack([rw_w0[l, 0], rw_w0[l, 1], rw_a0[l], v0, rw_k_k[l], rw_k_a[l], zeros, zeros])
        lw_f, lw_b, a, k, kk, v, g = _rwkv_prep(ps, w_up=rw_w_up[l], a_up=rw_a_up[l], g_up=rw_g_up[l],
                                                vec=vec, **mixing)
        y_f, y_b = _rwkv_scan(ps, k, v, kk, a, lw_f, lw_b, **scan)
        y = _rw_heads(y_f + y_b)
        yc = y - jnp.mean(y, axis=-1, keepdims=True)
        yn = yc * lax.rsqrt(jnp.mean(yc * yc, axis=-1, keepdims=True) + RW_LN_EPS)
        yn = yn * _rw_heads(rw_lnx_w[l][None])[0] + _rw_heads(rw_lnx_b[l][None])[0]
        bonus = jnp.sum(_rw_heads(r) * _rw_heads(k) * rw_r_k[l], axis=-1, keepdims=True) * _rw_heads(v)
        o_b = (yn + bonus).reshape(t, RW_WIDTH) * g

        x2 = _merge_ln(o_a, o_b, p_gate, x2, proj_a[l], proj_b[l], w_out[l], ln1_g[l], ln1_b[l])
        yg, gates = _moe(x2, router_w[l], router_b[l], moe_w1, moe_b1, moe_w2, moe_b2, layer=l)
        x2, xb = _combine_ln(x2, yg, gates, ln2_g[l], ln2_b[l])
    return x2.reshape(bsz, s, d)
```
